```python
import jax, jax.numpy as jnp
from jax import lax
import numpy as np


D_MODEL = 1024
BATCH = 8
SEQ = 2048
DEPTH = 2

D_MIX = D_MODEL
ATTN_WIDTH = D_MIX // 2
ATTN_HEAD_DIM = 64
ATTN_HEADS = ATTN_WIDTH // ATTN_HEAD_DIM
DILATED_PATTERNS = ((128, 1), (512, 4), (2048, 16))
ROPE_THETA = 10000.0
MLSTM_WIDTH = D_MIX - ATTN_WIDTH
MLSTM_HEADS = 4
MLSTM_HEAD_DIM = MLSTM_WIDTH // MLSTM_HEADS
MLSTM_CHUNK = 128
MLSTM_CONV = 5
PROJ_COLS = 3 * ATTN_WIDTH + 4 * MLSTM_WIDTH + 4 * MLSTM_HEADS
D_FF = ((8 * D_MODEL // 3 + 255) // 256) * 256
FFN_CONV = 3
NORM_EPS = 1e-6
NEG_INF = -1e30

kernel_name = 'hybrid_dilated_attn_mlstm_convffn'


def rms_norm(x, g):
    xf = x.astype(jnp.float32)
    y = xf * lax.rsqrt(jnp.mean(xf * xf, axis=-1, keepdims=True) + NORM_EPS)
    return (y * g.astype(jnp.float32)).astype(x.dtype)


def depthwise_conv_centred(x, w, b):
    k = w.shape[0]
    pad = k // 2
    s = x.shape[1]
    xp = jnp.pad(x, ((0, 0), (pad, pad), (0, 0)))
    out = xp[:, 0:s] * w[0]
    for j in range(1, k):
        out = out + xp[:, j:j + s] * w[j]
    return out + b


def apply_rotary(t):
    s, dh = t.shape[2], t.shape[3]
    inv_freq = ROPE_THETA ** (-jnp.arange(0, dh, 2, dtype=jnp.float32) / dh)
    ang = jnp.arange(s, dtype=jnp.float32)[:, None] * inv_freq[None, :]
    cos, sin = jnp.cos(ang), jnp.sin(ang)
    t1, t2 = t[..., :dh // 2], t[..., dh // 2:]
    return jnp.concatenate([t1 * cos - t2 * sin, t2 * cos + t1 * sin], axis=-1)


def dilated_window_branch(q, k, v, window, dilation):
    b, h, s, dh = q.shape
    half = window // (2 * dilation)
    blk = half
    n_sub = s // dilation
    nb = -(-n_sub // blk)
    lp = nb * blk

    def to_sub(t):
        return jnp.swapaxes(t.reshape(b, h, n_sub, dilation, dh), 2, 3)

    qs = jnp.pad(to_sub(q), ((0, 0), (0, 0), (0, 0), (0, lp - n_sub), (0, 0)))
    kpad = ((0, 0), (0, 0), (0, 0), (blk, lp - n_sub + blk), (0, 0))
    ks = jnp.pad(to_sub(k), kpad)
    vs = jnp.pad(to_sub(v), kpad)

    def band(t):
        return jnp.concatenate(
            [t[:, :, :, i * blk:i * blk + lp].reshape(b, h, dilation, nb, blk, dh) for i in range(3)], axis=4)

    qb = qs.reshape(b, h, dilation, nb, blk, dh)
    kb, vb = band(ks), band(vs)
    scores = jnp.einsum('bhrnqe,bhrnke->bhrnqk', qb, kb) * (dh ** -0.5)
    qi = jnp.arange(nb)[:, None] * blk + jnp.arange(blk)[None, :]
    ki = jnp.arange(nb)[:, None] * blk - blk + jnp.arange(3 * blk)[None, :]
    off = ki[:, None, :] - qi[:, :, None]
    valid = (jnp.abs(off) <= half) & (ki[:, None, :] >= 0) & (ki[:, None, :] < n_sub)
    scores = jnp.where(valid, scores, NEG_INF)
    m = jnp.max(scores, axis=-1, keepdims=True)
    p = jnp.exp(scores - m)
    den = jnp.sum(p, axis=-1)
    out = jnp.einsum('bhrnqk,bhrnke->bhrnqe', p, vb) / den[..., None]
    lse = m[..., 0] + jnp.log(den)
    out = jnp.swapaxes(out.reshape(b, h, dilation, lp, dh)[:, :, :, :n_sub], 2, 3).reshape(b, h, s, dh)
    lse = jnp.swapaxes(lse.reshape(b, h, dilation, lp)[:, :, :, :n_sub], 2, 3).reshape(b, h, s)
    return out, lse


def dilated_attention(q, k, v):
    outs, lses = [], []
    for window, dilation in DILATED_PATTERNS:
        o, l = dilated_window_branch(q, k, v, window, dilation)
        outs.append(o)
        lses.append(l)
    wts = jax.nn.softmax(jnp.stack(lses, axis=0), axis=0)
    out = wts[0][..., None] * outs[0]
    for i in range(1, len(outs)):
        out = out + wts[i][..., None] * outs[i]
    return out


def mlstm_direction(q, k, v, log_i, log_f):
    b, h, s, dh = q.shape
    ch = MLSTM_CHUNK
    nc = s // ch
    k = k * (dh ** -0.5)

    def chunks(t):
        return jnp.moveaxis(t.reshape(b, h, nc, ch, *t.shape[3:]), 2, 0)

    causal = jnp.tril(jnp.ones((ch, ch), dtype=bool))

    def step(carry, xs):
        c_st, n_st, m_st = carry
        qc, kc, vc, lic, lfc = xs
        bcum = jnp.cumsum(lfc, axis=-1)
        dlog = bcum[..., :, None] - bcum[..., None, :] + lic[..., None, :]
        dlog = jnp.where(causal, dlog, NEG_INF)
        inter = bcum + m_st[..., None]
        m_t = jnp.maximum(inter, jnp.max(dlog, axis=-1))
        sc = jnp.einsum('bhte,bhse->bhts', qc, kc) * jnp.exp(dlog - m_t[..., None])
        g = jnp.exp(inter - m_t)
        num = jnp.einsum('bhts,bhsf->bhtf', sc, vc) + g[..., None] * jnp.einsum('bhte,bhef->bhtf', qc, c_st)
        den = jnp.sum(sc, axis=-1) + g * jnp.einsum('bhte,bhe->bht', qc, n_st)
        h_t = num / jnp.maximum(jnp.abs(den), jnp.exp(-m_t))[..., None]
        b_last = bcum[..., -1]
        wlog = b_last[..., None] - bcum + lic
        m_new = jnp.maximum(b_last + m_st, jnp.max(wlog, axis=-1))
        wexp = jnp.exp(wlog - m_new[..., None])
        decay = jnp.exp(b_last + m_st - m_new)
        c_new = decay[..., None, None] * c_st + jnp.einsum('bhs,bhse,bhsf->bhef', wexp, kc, vc)
        n_new = decay[..., None] * n_st + jnp.einsum('bhs,bhse->bhe', wexp, kc)
        return (c_new, n_new, m_new), h_t

    init = (jnp.zeros((b, h, dh, dh), jnp.float32), jnp.zeros((b, h, dh), jnp.float32),
            jnp.zeros((b, h), jnp.float32))
    _, hs = lax.scan(step, init, (chunks(q), chunks(k), chunks(v), chunks(log_i), chunks(log_f)))
    return jnp.moveaxis(hs, 0, 2).reshape(b, h, s, dh)


def hybrid_mixer(hn, w_in, mlstm_conv_w, mlstm_conv_b, mlstm_gate_b, mlstm_head_g, w_out):
    b, s, _ = hn.shape
    f32 = jnp.float32
    sizes = [ATTN_WIDTH] * 3 + [MLSTM_WIDTH] * 4 + [4 * MLSTM_HEADS]
    split_at = np.cumsum(sizes)[:-1].tolist()
    aq, ak, av, mq, mk, mv, mo, gates = jnp.split(hn @ w_in, split_at, axis=-1)

    def heads(t, n):
        return jnp.transpose(t.astype(f32).reshape(b, s, n, -1), (0, 2, 1, 3))

    q = apply_rotary(heads(aq, ATTN_HEADS))
    k = apply_rotary(heads(ak, ATTN_HEADS))
    v = heads(av, ATTN_HEADS)
    attn = dilated_attention(q, k, v)
    attn = jnp.transpose(attn, (0, 2, 1, 3)).reshape(b, s, ATTN_WIDTH).astype(hn.dtype)

    qk = jax.nn.silu(depthwise_conv_centred(jnp.concatenate([mq, mk], axis=-1), mlstm_conv_w, mlstm_conv_b))
    mq, mk = jnp.split(qk, 2, axis=-1)
    q = heads(mq, MLSTM_HEADS)
    k = heads(mk, MLSTM_HEADS)
    v = heads(mv, MLSTM_HEADS)
    g = (gates + mlstm_gate_b).astype(f32).reshape(b, s, 4, MLSTM_HEADS)
    g = jnp.transpose(g, (2, 0, 3, 1))
    h_fwd = mlstm_direction(q, k, v, g[0], jax.nn.log_sigmoid(g[1]))
    flip = lambda t: jnp.flip(t, axis=2)
    h_bwd = flip(mlstm_direction(flip(q), flip(k), flip(v), flip(g[2]), flip(jax.nn.log_sigmoid(g[3]))))
    hm = h_fwd + h_bwd
    hm = hm * lax.rsqrt(jnp.mean(hm * hm, axis=-1, keepdims=True) + NORM_EPS)
    hm = jnp.transpose(hm, (0, 2, 1, 3)).reshape(b, s, MLSTM_WIDTH)
    mlstm = (hm * mlstm_head_g.astype(f32) * jax.nn.sigmoid(mo.astype(f32))).astype(hn.dtype)

    return jnp.concatenate([attn, mlstm], axis=-1) @ w_out


def conv_ffn(hn, w_up, conv_w, conv_b, w_down):
    u = depthwise_conv_centred(hn @ w_up, conv_w, conv_b)
    gate, val = jnp.split(u, 2, axis=-1)
    return (jax.nn.gelu(gate, approximate=True) * val) @ w_down


def setup_inputs(seed: int = 0) -> dict:
    key = jax.random.key(seed)
    ks = jax.random.split(key, 16)

    def nrm(k, shape, scale):
        return jax.random.normal(k, shape, jnp.float32) * scale

    base = jnp.stack([jnp.zeros((MLSTM_HEADS,), jnp.float32),
                      jnp.linspace(3.0, 6.0, MLSTM_HEADS, dtype=jnp.float32)], axis=0)
    gate_b = (nrm(ks[5], (DEPTH, 2, 2, MLSTM_HEADS), 0.1) + base[None, None]).reshape(DEPTH, 4 * MLSTM_HEADS)
    return {
        'x': nrm(ks[0], (BATCH, SEQ, D_MODEL), 1.0),
        'mix_pre_g': 1.0 + nrm(ks[1], (DEPTH, D_MODEL), 0.02),
        'w_in': nrm(ks[2], (DEPTH, D_MODEL, PROJ_COLS), D_MODEL ** -0.5),
        'mlstm_conv_w': nrm(ks[3], (DEPTH, MLSTM_CONV, 2 * MLSTM_WIDTH), MLSTM_CONV ** -0.5),
        'mlstm_conv_b': nrm(ks[4], (DEPTH, 2 * MLSTM_WIDTH), 0.02),
        'mlstm_gate_b': gate_b,
        'mlstm_head_g': 1.0 + nrm(ks[6], (DEPTH, MLSTM_WIDTH), 0.02),
        'w_out': nrm(ks[7], (DEPTH, D_MIX, D_MODEL), D_MIX ** -0.5),
        'mix_post_g': 1.0 + nrm(ks[8], (DEPTH, D_MODEL), 0.02),
        'ffn_pre_g': 1.0 + nrm(ks[9], (DEPTH, D_MODEL), 0.02),
        'w_up': nrm(ks[10], (DEPTH, D_MODEL, 2 * D_FF), D_MODEL ** -0.5),
        'ffn_conv_w': nrm(ks[11], (DEPTH, FFN_CONV, 2 * D_FF), FFN_CONV ** -0.5),
        'ffn_conv_b': nrm(ks[12], (DEPTH, 2 * D_FF), 0.02),
        'w_down': nrm(ks[13], (DEPTH, D_FF, D_MODEL), D_FF ** -0.5),
        'ffn_post_g': 1.0 + nrm(ks[14], (DEPTH, D_MODEL), 0.02),
    }


def reference(x, mix_pre_g, w_in, mlstm_conv_w, mlstm_conv_b, mlstm_gate_b, mlstm_head_g, w_out,
              mix_post_g, ffn_pre_g, w_up, ffn_conv_w, ffn_conv_b, w_down, ffn_post_g):
    for l in range(DEPTH):
        mixed = hybrid_mixer(rms_norm(x, mix_pre_g[l]), w_in[l], mlstm_conv_w[l], mlstm_conv_b[l],
                             mlstm_gate_b[l], mlstm_head_g[l], w_out[l])
        x = x + rms_norm(mixed, mix_post_g[l])
        ff = conv_ffn(rms_norm(x, ffn_pre_g[l]), w_up[l], ffn_conv_w[l], ffn_conv_b[l], w_down[l])
        x = x + rms_norm(ff, ffn_post_g[l])
    return x
```

```python
import functools

import numpy as np
import jax
import jax.numpy as jnp
from jax import lax
from jax.experimental import pallas as pl
from jax.experimental.pallas import tpu as pltpu

F32 = jnp.float32
BF16 = jnp.bfloat16

ATTN_HEAD_DIM = 64
ATTN_HALF_WINDOW = 64
DILATIONS = (1, 4, 16)
MLSTM_HEADS = 4
MLSTM_CHUNK = 128
ROPE_THETA = 10000.0
NORM_EPS = 1e-6
NEG_INF = -1e30

LANES = 128
Q_BLOCK = 128
VMEM_LIMIT_BYTES = 56 * 1024 * 1024


def _cparams(n_grid_dims):
    return pltpu.CompilerParams(
        dimension_semantics=("arbitrary",) * n_grid_dims,
        vmem_limit_bytes=VMEM_LIMIT_BYTES)


def _rms(x, g):
    return x * lax.rsqrt(jnp.mean(x * x, axis=-1, keepdims=True) + NORM_EPS) * g


def _prenorm_kernel(x_ref, g_ref, o_ref):
    o_ref[...] = _rms(x_ref[...], g_ref[...]).astype(o_ref.dtype)


def _prenorm(x, g, tm=1024):
    t, d = x.shape
    return pl.pallas_call(
        _prenorm_kernel,
        grid=(t // tm,),
        in_specs=[pl.BlockSpec((tm, d), lambda i: (i, 0)),
                  pl.BlockSpec((1, d), lambda i: (0, 0))],
        out_specs=pl.BlockSpec((tm, d), lambda i: (i, 0)),
        out_shape=jax.ShapeDtypeStruct((t, d), BF16),
        compiler_params=_cparams(1),
        name="prenorm",
    )(x, g.reshape(1, d))


def _inproj_kernel(hn_ref, w_ref, wg_ref, gb_ref, cos_ref, sin_ref,
                   qkv_ref, mall_ref, gates_ref, *, attn_w, mix_w):
    hn = hn_ref[...]
    cos = cos_ref[...]
    sin = sin_ref[...]
    tm = hn.shape[0]
    lane = lax.broadcasted_iota(jnp.int32, (tm, LANES), 1)
    first_half = (lane % ATTN_HEAD_DIM) < (ATTN_HEAD_DIM // 2)
    n_pairs = attn_w // LANES
    for grp in range(3):
        res = jnp.dot(hn, w_ref[:, grp * attn_w:(grp + 1) * attn_w],
                      preferred_element_type=F32)
        for hp in range(n_pairs):
            xs = res[:, hp * LANES:(hp + 1) * LANES]
            if grp < 2:
                rot = jnp.where(first_half,
                                pltpu.roll(xs, LANES - ATTN_HEAD_DIM // 2, 1),
                                pltpu.roll(xs, ATTN_HEAD_DIM // 2, 1))
                xs = xs * cos + rot * sin
            if grp == 0:
                xs = xs * (ATTN_HEAD_DIM ** -0.5)
            qkv_ref[grp * n_pairs + hp] = xs.astype(qkv_ref.dtype)
    base = 3 * attn_w
    for grp in range(4):
        res = jnp.dot(hn, w_ref[:, base + grp * mix_w: base + (grp + 1) * mix_w],
                      preferred_element_type=F32)
        mall_ref[:, grp * mix_w:(grp + 1) * mix_w] = res.astype(mall_ref.dtype)
    gates_ref[...] = jnp.dot(hn, wg_ref[...], preferred_element_type=F32) + gb_ref[...]


def _inproj(hn, w_main, w_gate, gate_b, cos, sin, *, attn_w, mix_w, seq, tm=512):
    t, d = hn.shape
    n_pairs = attn_w // LANES
    n_main = w_main.shape[1]
    tiles_per_seq = seq // tm
    kern = functools.partial(_inproj_kernel, attn_w=attn_w, mix_w=mix_w)
    return pl.pallas_call(
        kern,
        grid=(t // tm,),
        in_specs=[pl.BlockSpec((tm, d), lambda i: (i, 0)),
                  pl.BlockSpec((d, n_main), lambda i: (0, 0)),
                  pl.BlockSpec((d, LANES), lambda i: (0, 0)),
                  pl.BlockSpec((1, LANES), lambda i: (0, 0)),
                  pl.BlockSpec((tm, LANES), lambda i: (i % tiles_per_seq, 0)),
                  pl.BlockSpec((tm, LANES), lambda i: (i % tiles_per_seq, 0))],
        out_specs=[pl.BlockSpec((3 * n_pairs, tm, LANES), lambda i: (0, i, 0)),
                   pl.BlockSpec((tm, 4 * mix_w), lambda i: (i, 0)),
                   pl.BlockSpec((tm, LANES), lambda i: (i, 0))],
        out_shape=[jax.ShapeDtypeStruct((3 * n_pairs, t, LANES), BF16),
                   jax.ShapeDtypeStruct((t, 4 * mix_w), BF16),
                   jax.ShapeDtypeStruct((t, LANES), F32)],
        compiler_params=_cparams(1),
        name="inproj",
    )(hn, w_main, w_gate, gate_b, cos, sin)


def _attn_block(q, k, v, qpos0, kpos0):
    nk = k.shape[0]
    row = lax.broadcasted_iota(jnp.int32, (Q_BLOCK, nk), 0)
    col = lax.broadcasted_iota(jnp.int32, (Q_BLOCK, nk), 1)
    off = (col + kpos0) - (row + qpos0)
    valid = jnp.abs(off) <= ATTN_HALF_WINDOW
    lane = lax.broadcasted_iota(jnp.int32, (Q_BLOCK, LANES), 1)
    head0 = lane < ATTN_HEAD_DIM
    outs, lses = [], []
    for hh in range(2):
        sel = head0 if hh == 0 else jnp.logical_not(head0)
        qh = jnp.where(sel, q, jnp.zeros_like(q))
        s = lax.dot_general(qh, k, (((1,), (1,)), ((), ())), preferred_element_type=F32)
        s = jnp.where(valid, s, NEG_INF)
        m = jnp.max(s, axis=-1, keepdims=True)
        p = jnp.exp(s - m)
        l = jnp.sum(p, axis=-1, keepdims=True)
        o = jnp.dot(p.astype(v.dtype), v, preferred_element_type=F32)
        outs.append(o / l)
        lses.append(m + jnp.log(l))
    return jnp.where(head0, outs[0], outs[1]), jnp.where(head0, lses[0], lses[1])


def _attn_kernel(qn, kn, vn, q4, k4, v4, q16, k16, v16, o_ref, acc_refs, lse_refs, *, seq):
    views = ((qn, kn, vn), (q4, k4, v4), (q16, k16, v16))
    for (q_ref, k_ref, v_ref), dil, acc, lse in zip(views, DILATIONS, acc_refs, lse_refs):
        n_sub = seq // dil
        n_blocks = n_sub // Q_BLOCK
        kwin = min(n_sub, Q_BLOCK + 2 * ATTN_HALF_WINDOW)

        def body(idx, carry, q_ref=q_ref, k_ref=k_ref, v_ref=v_ref, dil=dil, acc=acc, lse=lse,
                 n_sub=n_sub, n_blocks=n_blocks, kwin=kwin):
            r = idx // n_blocks
            blk = idx % n_blocks
            qs = pl.multiple_of(blk * Q_BLOCK, Q_BLOCK)
            ks = pl.multiple_of(jnp.clip(qs - ATTN_HALF_WINDOW, 0, n_sub - kwin), ATTN_HALF_WINDOW)
            ls = pl.multiple_of(r * LANES, LANES)
            q = q_ref[pl.ds(qs, Q_BLOCK), pl.ds(ls, LANES)]
            k = k_ref[pl.ds(ks, kwin), pl.ds(ls, LANES)]
            v = v_ref[pl.ds(ks, kwin), pl.ds(ls, LANES)]
            o, l = _attn_block(q, k, v, qs, ks)
            if dil == 1:
                acc[pl.ds(qs, Q_BLOCK), :] = o
                lse[pl.ds(qs, Q_BLOCK), :] = l
            else:
                acc[pl.ds(qs * dil + r, Q_BLOCK, stride=dil), :] = o
                lse[pl.ds(qs * dil + r, Q_BLOCK, stride=dil), :] = l
            return carry

        lax.fori_loop(0, dil * n_blocks, body, 0)

    def combine(i, carry):
        rs = pl.multiple_of(i * Q_BLOCK, Q_BLOCK)
        ls = [lse[pl.ds(rs, Q_BLOCK), :] for lse in lse_refs]
        mx = jnp.maximum(jnp.maximum(ls[0], ls[1]), ls[2])
        es = [jnp.exp(l - mx) for l in ls]
        num = es[0] * acc_refs[0][pl.ds(rs, Q_BLOCK), :]
        for e, acc in zip(es[1:], acc_refs[1:]):
            num = num + e * acc[pl.ds(rs, Q_BLOCK), :]
        o_ref[pl.ds(rs, Q_BLOCK), :] = (num / (es[0] + es[1] + es[2])).astype(o_ref.dtype)
        return carry

    lax.fori_loop(0, seq // Q_BLOCK, combine, 0)


def _attn_kernel_entry(qn, kn, vn, q4, k4, v4, q16, k16, v16, o_ref,
                       a1, a2, a3, l1, l2, l3, *, seq):
    _attn_kernel(qn, kn, vn, q4, k4, v4, q16, k16, v16, o_ref, (a1, a2, a3), (l1, l2, l3), seq=seq)


def _attention(qkv, *, batch, seq):
    n3, t, _ = qkv.shape
    n_pairs = n3 // 3
    in_specs, args = [], []
    for dil in DILATIONS:
        n_sub = seq // dil
        view = qkv.reshape(n3, batch, n_sub, dil * LANES)
        for which in range(3):
            in_specs.append(pl.BlockSpec(
                (None, None, n_sub, dil * LANES),
                lambda b, hp, which=which: (which * n_pairs + hp, b, 0, 0)))
            args.append(view)
    scratch = [pltpu.VMEM((seq, LANES), F32) for _ in range(6)]
    return pl.pallas_call(
        functools.partial(_attn_kernel_entry, seq=seq),
        grid=(batch, n_pairs),
        in_specs=in_specs,
        out_specs=pl.BlockSpec((seq, LANES), lambda b, hp: (b, hp)),
        out_shape=jax.ShapeDtypeStruct((t, n_pairs * LANES), BF16),
        scratch_shapes=scratch,
        compiler_params=_cparams(2),
        name="dilated_attn",
    )(*args)


def _gateprep_kernel(g_ref, o_ref, *, n_heads):
    row = lax.broadcasted_iota(jnp.int32, (MLSTM_CHUNK, MLSTM_CHUNK), 0)
    col = lax.broadcasted_iota(jnp.int32, (MLSTM_CHUNK, MLSTM_CHUNK), 1)
    tri_prefix = (col <= row).astype(F32)
    tri_suffix = (col >= row).astype(F32)
    lane = lax.broadcasted_iota(jnp.int32, (MLSTM_CHUNK, LANES), 1)
    fwd_f = (lane >= n_heads) & (lane < 2 * n_heads)
    bwd_f = (lane >= 3 * n_heads) & (lane < 4 * n_heads)
    for c in range(g_ref.shape[0] // MLSTM_CHUNK):
        g = g_ref[c * MLSTM_CHUNK:(c + 1) * MLSTM_CHUNK, :]
        log_f = jnp.minimum(g, 0.0) - jnp.log(1.0 + jnp.exp(-jnp.abs(g)))
        pre = jnp.dot(tri_prefix, log_f, preferred_element_type=F32, precision=lax.Precision.HIGHEST)
        suf = jnp.dot(tri_suffix, log_f, preferred_element_type=F32, precision=lax.Precision.HIGHEST)
        o_ref[c * MLSTM_CHUNK:(c + 1) * MLSTM_CHUNK, :] = jnp.where(fwd_f, pre, jnp.where(bwd_f, suf, g))


def _gateprep(gates, *, n_heads, tm=1024):
    t, w = gates.shape
    return pl.pallas_call(
        functools.partial(_gateprep_kernel, n_heads=n_heads),
        grid=(t // tm,),
        in_specs=[pl.BlockSpec((tm, w), lambda i: (i, 0))],
        out_specs=pl.BlockSpec((tm, w), lambda i: (i, 0)),
        out_shape=jax.ShapeDtypeStruct((t, w), F32),
        compiler_params=_cparams(1),
        name="gateprep",
    )(gates)


def _conv_silu(x_ref, w_ref, b_ref):
    x = x_ref[...].astype(F32)
    n = x.shape[0]
    taps = w_ref.shape[0]
    pad = taps // 2
    row = lax.broadcasted_iota(jnp.int32, x.shape, 0)
    acc = x * w_ref[pad:pad + 1, :] + b_ref[...]
    for j in range(taps):
        d = j - pad
        if d == 0:
            continue
        shifted = pltpu.roll(x, (-d) % n, 0)
        inside = (row + d >= 0) & (row + d < n)
        acc = acc + jnp.where(inside, shifted, 0.0) * w_ref[j:j + 1, :]
    return acc * jax.nn.sigmoid(acc)


def _mlstm_kernel(mq_ref, mk_ref, mv_ref, mo_ref, gc_ref, gr_ref, cwq_ref, cwk_ref, cbq_ref, cbk_ref,
                  hg_ref, o_ref, q_s, k_s, gc_s, hf_s, hb_s, *, n_heads):
    head = pl.program_id(1)
    seq, dh = q_s.shape
    ch = MLSTM_CHUNK
    n_chunks = seq // ch
    q_s[...] = _conv_silu(mq_ref, cwq_ref, cbq_ref).astype(q_s.dtype)
    k_s[...] = (_conv_silu(mk_ref, cwk_ref, cbk_ref) * (dh ** -0.5)).astype(k_s.dtype)
    gc_s[...] = pltpu.roll(gc_ref[...], (LANES - head) % LANES, 1)

    row = lax.broadcasted_iota(jnp.int32, (ch, ch), 0)
    col = lax.broadcasted_iota(jnp.int32, (ch, ch), 1)

    def direction(c, q_i, q_b, mask, last, state, h_store):
        c_st, n_st, m_st = state
        rs = pl.multiple_of(c * ch, ch)
        qc = q_s[pl.ds(rs, ch), :]
        kc = k_s[pl.ds(rs, ch), :]
        vc = mv_ref[pl.ds(rs, ch), :]
        g = gc_s[pl.ds(rs, ch), :]
        li_col = g[:, q_i * n_heads:q_i * n_heads + 1]
        b_col = g[:, q_b * n_heads:q_b * n_heads + 1]
        li_row = gr_ref[q_i, pl.ds(c, 1), :]
        b_row = gr_ref[q_b, pl.ds(c, 1), :]
        b_last = b_row[:, last:last + 1]
        dlog = jnp.where(mask, b_col - b_row + li_row, NEG_INF)
        inter = b_col + m_st
        m_t = jnp.maximum(inter, jnp.max(dlog, axis=-1, keepdims=True))
        s = lax.dot_general(qc, kc, (((1,), (1,)), ((), ())), preferred_element_type=F32)
        sc = s * jnp.exp(dlog - m_t)
        g_int = jnp.exp(inter - m_t)
        num = (jnp.dot(sc.astype(vc.dtype), vc, preferred_element_type=F32)
               + g_int * jnp.dot(qc, c_st.astype(qc.dtype), preferred_element_type=F32))
        den = (jnp.sum(sc, axis=-1, keepdims=True)
               + g_int * jnp.sum(qc.astype(F32) * n_st, axis=-1, keepdims=True))
        h_store[pl.ds(rs, ch), :] = num / jnp.maximum(jnp.abs(den), jnp.exp(-m_t))
        wlog = b_last - b_col + li_col
        m_new = jnp.maximum(b_last + m_st, jnp.max(wlog, axis=0, keepdims=True))
        kw = kc.astype(F32) * jnp.exp(wlog - m_new)
        decay = jnp.exp(b_last + m_st - m_new)
        c_new = decay * c_st + lax.dot_general(kw.astype(vc.dtype), vc, (((0,), (0,)), ((), ())),
                                               preferred_element_type=F32)
        n_new = decay * n_st + jnp.sum(kw, axis=0, keepdims=True)
        return c_new, n_new, m_new

    def step(j, carry):
        fwd, bwd = carry
        fwd = direction(j, 0, 1, col <= row, ch - 1, fwd, hf_s)
        bwd = direction(n_chunks - 1 - j, 2, 3, col >= row, 0, bwd, hb_s)
        return fwd, bwd

    zero = (jnp.zeros((dh, dh), F32), jnp.zeros((1, dh), F32), jnp.zeros((1, 1), F32))
    lax.fori_loop(0, n_chunks, step, (zero, zero))

    hm = hf_s[...] + hb_s[...]
    hm = hm * lax.rsqrt(jnp.mean(hm * hm, axis=-1, keepdims=True) + NORM_EPS)
    o_ref[...] = (hm * hg_ref[...] * jax.nn.sigmoid(mo_ref[...].astype(F32))).astype(o_ref.dtype)


def _mlstm(mall, gc, gr, conv_w, conv_b, head_g, *, batch, seq, n_heads):
    t = mall.shape[0]
    dh = mall.shape[1] // (4 * n_heads)
    taps = conv_w.shape[0]
    n_chunks = seq // MLSTM_CHUNK

    def col_block(offset):
        return pl.BlockSpec((seq, dh), lambda b, h, offset=offset: (b, offset * n_heads + h))

    in_specs = [col_block(0), col_block(1), col_block(2), col_block(3),
                pl.BlockSpec((seq, LANES), lambda b, h: (b, 0)),
                pl.BlockSpec((None, 4, None, n_chunks, MLSTM_CHUNK), lambda b, h: (b, 0, h, 0, 0)),
                pl.BlockSpec((taps, dh), lambda b, h: (0, h)),
                pl.BlockSpec((taps, dh), lambda b, h: (0, n_heads + h)),
                pl.BlockSpec((1, dh), lambda b, h: (0, h)),
                pl.BlockSpec((1, dh), lambda b, h: (0, n_heads + h)),
                pl.BlockSpec((1, dh), lambda b, h: (0, h))]
    scratch = [pltpu.VMEM((seq, dh), BF16), pltpu.VMEM((seq, dh), BF16),
               pltpu.VMEM((seq, LANES), F32), pltpu.VMEM((seq, dh), F32), pltpu.VMEM((seq, dh), F32)]
    return pl.pallas_call(
        functools.partial(_mlstm_kernel, n_heads=n_heads),
        grid=(batch, n_heads),
        in_specs=in_specs,
        out_specs=pl.BlockSpec((seq, dh), lambda b, h: (b, h)),
        out_shape=jax.ShapeDtypeStruct((t, n_heads * dh), BF16),
        scratch_shapes=scratch,
        compiler_params=_cparams(2),
        name="mlstm",
    )(mall, mall, mall, mall, gc, gr, conv_w, conv_w, conv_b, conv_b, head_g)


def _outproj_kernel(a_ref, m_ref, w_ref, x_ref, gpost_ref, gpre_ref, x1_ref, hn_ref):
    ka = a_ref.shape[1]
    mixed = (jnp.dot(a_ref[...], w_ref[:ka, :], preferred_element_type=F32)
             + jnp.dot(m_ref[...], w_ref[ka:, :], preferred_element_type=F32))
    x1 = x_ref[...] + _rms(mixed, gpost_ref[...])
    x1_ref[...] = x1
    hn_ref[...] = _rms(x1, gpre_ref[...]).astype(hn_ref.dtype)


def _outproj(attn, ml, w_out, x, g_post, g_pre, tm=512):
    t, d = x.shape
    ka, km = attn.shape[1], ml.shape[1]
    return pl.pallas_call(
        _outproj_kernel,
        grid=(t // tm,),
        in_specs=[pl.BlockSpec((tm, ka), lambda i: (i, 0)),
                  pl.BlockSpec((tm, km), lambda i: (i, 0)),
                  pl.BlockSpec((ka + km, d), lambda i: (0, 0)),
                  pl.BlockSpec((tm, d), lambda i: (i, 0)),
                  pl.BlockSpec((1, d), lambda i: (0, 0)),
                  pl.BlockSpec((1, d), lambda i: (0, 0))],
        out_specs=[pl.BlockSpec((tm, d), lambda i: (i, 0)),
                   pl.BlockSpec((tm, d), lambda i: (i, 0))],
        out_shape=[jax.ShapeDtypeStruct((t, d), F32), jax.ShapeDtypeStruct((t, d), BF16)],
        compiler_params=_cparams(1),
        name="outproj",
    )(attn, ml, w_out, x, g_post.reshape(1, d), g_pre.reshape(1, d))


FFN_HALO = 16
FFN_CHUNK = 256


def _ffn_kernel(prev_ref, main_ref, next_ref, x1_ref, wup_ref, wdn_ref, cw_ref, cb_ref,
                gpost_ref, gnext_ref, x2_ref, hn_ref, acc_ref, *, tiles_per_seq):
    i = pl.program_id(0)
    tm = main_ref.shape[0]
    d_ff = wdn_ref.shape[0]
    pos = i % tiles_per_seq
    prev = jnp.where(pos == 0, jnp.zeros_like(prev_ref[...]), prev_ref[...])
    nxt = jnp.where(pos == tiles_per_seq - 1, jnp.zeros_like(next_ref[...]), next_ref[...])
    lhs = jnp.concatenate([prev, main_ref[...], nxt], axis=0)

    def conv(u, c0):
        w = cw_ref[:, c0:c0 + FFN_CHUNK]
        out = u[FFN_HALO:FFN_HALO + tm] * w[1:2] + cb_ref[:, c0:c0 + FFN_CHUNK]
        out = out + u[FFN_HALO - 1:FFN_HALO - 1 + tm] * w[0:1]
        return out + u[FFN_HALO + 1:FFN_HALO + 1 + tm] * w[2:3]

    for c in range(d_ff // FFN_CHUNK):
        c0 = c * FFN_CHUNK
        gate = conv(jnp.dot(lhs, wup_ref[:, c0:c0 + FFN_CHUNK], preferred_element_type=F32), c0)
        val = conv(jnp.dot(lhs, wup_ref[:, d_ff + c0:d_ff + c0 + FFN_CHUNK],
                           preferred_element_type=F32), d_ff + c0)
        inner = np.sqrt(2.0 / np.pi) * (gate + 0.044715 * (gate * gate * gate))
        act = (0.5 * gate * (1.0 + jnp.tanh(inner)) * val).astype(lhs.dtype)
        part = jnp.dot(act, wdn_ref[c0:c0 + FFN_CHUNK, :], preferred_element_type=F32)
        if c == 0:
            acc_ref[...] = part
        else:
            acc_ref[...] += part

    x2 = x1_ref[...] + _rms(acc_ref[...], gpost_ref[...])
    x2_ref[...] = x2
    hn_ref[...] = _rms(x2, gnext_ref[...]).astype(hn_ref.dtype)


def _ffn(hn, x1, w_up, w_down, conv_w, conv_b, g_post, g_next, *, seq, tm=512):
    t, d = x1.shape
    d_ff = w_down.shape[0]
    tiles_per_seq = seq // tm
    halo_per_tile = tm // FFN_HALO
    n_halo_blocks = t // FFN_HALO
    return pl.pallas_call(
        functools.partial(_ffn_kernel, tiles_per_seq=tiles_per_seq),
        grid=(t // tm,),
        in_specs=[pl.BlockSpec((FFN_HALO, d), lambda i: (jnp.maximum(i * halo_per_tile - 1, 0), 0)),
                  pl.BlockSpec((tm, d), lambda i: (i, 0)),
                  pl.BlockSpec((FFN_HALO, d),
                               lambda i: (jnp.minimum((i + 1) * halo_per_tile, n_halo_blocks - 1), 0)),
                  pl.BlockSpec((tm, d), lambda i: (i, 0)),
                  pl.BlockSpec((d, 2 * d_ff), lambda i: (0, 0)),
                  pl.BlockSpec((d_ff, d), lambda i: (0, 0)),
                  pl.BlockSpec(conv_w.shape, lambda i: (0, 0)),
                  pl.BlockSpec((1, 2 * d_ff), lambda i: (0, 0)),
                  pl.BlockSpec((1, d), lambda i: (0, 0)),
                  pl.BlockSpec((1, d), lambda i: (0, 0))],
        out_specs=[pl.BlockSpec((tm, d), lambda i: (i, 0)),
                   pl.BlockSpec((tm, d), lambda i: (i, 0))],
        out_shape=[jax.ShapeDtypeStruct((t, d), F32), jax.ShapeDtypeStruct((t, d), BF16)],
        scratch_shapes=[pltpu.VMEM((tm, d), F32)],
        compiler_params=_cparams(1),
        name="convffn",
    )(hn, hn, hn, x1, w_up, w_down, conv_w, conv_b.reshape(1, -1), g_post.reshape(1, d),
      g_next.reshape(1, d))


def _rotary_tables(seq):
    half = ATTN_HEAD_DIM // 2
    inv_freq = ROPE_THETA ** (-jnp.arange(0, ATTN_HEAD_DIM, 2, dtype=F32) / ATTN_HEAD_DIM)
    ang = jnp.arange(seq, dtype=F32)[:, None] * inv_freq[None, :]
    cos, sin = jnp.cos(ang), jnp.sin(ang)
    reps = LANES // ATTN_HEAD_DIM
    cos_t = jnp.tile(jnp.concatenate([cos, cos], axis=-1), (1, reps))
    sin_t = jnp.tile(jnp.concatenate([-sin, sin], axis=-1), (1, reps))
    return cos_t, sin_t


def kernel(x, mix_pre_g, w_in, mlstm_conv_w, mlstm_conv_b, mlstm_gate_b, mlstm_head_g, w_out,
           mix_post_g, ffn_pre_g, w_up, ffn_conv_w, ffn_conv_b, w_down, ffn_post_g):
    batch, seq, d = x.shape
    depth = w_in.shape[0]
    t = batch * seq
    n_heads = MLSTM_HEADS
    n_gates = mlstm_gate_b.shape[1]
    mix_w = mlstm_head_g.shape[1]
    attn_w = w_out.shape[1] - mix_w
    n_chunks = seq // MLSTM_CHUNK
    assert n_gates == 4 * n_heads and n_gates <= LANES
    assert w_in.shape[2] == 3 * attn_w + 4 * mix_w + n_gates

    cos_t, sin_t = _rotary_tables(seq)
    xf = x.reshape(t, d)
    hn = _prenorm(xf, mix_pre_g[0])
    for l in range(depth):
        w = w_in[l].astype(BF16)
        n_main = 3 * attn_w + 4 * mix_w
        w_gate = jnp.pad(w[:, n_main:], ((0, 0), (0, LANES - n_gates)))
        gate_b = jnp.pad(mlstm_gate_b[l], (0, LANES - n_gates)).reshape(1, LANES)
        qkv, mall, gates = _inproj(hn, w[:, :n_main], w_gate, gate_b, cos_t, sin_t,
                                   attn_w=attn_w, mix_w=mix_w, seq=seq)
        attn = _attention(qkv, batch=batch, seq=seq)
        gc = _gateprep(gates, n_heads=n_heads)
        gr = gc[:, :n_gates].reshape(batch, seq, n_gates).transpose(0, 2, 1)
        gr = gr.reshape(batch, 4, n_heads, n_chunks, MLSTM_CHUNK)
        ml = _mlstm(mall, gc, gr, mlstm_conv_w[l], mlstm_conv_b[l].reshape(1, -1),
                    mlstm_head_g[l].reshape(1, -1), batch=batch, seq=seq, n_heads=n_heads)
        x1, hn2 = _outproj(attn, ml, w_out[l].astype(BF16), xf, mix_post_g[l], ffn_pre_g[l])
        g_next = mix_pre_g[l + 1] if l + 1 < depth else mix_pre_g[l]
        xf, hn = _ffn(hn2, x1, w_up[l].astype(BF16), w_down[l].astype(BF16), ffn_conv_w[l],
                      ffn_conv_b[l], ffn_post_g[l], g_next, seq=seq)
    return xf.reshape(batch, seq, d)
```

```python
import functools

import numpy as np
import jax
import jax.numpy as jnp
from jax import lax
from jax.experimental import pallas as pl
from jax.experimental.pallas import tpu as pltpu

F32 = jnp.float32
BF16 = jnp.bfloat16

ATTN_HEAD_DIM = 64
ATTN_HALF_WINDOW = 64
DILATIONS = (1, 4, 16)
MLSTM_HEADS = 4
MLSTM_CHUNK = 128
ROPE_THETA = 10000.0
NORM_EPS = 1e-6
NEG_INF = -1e30

LANES = 128
Q_BLOCK = 128
K_WINDOW = Q_BLOCK + 2 * ATTN_HALF_WINDOW
DIL_MID, DIL_MAX = DILATIONS[1], DILATIONS[2]
DIL_RATIO = DIL_MAX // DIL_MID
ATTN_Q_SCALE = float(ATTN_HEAD_DIM ** -0.5 * np.log2(np.e))
VMEM_LIMIT_BYTES = 56 * 1024 * 1024


def _cparams(n_grid_dims):
    return pltpu.CompilerParams(
        dimension_semantics=("arbitrary",) * n_grid_dims,
        vmem_limit_bytes=VMEM_LIMIT_BYTES)


def _rms(x, g):
    return x * lax.rsqrt(jnp.mean(x * x, axis=-1, keepdims=True) + NORM_EPS) * g


def _prenorm_kernel(x_ref, g_ref, o_ref):
    o_ref[...] = _rms(x_ref[...], g_ref[...]).astype(o_ref.dtype)


def _prenorm(x, g, tm=1024):
    t, d = x.shape
    return pl.pallas_call(
        _prenorm_kernel,
        grid=(t // tm,),
        in_specs=[pl.BlockSpec((tm, d), lambda i: (i, 0)),
                  pl.BlockSpec((1, d), lambda i: (0, 0))],
        out_specs=pl.BlockSpec((tm, d), lambda i: (i, 0)),
        out_shape=jax.ShapeDtypeStruct((t, d), BF16),
        compiler_params=_cparams(1),
        name="prenorm",
    )(x, g.reshape(1, d))


def _inproj_kernel(hn_ref, w_ref, wg_ref, gb_ref, cos_ref, sin_ref,
                   qkv_ref, qkv16_ref, mall_ref, gates_ref, stage_ref, *, attn_w, mix_w):
    hn = hn_ref[...]
    cos = cos_ref[...]
    sin = sin_ref[...]
    tm = hn.shape[0]
    lane = lax.broadcasted_iota(jnp.int32, (tm, LANES), 1)
    first_half = (lane % ATTN_HEAD_DIM) < (ATTN_HEAD_DIM // 2)
    n_pairs = attn_w // LANES
    for grp in range(3):
        res = jnp.dot(hn, w_ref[:, grp * attn_w:(grp + 1) * attn_w],
                      preferred_element_type=F32)
        for hp in range(n_pairs):
            xs = res[:, hp * LANES:(hp + 1) * LANES]
            if grp < 2:
                rot = jnp.where(first_half,
                                pltpu.roll(xs, LANES - ATTN_HEAD_DIM // 2, 1),
                                pltpu.roll(xs, ATTN_HEAD_DIM // 2, 1))
                xs = xs * cos + rot * sin
            if grp == 0:
                xs = xs * ATTN_Q_SCALE
            slab = grp * n_pairs + hp
            qkv_ref[slab] = xs.astype(qkv_ref.dtype)
            stage_ref[slab] = xs
            for r in range(DIL_MAX):
                qkv16_ref[slab, :, r * LANES:(r + 1) * LANES] = (
                    stage_ref[slab, pl.ds(r, tm // DIL_MAX, stride=DIL_MAX), :].astype(qkv16_ref.dtype))
    base = 3 * attn_w
    for grp in range(4):
        res = jnp.dot(hn, w_ref[:, base + grp * mix_w: base + (grp + 1) * mix_w],
                      preferred_element_type=F32)
        mall_ref[:, grp * mix_w:(grp + 1) * mix_w] = res.astype(mall_ref.dtype)
    gates_ref[...] = jnp.dot(hn, wg_ref[...], preferred_element_type=F32) + gb_ref[...]


def _inproj(hn, w_main, w_gate, gate_b, cos, sin, *, attn_w, mix_w, seq, tm=512):
    t, d = hn.shape
    n_pairs = attn_w // LANES
    n_main = w_main.shape[1]
    tiles_per_seq = seq // tm
    kern = functools.partial(_inproj_kernel, attn_w=attn_w, mix_w=mix_w)
    return pl.pallas_call(
        kern,
        grid=(t // tm,),
        in_specs=[pl.BlockSpec((tm, d), lambda i: (i, 0)),
                  pl.BlockSpec((d, n_main), lambda i: (0, 0)),
                  pl.BlockSpec((d, LANES), lambda i: (0, 0)),
                  pl.BlockSpec((1, LANES), lambda i: (0, 0)),
                  pl.BlockSpec((tm, LANES), lambda i: (i % tiles_per_seq, 0)),
                  pl.BlockSpec((tm, LANES), lambda i: (i % tiles_per_seq, 0))],
        out_specs=[pl.BlockSpec((3 * n_pairs, tm, LANES), lambda i: (0, i, 0)),
                   pl.BlockSpec((3 * n_pairs, tm // DIL_MAX, DIL_MAX * LANES), lambda i: (0, i, 0)),
                   pl.BlockSpec((tm, 4 * mix_w), lambda i: (i, 0)),
                   pl.BlockSpec((tm, LANES), lambda i: (i, 0))],
        out_shape=[jax.ShapeDtypeStruct((3 * n_pairs, t, LANES), BF16),
                   jax.ShapeDtypeStruct((3 * n_pairs, t // DIL_MAX, DIL_MAX * LANES), BF16),
                   jax.ShapeDtypeStruct((t, 4 * mix_w), BF16),
                   jax.ShapeDtypeStruct((t, LANES), F32)],
        scratch_shapes=[pltpu.VMEM((3 * n_pairs, tm, LANES), F32)],
        compiler_params=_cparams(1),
        name="inproj",
    )(hn, w_main, w_gate, gate_b, cos, sin)


def _attn_block(q, k, v, bias):
    nk = k.shape[0]

    def head_masks(rows):
        lane = lax.broadcasted_iota(jnp.int32, (rows, LANES), 1)
        h0 = jnp.where(lane < ATTN_HEAD_DIM, 1.0, 0.0).astype(q.dtype)
        return h0, (1.0 - h0.astype(F32)).astype(q.dtype)

    q0, q1 = head_masks(Q_BLOCK)
    v0, v1 = head_masks(nk)
    q_st = jnp.concatenate([q * q0, q * q1], axis=0)
    s = lax.dot_general(q_st, k, (((1,), (1,)), ((), ())), preferred_element_type=F32)
    s = s + jnp.concatenate([bias, bias], axis=0)
    m = jnp.max(s, axis=-1, keepdims=True)
    p = jnp.exp2(s - m).astype(v.dtype)
    p_cat = jnp.concatenate([p[:Q_BLOCK], p[Q_BLOCK:]], axis=1)
    v_ext = jnp.concatenate([jnp.concatenate([v * v0, v0], axis=1),
                             jnp.concatenate([v * v1, v1], axis=1)], axis=0)
    o = jnp.dot(p_cat, v_ext, preferred_element_type=F32)
    lane_o = lax.broadcasted_iota(jnp.int32, (Q_BLOCK, LANES), 1)
    m_lanes = jnp.where(lane_o < ATTN_HEAD_DIM, m[:Q_BLOCK], m[Q_BLOCK:])
    return o[:, :LANES], o[:, LANES:], m_lanes


def _attn_kernel(qn, kn, vn, q16, k16, v16, bias1_ref, bias4_ref, bias16_ref, o_ref,
                 acc_ref, den_ref, max_ref, *, seq):
    n_blocks = seq // Q_BLOCK
    n_mid = (seq // DIL_MID) // Q_BLOCK
    n_max = seq // DIL_MAX
    q_rows = Q_BLOCK // DIL_RATIO
    k_rows = K_WINDOW // DIL_RATIO

    def edge_table(blk, n):
        return (blk > 0).astype(jnp.int32) + (blk == n - 1).astype(jnp.int32)

    def store(branch, start, size, stride, vals, rows=slice(None)):
        for ref, val in zip((acc_ref, den_ref, max_ref), vals):
            idx = pl.ds(start, size) if stride == 1 else pl.ds(start, size, stride=stride)
            ref[branch, idx, :] = val[rows]

    def body(i, carry):
        qs = pl.multiple_of(i * Q_BLOCK, Q_BLOCK)
        ks = pl.multiple_of(jnp.clip(qs - ATTN_HALF_WINDOW, 0, seq - K_WINDOW), ATTN_HALF_WINDOW)
        res = _attn_block(qn[pl.ds(qs, Q_BLOCK), :], kn[pl.ds(ks, K_WINDOW), :], vn[pl.ds(ks, K_WINDOW), :],
                          bias1_ref[edge_table(i, n_blocks)])
        store(0, qs, Q_BLOCK, 1, res)

        cls = i // n_mid
        blk = i % n_mid
        r0 = pl.multiple_of(blk * q_rows, q_rows)
        k0 = pl.multiple_of(jnp.clip(r0 - ATTN_HALF_WINDOW // DIL_RATIO, 0, n_max - k_rows),
                            ATTN_HALF_WINDOW // DIL_RATIO)
        lanes = [pl.ds(pl.multiple_of((cls + DIL_MID * m) * LANES, LANES), LANES) for m in range(DIL_RATIO)]
        q = jnp.concatenate([q16[pl.ds(r0, q_rows), ln] for ln in lanes], axis=0)
        k = jnp.concatenate([k16[pl.ds(k0, k_rows), ln] for ln in lanes], axis=0)
        v = jnp.concatenate([v16[pl.ds(k0, k_rows), ln] for ln in lanes], axis=0)
        res = _attn_block(q, k, v, bias4_ref[edge_table(blk, n_mid)])
        for m in range(DIL_RATIO):
            store(1, DIL_MAX * r0 + DIL_MID * m + cls, q_rows, DIL_MAX, res,
                  rows=slice(m * q_rows, (m + 1) * q_rows))

        ln = pl.ds(pl.multiple_of(i * LANES, LANES), LANES)
        res = _attn_block(q16[:, ln], k16[:, ln], v16[:, ln], bias16_ref[...])
        store(2, i, n_max, DIL_MAX, res)
        return carry

    lax.fori_loop(0, n_blocks, body, 0, unroll=4)

    def combine(i, carry):
        rows = pl.ds(pl.multiple_of(i * Q_BLOCK, Q_BLOCK), Q_BLOCK)
        ms = [max_ref[b, rows, :] for b in range(3)]
        mx = jnp.maximum(jnp.maximum(ms[0], ms[1]), ms[2])
        ws = [jnp.exp2(m - mx) for m in ms]
        num = ws[0] * acc_ref[0, rows, :] + ws[1] * acc_ref[1, rows, :] + ws[2] * acc_ref[2, rows, :]
        den = ws[0] * den_ref[0, rows, :] + ws[1] * den_ref[1, rows, :] + ws[2] * den_ref[2, rows, :]
        o_ref[rows, :] = (num / den).astype(o_ref.dtype)
        return carry

    lax.fori_loop(0, n_blocks, combine, 0)


def _attn_bias_tables():
    hw = ATTN_HALF_WINDOW
    row = np.arange(Q_BLOCK)[:, None]
    col = np.arange(K_WINDOW)[None, :]

    def bias(off):
        return np.where(np.abs(off) <= hw, 0.0, NEG_INF).astype(np.float32)

    nat = np.stack([bias(col - shift - row) for shift in (0, hw, 2 * hw)])
    q_rows, k_rows = Q_BLOCK // DIL_RATIO, K_WINDOW // DIL_RATIO
    qpos = DIL_RATIO * (row % q_rows) + row // q_rows
    kpos = DIL_RATIO * (col % k_rows) + col // k_rows
    mid = np.stack([bias(kpos - shift - qpos) for shift in (0, hw, 2 * hw)])
    wide = bias(np.arange(Q_BLOCK)[None, :] - row)
    return nat, mid, wide


def _attention(qkv, qkv16, *, batch, seq):
    n3, t, _ = qkv.shape
    n_pairs = n3 // 3
    n_max = seq // DIL_MAX
    assert DILATIONS[0] == 1 and n_max == Q_BLOCK and seq % (DIL_MID * Q_BLOCK) == 0
    nat, mid, wide = _attn_bias_tables()
    in_specs = ([pl.BlockSpec((None, seq, LANES), lambda b, hp, w=w: (w * n_pairs + hp, b, 0))
                 for w in range(3)]
                + [pl.BlockSpec((None, n_max, DIL_MAX * LANES), lambda b, hp, w=w: (w * n_pairs + hp, b, 0))
                   for w in range(3)]
                + [pl.BlockSpec(nat.shape, lambda b, hp: (0, 0, 0)),
                   pl.BlockSpec(mid.shape, lambda b, hp: (0, 0, 0)),
                   pl.BlockSpec(wide.shape, lambda b, hp: (0, 0))])
    scratch = [pltpu.VMEM((3, seq, LANES), F32) for _ in range(3)]
    return pl.pallas_call(
        functools.partial(_attn_kernel, seq=seq),
        grid=(batch, n_pairs),
        in_specs=in_specs,
        out_specs=pl.BlockSpec((seq, LANES), lambda b, hp: (b, hp)),
        out_shape=jax.ShapeDtypeStruct((t, n_pairs * LANES), BF16),
        scratch_shapes=scratch,
        compiler_params=_cparams(2),
        name="dilated_attn",
    )(qkv, qkv, qkv, qkv16, qkv16, qkv16, nat, mid, wide)


def _gateprep_kernel(g_ref, o_ref, *, n_heads):
    row = lax.broadcasted_iota(jnp.int32, (MLSTM_CHUNK, MLSTM_CHUNK), 0)
    col = lax.broadcasted_iota(jnp.int32, (MLSTM_CHUNK, MLSTM_CHUNK), 1)
    tri_prefix = (col <= row).astype(F32)
    tri_suffix = (col >= row).astype(F32)
    lane = lax.broadcasted_iota(jnp.int32, (MLSTM_CHUNK, LANES), 1)
    fwd_f = (lane >= n_heads) & (lane < 2 * n_heads)
    bwd_f = (lane >= 3 * n_heads) & (lane < 4 * n_heads)
    for c in range(g_ref.shape[0] // MLSTM_CHUNK):
        g = g_ref[c * MLSTM_CHUNK:(c + 1) * MLSTM_CHUNK, :]
        log_f = jnp.minimum(g, 0.0) - jnp.log(1.0 + jnp.exp(-jnp.abs(g)))
        pre = jnp.dot(tri_prefix, log_f, preferred_element_type=F32, precision=lax.Precision.HIGHEST)
        suf = jnp.dot(tri_suffix, log_f, preferred_element_type=F32, precision=lax.Precision.HIGHEST)
        o_ref[c * MLSTM_CHUNK:(c + 1) * MLSTM_CHUNK, :] = jnp.where(fwd_f, pre, jnp.where(bwd_f, suf, g))


def _gateprep(gates, *, n_heads, tm=1024):
    t, w = gates.shape
    return pl.pallas_call(
        functools.partial(_gateprep_kernel, n_heads=n_heads),
        grid=(t // tm,),
        in_specs=[pl.BlockSpec((tm, w), lambda i: (i, 0))],
        out_specs=pl.BlockSpec((tm, w), lambda i: (i, 0)),
        out_shape=jax.ShapeDtypeStruct((t, w), F32),
        compiler_params=_cparams(1),
        name="gateprep",
    )(gates)


def _conv_silu(x_ref, w_ref, b_ref):
    x = x_ref[...].astype(F32)
    n = x.shape[0]
    taps = w_ref.shape[0]
    pad = taps // 2
    row = lax.broadcasted_iota(jnp.int32, x.shape, 0)
    acc = x * w_ref[pad:pad + 1, :] + b_ref[...]
    for j in range(taps):
        d = j - pad
        if d == 0:
            continue
        shifted = pltpu.roll(x, (-d) % n, 0)
        inside = (row + d >= 0) & (row + d < n)
        acc = acc + jnp.where(inside, shifted, 0.0) * w_ref[j:j + 1, :]
    return acc * jax.nn.sigmoid(acc)


def _mlstm_kernel(mq_ref, mk_ref, mv_ref, mo_ref, gc_ref, gr_ref, cwq_ref, cwk_ref, cbq_ref, cbk_ref,
                  hg_ref, o_ref, q_s, k_s, gc_s, hf_s, hb_s, *, n_heads):
    head = pl.program_id(1)
    seq, dh = q_s.shape
    ch = MLSTM_CHUNK
    n_chunks = seq // ch
    q_s[...] = _conv_silu(mq_ref, cwq_ref, cbq_ref).astype(q_s.dtype)
    k_s[...] = (_conv_silu(mk_ref, cwk_ref, cbk_ref) * (dh ** -0.5)).astype(k_s.dtype)
    gc_s[...] = pltpu.roll(gc_ref[...], (LANES - head) % LANES, 1)

    row = lax.broadcasted_iota(jnp.int32, (ch, ch), 0)
    col = lax.broadcasted_iota(jnp.int32, (ch, ch), 1)

    def direction(c, q_i, q_b, mask, last, state, h_store):
        c_st, n_st, m_st = state
        rs = pl.multiple_of(c * ch, ch)
        qc = q_s[pl.ds(rs, ch), :]
        kc = k_s[pl.ds(rs, ch), :]
        vc = mv_ref[pl.ds(rs, ch), :]
        g = gc_s[pl.ds(rs, ch), :]
        li_col = g[:, q_i * n_heads:q_i * n_heads + 1]
        b_col = g[:, q_b * n_heads:q_b * n_heads + 1]
        li_row = gr_ref[q_i, pl.ds(c, 1), :]
        b_row = gr_ref[q_b, pl.ds(c, 1), :]
        b_last = b_row[:, last:last + 1]
        dlog = jnp.where(mask, b_col - b_row + li_row, NEG_INF)
        inter = b_col + m_st
        m_t = jnp.maximum(inter, jnp.max(dlog, axis=-1, keepdims=True))
        s = lax.dot_general(qc, kc, (((1,), (1,)), ((), ())), preferred_element_type=F32)
        sc = s * jnp.exp(dlog - m_t)
        g_int = jnp.exp(inter - m_t)
        num = (jnp.dot(sc.astype(vc.dtype), vc, preferred_element_type=F32)
               + g_int * jnp.dot(qc, c_st.astype(qc.dtype), preferred_element_type=F32))
        den = (jnp.sum(sc, axis=-1, keepdims=True)
               + g_int * jnp.sum(qc.astype(F32) * n_st, axis=-1, keepdims=True))
        h_store[pl.ds(rs, ch), :] = num / jnp.maximum(jnp.abs(den), jnp.exp(-m_t))
        wlog = b_last - b_col + li_col
        m_new = jnp.maximum(b_last + m_st, jnp.max(wlog, axis=0, keepdims=True))
        kw = kc.astype(F32) * jnp.exp(wlog - m_new)
        decay = jnp.exp(b_last + m_st - m_new)
        c_new = decay * c_st + lax.dot_general(kw.astype(vc.dtype), vc, (((0,), (0,)), ((), ())),
                                               preferred_element_type=F32)
        n_new = decay * n_st + jnp.sum(kw, axis=0, keepdims=True)
        return c_new, n_new, m_new

    def step(j, carry):
        fwd, bwd = carry
        fwd = direction(j, 0, 1, col <= row, ch - 1, fwd, hf_s)
        bwd = direction(n_chunks - 1 - j, 2, 3, col >= row, 0, bwd, hb_s)
        return fwd, bwd

    zero = (jnp.zeros((dh, dh), F32), jnp.zeros((1, dh), F32), jnp.zeros((1, 1), F32))
    lax.fori_loop(0, n_chunks, step, (zero, zero))

    hm = hf_s[...] + hb_s[...]
    hm = hm * lax.rsqrt(jnp.mean(hm * hm, axis=-1, keepdims=True) + NORM_EPS)
    o_ref[...] = (hm * hg_ref[...] * jax.nn.sigmoid(mo_ref[...].astype(F32))).astype(o_ref.dtype)


def _mlstm(mall, gc, gr, conv_w, conv_b, head_g, *, batch, seq, n_heads):
    t = mall.shape[0]
    dh = mall.shape[1] // (4 * n_heads)
    taps = conv_w.shape[0]
    n_chunks = seq // MLSTM_CHUNK

    def col_block(offset):
        return pl.BlockSpec((seq, dh), lambda b, h, offset=offset: (b, offset * n_heads + h))

    in_specs = [col_block(0), col_block(1), col_block(2), col_block(3),
                pl.BlockSpec((seq, LANES), lambda b, h: (b, 0)),
                pl.BlockSpec((None, 4, None, n_chunks, MLSTM_CHUNK), lambda b, h: (b, 0, h, 0, 0)),
                pl.BlockSpec((taps, dh), lambda b, h: (0, h)),
                pl.BlockSpec((taps, dh), lambda b, h: (0, n_heads + h)),
                pl.BlockSpec((1, dh), lambda b, h: (0, h)),
                pl.BlockSpec((1, dh), lambda b, h: (0, n_heads + h)),
                pl.BlockSpec((1, dh), lambda b, h: (0, h))]
    scratch = [pltpu.VMEM((seq, dh), BF16), pltpu.VMEM((seq, dh), BF16),
               pltpu.VMEM((seq, LANES), F32), pltpu.VMEM((seq, dh), F32), pltpu.VMEM((seq, dh), F32)]
    return pl.pallas_call(
        functools.partial(_mlstm_kernel, n_heads=n_heads),
        grid=(batch, n_heads),
        in_specs=in_specs,
        out_specs=pl.BlockSpec((seq, dh), lambda b, h: (b, h)),
        out_shape=jax.ShapeDtypeStruct((t, n_heads * dh), BF16),
        scratch_shapes=scratch,
        compiler_params=_cparams(2),
        name="mlstm",
    )(mall, mall, mall, mall, gc, gr, conv_w, conv_w, conv_b, conv_b, head_g)


def _outproj_kernel(a_ref, m_ref, w_ref, x_ref, gpost_ref, gpre_ref, x1_ref, hn_ref):
    ka = a_ref.shape[1]
    mixed = (jnp.dot(a_ref[...], w_ref[:ka, :], preferred_element_type=F32)
             + jnp.dot(m_ref[...], w_ref[ka:, :], preferred_element_type=F32))
    x1 = x_ref[...] + _rms(mixed, gpost_ref[...])
    x1_ref[...] = x1
    hn_ref[...] = _rms(x1, gpre_ref[...]).astype(hn_ref.dtype)


def _outproj(attn, ml, w_out, x, g_post, g_pre, tm=512):
    t, d = x.shape
    ka, km = attn.shape[1], ml.shape[1]
    return pl.pallas_call(
        _outproj_kernel,
        grid=(t // tm,),
        in_specs=[pl.BlockSpec((tm, ka), lambda i: (i, 0)),
                  pl.BlockSpec((tm, km), lambda i: (i, 0)),
                  pl.BlockSpec((ka + km, d), lambda i: (0, 0)),
                  pl.BlockSpec((tm, d), lambda i: (i, 0)),
                  pl.BlockSpec((1, d), lambda i: (0, 0)),
                  pl.BlockSpec((1, d), lambda i: (0, 0))],
        out_specs=[pl.BlockSpec((tm, d), lambda i: (i, 0)),
                   pl.BlockSpec((tm, d), lambda i: (i, 0))],
        out_shape=[jax.ShapeDtypeStruct((t, d), F32), jax.ShapeDtypeStruct((t, d), BF16)],
        compiler_params=_cparams(1),
        name="outproj",
    )(attn, ml, w_out, x, g_post.reshape(1, d), g_pre.reshape(1, d))


FFN_HALO = 16
FFN_CHUNK = 256


def _ffn_kernel(prev_ref, main_ref, next_ref, x1_ref, wup_ref, wdn_ref, cw_ref, cb_ref,
                gpost_ref, gnext_ref, x2_ref, hn_ref, acc_ref, *, tiles_per_seq):
    i = pl.program_id(0)
    tm = main_ref.shape[0]
    d_ff = wdn_ref.shape[0]
    pos = i % tiles_per_seq
    prev = jnp.where(pos == 0, jnp.zeros_like(prev_ref[...]), prev_ref[...])
    nxt = jnp.where(pos == tiles_per_seq - 1, jnp.zeros_like(next_ref[...]), next_ref[...])
    lhs = jnp.concatenate([prev, main_ref[...], nxt], axis=0)

    def conv(u, c0):
        w = cw_ref[:, c0:c0 + FFN_CHUNK]
        out = u[FFN_HALO:FFN_HALO + tm] * w[1:2] + cb_ref[:, c0:c0 + FFN_CHUNK]
        out = out + u[FFN_HALO - 1:FFN_HALO - 1 + tm] * w[0:1]
        return out + u[FFN_HALO + 1:FFN_HALO + 1 + tm] * w[2:3]

    for c in range(d_ff // FFN_CHUNK):
        c0 = c * FFN_CHUNK
        gate = conv(jnp.dot(lhs, wup_ref[:, c0:c0 + FFN_CHUNK], preferred_element_type=F32), c0)
        val = conv(jnp.dot(lhs, wup_ref[:, d_ff + c0:d_ff + c0 + FFN_CHUNK],
                           preferred_element_type=F32), d_ff + c0)
        inner = np.sqrt(2.0 / np.pi) * (gate + 0.044715 * (gate * gate * gate))
        act = (0.5 * gate * (1.0 + jnp.tanh(inner)) * val).astype(lhs.dtype)
        part = jnp.dot(act, wdn_ref[c0:c0 + FFN_CHUNK, :], preferred_element_type=F32)
        if c == 0:
            acc_ref[...] = part
        else:
            acc_ref[...] += part

    x2 = x1_ref[...] + _rms(acc_ref[...], gpost_ref[...])
    x2_ref[...] = x2
    hn_ref[...] = _rms(x2, gnext_ref[...]).astype(hn_ref.dtype)


def _ffn(hn, x1, w_up, w_down, conv_w, conv_b, g_post, g_next, *, seq, tm=512):
    t, d = x1.shape
    d_ff = w_down.shape[0]
    tiles_per_seq = seq // tm
    halo_per_tile = tm // FFN_HALO
    n_halo_blocks = t // FFN_HALO
    return pl.pallas_call(
        functools.partial(_ffn_kernel, tiles_per_seq=tiles_per_seq),
        grid=(t // tm,),
        in_specs=[pl.BlockSpec((FFN_HALO, d), lambda i: (jnp.maximum(i * halo_per_tile - 1, 0), 0)),
                  pl.BlockSpec((tm, d), lambda i: (i, 0)),
                  pl.BlockSpec((FFN_HALO, d),
                               lambda i: (jnp.minimum((i + 1) * halo_per_tile, n_halo_blocks - 1), 0)),
                  pl.BlockSpec((tm, d), lambda i: (i, 0)),
                  pl.BlockSpec((d, 2 * d_ff), lambda i: (0, 0)),
                  pl.BlockSpec((d_ff, d), lambda i: (0, 0)),
                  pl.BlockSpec(conv_w.shape, lambda i: (0, 0)),
                  pl.BlockSpec((1, 2 * d_ff), lambda i: (0, 0)),
                  pl.BlockSpec((1, d), lambda i: (0, 0)),
                  pl.BlockSpec((1, d), lambda i: (0, 0))],
        out_specs=[pl.BlockSpec((tm, d), lambda i: (i, 0)),
                   pl.BlockSpec((tm, d), lambda i: (i, 0))],
        out_shape=[jax.ShapeDtypeStruct((t, d), F32), jax.ShapeDtypeStruct((t, d), BF16)],
        scratch_shapes=[pltpu.VMEM((tm, d), F32)],
        compiler_params=_cparams(1),
        name="convffn",
    )(hn, hn, hn, x1, w_up, w_down, conv_w, conv_b.reshape(1, -1), g_post.reshape(1, d),
      g_next.reshape(1, d))


def _rotary_tables(seq):
    half = ATTN_HEAD_DIM // 2
    inv_freq = ROPE_THETA ** (-jnp.arange(0, ATTN_HEAD_DIM, 2, dtype=F32) / ATTN_HEAD_DIM)
    ang = jnp.arange(seq, dtype=F32)[:, None] * inv_freq[None, :]
    cos, sin = jnp.cos(ang), jnp.sin(ang)
    reps = LANES // ATTN_HEAD_DIM
    cos_t = jnp.tile(jnp.concatenate([cos, cos], axis=-1), (1, reps))
    sin_t = jnp.tile(jnp.concatenate([-sin, sin], axis=-1), (1, reps))
    return cos_t, sin_t


def kernel(x, mix_pre_g, w_in, mlstm_conv_w, mlstm_conv_b, mlstm_gate_b, mlstm_head_g, w_out,
           mix_post_g, ffn_pre_g, w_up, ffn_conv_w, ffn_conv_b, w_down, ffn_post_g):
    batch, seq, d = x.shape
    depth = w_in.shape[0]
    t = batch * seq
    n_heads = MLSTM_HEADS
    n_gates = mlstm_gate_b.shape[1]
    mix_w = mlstm_head_g.shape[1]
    attn_w = w_out.shape[1] - mix_w
    n_chunks = seq // MLSTM_CHUNK
    assert n_gates == 4 * n_heads and n_gates <= LANES
    assert w_in.shape[2] == 3 * attn_w + 4 * mix_w + n_gates

    cos_t, sin_t = _rotary_tables(seq)
    xf = x.reshape(t, d)
    hn = _prenorm(xf, mix_pre_g[0])
    for l in range(depth):
        w = w_in[l].astype(BF16)
        n_main = 3 * attn_w + 4 * mix_w
        w_gate = jnp.pad(w[:, n_main:], ((0, 0), (0, LANES - n_gates)))
        gate_b = jnp.pad(mlstm_gate_b[l], (0, LANES - n_gates)).reshape(1, LANES)
        qkv, qkv16, mall, gates = _inproj(hn, w[:, :n_main], w_gate, gate_b, cos_t, sin_t,
                                          attn_w=attn_w, mix_w=mix_w, seq=seq)
        attn = _attention(qkv, qkv16, batch=batch, seq=seq)
        gc = _gateprep(gates, n_heads=n_heads)
        gr = gc[:, :n_gates].reshape(batch, seq, n_gates).transpose(0, 2, 1)
        gr = gr.reshape(batch, 4, n_heads, n_chunks, MLSTM_CHUNK)
        ml = _mlstm(mall, gc, gr, mlstm_conv_w[l], mlstm_conv_b[l].reshape(1, -1),
                    mlstm_head_g[l].reshape(1, -1), batch=batch, seq=seq, n_heads=n_heads)
        x1, hn2 = _outproj(attn, ml, w_out[l].astype(BF16), xf, mix_post_g[l], ffn_pre_g[l])
        g_next = mix_pre_g[l + 1] if l + 1 < depth else mix_pre_g[l]
        xf, hn = _ffn(hn2, x1, w_up[l].astype(BF16), w_down[l].astype(BF16), ffn_conv_w[l],
                      ffn_conv_b[l], ffn_post_g[l], g_next, seq=seq)
    return xf.reshape(batch, seq, d)
```

```python
import functools

import numpy as np
import jax
import jax.numpy as jnp
from jax import lax
from jax.experimental import pallas as pl
from jax.experimental.pallas import tpu as pltpu

F32 = jnp.float32
BF16 = jnp.bfloat16

ATTN_HEAD_DIM = 64
ATTN_HALF_WINDOW = 64
DILATIONS = (1, 4, 16)
MLSTM_HEADS = 4
MLSTM_CHUNK = 128
ROPE_THETA = 10000.0
NORM_EPS = 1e-6
NEG_INF = -1e30

LANES = 128
Q_BLOCK = 128
K_WINDOW = Q_BLOCK + 2 * ATTN_HALF_WINDOW
DIL_MID, DIL_MAX = DILATIONS[1], DILATIONS[2]
DIL_RATIO = DIL_MAX // DIL_MID
ATTN_Q_SCALE = float(ATTN_HEAD_DIM ** -0.5 * np.log2(np.e))
VMEM_LIMIT_BYTES = 56 * 1024 * 1024


def _cparams(n_grid_dims):
    return pltpu.CompilerParams(
        dimension_semantics=("arbitrary",) * n_grid_dims,
        vmem_limit_bytes=VMEM_LIMIT_BYTES)


def _rms(x, g):
    return x * lax.rsqrt(jnp.mean(x * x, axis=-1, keepdims=True) + NORM_EPS) * g


def _prenorm_kernel(x_ref, g_ref, o_ref):
    o_ref[...] = _rms(x_ref[...], g_ref[...]).astype(o_ref.dtype)


def _prenorm(x, g, tm=1024):
    t, d = x.shape
    return pl.pallas_call(
        _prenorm_kernel,
        grid=(t // tm,),
        in_specs=[pl.BlockSpec((tm, d), lambda i: (i, 0)),
                  pl.BlockSpec((1, d), lambda i: (0, 0))],
        out_specs=pl.BlockSpec((tm, d), lambda i: (i, 0)),
        out_shape=jax.ShapeDtypeStruct((t, d), BF16),
        compiler_params=_cparams(1),
        name="prenorm",
    )(x, g.reshape(1, d))


def _inproj_kernel(hn_ref, w_ref, wg_ref, gb_ref, cos_ref, sin_ref,
                   qkv_ref, qkv16_ref, mall_ref, gates_ref, stage_ref, *, attn_w, mix_w):
    hn = hn_ref[...]
    cos = cos_ref[...]
    sin = sin_ref[...]
    tm = hn.shape[0]
    lane = lax.broadcasted_iota(jnp.int32, (tm, LANES), 1)
    first_half = (lane % ATTN_HEAD_DIM) < (ATTN_HEAD_DIM // 2)
    n_pairs = attn_w // LANES
    for grp in range(3):
        res = jnp.dot(hn, w_ref[:, grp * attn_w:(grp + 1) * attn_w],
                      preferred_element_type=F32)
        for hp in range(n_pairs):
            xs = res[:, hp * LANES:(hp + 1) * LANES]
            if grp < 2:
                rot = jnp.where(first_half,
                                pltpu.roll(xs, LANES - ATTN_HEAD_DIM // 2, 1),
                                pltpu.roll(xs, ATTN_HEAD_DIM // 2, 1))
                xs = xs * cos + rot * sin
            if grp == 0:
                xs = xs * ATTN_Q_SCALE
            slab = grp * n_pairs + hp
            qkv_ref[slab] = xs.astype(qkv_ref.dtype)
            stage_ref[slab] = xs
            for r in range(DIL_MAX):
                qkv16_ref[slab, :, r * LANES:(r + 1) * LANES] = (
                    stage_ref[slab, pl.ds(r, tm // DIL_MAX, stride=DIL_MAX), :].astype(qkv16_ref.dtype))
    base = 3 * attn_w
    for grp in range(4):
        res = jnp.dot(hn, w_ref[:, base + grp * mix_w: base + (grp + 1) * mix_w],
                      preferred_element_type=F32)
        mall_ref[:, grp * mix_w:(grp + 1) * mix_w] = res.astype(mall_ref.dtype)
    gates_ref[...] = jnp.dot(hn, wg_ref[...], preferred_element_type=F32) + gb_ref[...]


def _inproj(hn, w_main, w_gate, gate_b, cos, sin, *, attn_w, mix_w, seq, tm=512):
    t, d = hn.shape
    n_pairs = attn_w // LANES
    n_main = w_main.shape[1]
    tiles_per_seq = seq // tm
    kern = functools.partial(_inproj_kernel, attn_w=attn_w, mix_w=mix_w)
    return pl.pallas_call(
        kern,
        grid=(t // tm,),
        in_specs=[pl.BlockSpec((tm, d), lambda i: (i, 0)),
                  pl.BlockSpec((d, n_main), lambda i: (0, 0)),
                  pl.BlockSpec((d, LANES), lambda i: (0, 0)),
                  pl.BlockSpec((1, LANES), lambda i: (0, 0)),
                  pl.BlockSpec((tm, LANES), lambda i: (i % tiles_per_seq, 0)),
                  pl.BlockSpec((tm, LANES), lambda i: (i % tiles_per_seq, 0))],
        out_specs=[pl.BlockSpec((3 * n_pairs, tm, LANES), lambda i: (0, i, 0)),
                   pl.BlockSpec((3 * n_pairs, tm // DIL_MAX, DIL_MAX * LANES), lambda i: (0, i, 0)),
                   pl.BlockSpec((tm, 4 * mix_w), lambda i: (i, 0)),
                   pl.BlockSpec((tm, LANES), lambda i: (i, 0))],
        out_shape=[jax.ShapeDtypeStruct((3 * n_pairs, t, LANES), BF16),
                   jax.ShapeDtypeStruct((3 * n_pairs, t // DIL_MAX, DIL_MAX * LANES), BF16),
                   jax.ShapeDtypeStruct((t, 4 * mix_w), BF16),
                   jax.ShapeDtypeStruct((t, LANES), F32)],
        scratch_shapes=[pltpu.VMEM((3 * n_pairs, tm, LANES), F32)],
        compiler_params=_cparams(1),
        name="inproj",
    )(hn, w_main, w_gate, gate_b, cos, sin)


def _attn_block(q, k, v, bias):
    nk = k.shape[0]

    def head_masks(rows):
        lane = lax.broadcasted_iota(jnp.int32, (rows, LANES), 1)
        h0 = jnp.where(lane < ATTN_HEAD_DIM, 1.0, 0.0).astype(q.dtype)
        return h0, (1.0 - h0.astype(F32)).astype(q.dtype)

    q0, q1 = head_masks(Q_BLOCK)
    v0, v1 = head_masks(nk)
    q_st = jnp.concatenate([q * q0, q * q1], axis=0)
    s = lax.dot_general(q_st, k, (((1,), (1,)), ((), ())), preferred_element_type=F32)
    s = s + jnp.concatenate([bias, bias], axis=0)
    m = jnp.max(s, axis=-1, keepdims=True)
    p = jnp.exp2(s - m).astype(v.dtype)
    p_cat = jnp.concatenate([p[:Q_BLOCK], p[Q_BLOCK:]], axis=1)
    v_ext = jnp.concatenate([jnp.concatenate([v * v0, v0], axis=1),
                             jnp.concatenate([v * v1, v1], axis=1)], axis=0)
    o = jnp.dot(p_cat, v_ext, preferred_element_type=F32)
    lane_o = lax.broadcasted_iota(jnp.int32, (Q_BLOCK, LANES), 1)
    m_lanes = jnp.where(lane_o < ATTN_HEAD_DIM, m[:Q_BLOCK], m[Q_BLOCK:])
    return o[:, :LANES], o[:, LANES:], m_lanes


def _attn_kernel(qn, kn, vn, q16, k16, v16, bias1_ref, bias4_ref, bias16_ref, o_ref,
                 acc_ref, den_ref, max_ref, *, seq):
    n_blocks = seq // Q_BLOCK
    n_mid = (seq // DIL_MID) // Q_BLOCK
    n_max = seq // DIL_MAX
    q_rows = Q_BLOCK // DIL_RATIO
    k_rows = K_WINDOW // DIL_RATIO

    def edge_table(blk, n):
        return jnp.where(blk > 0, 1, 0) + jnp.where(blk == n - 1, 1, 0)

    def store(branch, start, size, stride, vals, rows=slice(None)):
        for ref, val in zip((acc_ref, den_ref, max_ref), vals):
            idx = pl.ds(start, size) if stride == 1 else pl.ds(start, size, stride=stride)
            ref[branch, idx, :] = val[rows]

    def body(i, carry):
        qs = pl.multiple_of(i * Q_BLOCK, Q_BLOCK)
        ks = pl.multiple_of(jnp.clip(qs - ATTN_HALF_WINDOW, 0, seq - K_WINDOW), ATTN_HALF_WINDOW)
        res = _attn_block(qn[pl.ds(qs, Q_BLOCK), :], kn[pl.ds(ks, K_WINDOW), :], vn[pl.ds(ks, K_WINDOW), :],
                          bias1_ref[edge_table(i, n_blocks)])
        store(0, qs, Q_BLOCK, 1, res)

        cls = i // n_mid
        blk = i % n_mid
        r0 = pl.multiple_of(blk * q_rows, q_rows)
        k0 = pl.multiple_of(jnp.clip(r0 - ATTN_HALF_WINDOW // DIL_RATIO, 0, n_max - k_rows),
                            ATTN_HALF_WINDOW // DIL_RATIO)
        lanes = [pl.ds(pl.multiple_of((cls + DIL_MID * m) * LANES, LANES), LANES) for m in range(DIL_RATIO)]
        q = jnp.concatenate([q16[pl.ds(r0, q_rows), ln] for ln in lanes], axis=0)
        k = jnp.concatenate([k16[pl.ds(k0, k_rows), ln] for ln in lanes], axis=0)
        v = jnp.concatenate([v16[pl.ds(k0, k_rows), ln] for ln in lanes], axis=0)
        res = _attn_block(q, k, v, bias4_ref[edge_table(blk, n_mid)])
        for m in range(DIL_RATIO):
            store(1, DIL_MAX * r0 + DIL_MID * m + cls, q_rows, DIL_MAX, res,
                  rows=slice(m * q_rows, (m + 1) * q_rows))

        ln = pl.ds(pl.multiple_of(i * LANES, LANES), LANES)
        res = _attn_block(q16[:, ln], k16[:, ln], v16[:, ln], bias16_ref[...])
        store(2, i, n_max, DIL_MAX, res)
        return carry

    lax.fori_loop(0, n_blocks, body, 0, unroll=4)

    def combine(i, carry):
        rows = pl.ds(pl.multiple_of(i * Q_BLOCK, Q_BLOCK), Q_BLOCK)
        ms = [max_ref[b, rows, :] for b in range(3)]
        mx = jnp.maximum(jnp.maximum(ms[0], ms[1]), ms[2])
        ws = [jnp.exp2(m - mx) for m in ms]
        num = ws[0] * acc_ref[0, rows, :] + ws[1] * acc_ref[1, rows, :] + ws[2] * acc_ref[2, rows, :]
        den = ws[0] * den_ref[0, rows, :] + ws[1] * den_ref[1, rows, :] + ws[2] * den_ref[2, rows, :]
        o_ref[rows, :] = (num / den).astype(o_ref.dtype)
        return carry

    lax.fori_loop(0, n_blocks, combine, 0)


def _attn_bias_tables():
    hw = ATTN_HALF_WINDOW
    row = np.arange(Q_BLOCK)[:, None]
    col = np.arange(K_WINDOW)[None, :]

    def bias(off):
        return np.where(np.abs(off) <= hw, 0.0, NEG_INF).astype(np.float32)

    nat = np.stack([bias(col - shift - row) for shift in (0, hw, 2 * hw)])
    q_rows, k_rows = Q_BLOCK // DIL_RATIO, K_WINDOW // DIL_RATIO
    qpos = DIL_RATIO * (row % q_rows) + row // q_rows
    kpos = DIL_RATIO * (col % k_rows) + col // k_rows
    mid = np.stack([bias(kpos - shift - qpos) for shift in (0, hw, 2 * hw)])
    wide = bias(np.arange(Q_BLOCK)[None, :] - row)
    return nat, mid, wide


def _attention(qkv, qkv16, *, batch, seq):
    n3, t, _ = qkv.shape
    n_pairs = n3 // 3
    n_max = seq // DIL_MAX
    assert DILATIONS[0] == 1 and n_max == Q_BLOCK and seq % (DIL_MID * Q_BLOCK) == 0
    nat, mid, wide = _attn_bias_tables()
    in_specs = ([pl.BlockSpec((None, seq, LANES), lambda b, hp, w=w: (w * n_pairs + hp, b, 0))
                 for w in range(3)]
                + [pl.BlockSpec((None, n_max, DIL_MAX * LANES), lambda b, hp, w=w: (w * n_pairs + hp, b, 0))
                   for w in range(3)]
                + [pl.BlockSpec(nat.shape, lambda b, hp: (0, 0, 0)),
                   pl.BlockSpec(mid.shape, lambda b, hp: (0, 0, 0)),
                   pl.BlockSpec(wide.shape, lambda b, hp: (0, 0))])
    scratch = [pltpu.VMEM((3, seq, LANES), F32) for _ in range(3)]
    return pl.pallas_call(
        functools.partial(_attn_kernel, seq=seq),
        grid=(batch, n_pairs),
        in_specs=in_specs,
        out_specs=pl.BlockSpec((seq, LANES), lambda b, hp: (b, hp)),
        out_shape=jax.ShapeDtypeStruct((t, n_pairs * LANES), BF16),
        scratch_shapes=scratch,
        compiler_params=_cparams(2),
        name="dilated_attn",
    )(qkv, qkv, qkv, qkv16, qkv16, qkv16, nat, mid, wide)


GATE_QUANTITIES = 6


def _gateprep_kernel(g_ref, o_ref, *, n_heads):
    ch = MLSTM_CHUNK
    row = lax.broadcasted_iota(jnp.int32, (ch, ch), 0)
    col = lax.broadcasted_iota(jnp.int32, (ch, ch), 1)
    tri_prefix = (col <= row).astype(F32)
    tri_suffix = (col >= row).astype(F32)
    lane = lax.broadcasted_iota(jnp.int32, (ch, LANES), 1)
    srow = lax.broadcasted_iota(jnp.int32, (ch, LANES), 0)

    def to_lane0(x, src):
        return x if src == 0 else pltpu.roll(x, LANES - src, 1)

    def running_max(a, reverse):
        step = 1
        while step < ch:
            if reverse:
                shifted = jnp.where(srow + step < ch, pltpu.roll(a, ch - step, 0), -jnp.inf)
            else:
                shifted = jnp.where(srow >= step, pltpu.roll(a, step, 0), -jnp.inf)
            a = jnp.maximum(a, shifted)
            step *= 2
        return a

    for c in range(g_ref.shape[0] // ch):
        g = g_ref[c * ch:(c + 1) * ch, :]
        log_f = jnp.minimum(g, 0.0) - jnp.log(1.0 + jnp.exp(-jnp.abs(g)))
        pre = jnp.dot(tri_prefix, log_f, preferred_element_type=F32, precision=lax.Precision.HIGHEST)
        suf = jnp.dot(tri_suffix, log_f, preferred_element_type=F32, precision=lax.Precision.HIGHEST)
        b_f = to_lane0(pre, n_heads)
        a_f = g - b_f
        b_b = to_lane0(suf, 3 * n_heads)
        a_b = to_lane0(g, 2 * n_heads) - b_b
        parts = (b_f, running_max(a_f, False), a_f, b_b, running_max(a_b, True), a_b)
        out = jnp.zeros((ch, LANES), F32)
        for q, part in enumerate(parts):
            placed = part if q == 0 else pltpu.roll(part, q * n_heads, 1)
            out = jnp.where((lane >= q * n_heads) & (lane < (q + 1) * n_heads), placed, out)
        o_ref[c * ch:(c + 1) * ch, :] = out


def _gateprep(gates, *, n_heads, tm=1024):
    t, w = gates.shape
    assert GATE_QUANTITIES * n_heads <= w
    return pl.pallas_call(
        functools.partial(_gateprep_kernel, n_heads=n_heads),
        grid=(t // tm,),
        in_specs=[pl.BlockSpec((tm, w), lambda i: (i, 0))],
        out_specs=pl.BlockSpec((tm, w), lambda i: (i, 0)),
        out_shape=jax.ShapeDtypeStruct((t, w), F32),
        compiler_params=_cparams(1),
        name="gateprep",
    )(gates)


CONV_HALO = 16


def _conv_silu_chunks(x_ref, w_ref, b_ref, stage_ref, emit):
    n = x_ref.shape[0]
    taps = w_ref.shape[0]
    pad = taps // 2
    zeros = jnp.zeros((CONV_HALO, x_ref.shape[1]), F32)
    stage_ref[0:CONV_HALO, :] = zeros
    stage_ref[CONV_HALO + n:, :] = zeros
    stage_ref[CONV_HALO:CONV_HALO + n, :] = x_ref[...].astype(F32)

    def body(c, carry):
        base = pl.multiple_of(c * MLSTM_CHUNK, MLSTM_CHUNK)
        acc = b_ref[...]
        for j in range(taps):
            acc = acc + stage_ref[pl.ds(base + (CONV_HALO + j - pad), MLSTM_CHUNK), :] * w_ref[j:j + 1, :]
        emit(c, acc * jax.nn.sigmoid(acc))
        return carry

    lax.fori_loop(0, n // MLSTM_CHUNK, body, 0, unroll=2)


def _mlstm_kernel(mq_ref, mk_ref, mv_ref, mo_ref, gc_ref, gr_ref, cwq_ref, cwk_ref, cbq_ref, cbk_ref,
                  hg_ref, o_ref, q_s, kt_s, vx_s, gc_s, sc_s, hf_s, hb_s, stage_s, *, n_heads):
    head = pl.program_id(1)
    seq, dh = q_s.shape
    ch = MLSTM_CHUNK
    n_chunks = seq // ch
    def emit_q(c, y):
        q_s[pl.ds(pl.multiple_of(c * ch, ch), ch), :] = y.astype(q_s.dtype)

    def emit_k(c, y):
        kt_s[:, pl.ds(pl.multiple_of(c * ch, ch), ch)] = (y * (dh ** -0.5)).T.astype(kt_s.dtype)

    _conv_silu_chunks(mq_ref, cwq_ref, cbq_ref, stage_s, emit_q)
    _conv_silu_chunks(mk_ref, cwk_ref, cbk_ref, stage_s, emit_k)
    vx_s[:, :dh] = mv_ref[...]
    vx_s[:, dh:] = jnp.ones((seq, dh), vx_s.dtype)
    gc_s[...] = pltpu.roll(gc_ref[...], (LANES - head) % LANES, 1)

    for d in range(2):
        lane_b, lane_cm = (3 * d) * n_heads, (3 * d + 1) * n_heads
        m_run = jnp.zeros((1, 2 * dh), F32)
        for step in range(n_chunks):
            c = step if d == 0 else n_chunks - 1 - step
            last = c * ch + (ch - 1 if d == 0 else 0)
            b_last = jnp.broadcast_to(gc_s[last:last + 1, lane_b:lane_b + 1], (1, 2 * dh))
            a_max = jnp.broadcast_to(gc_s[last:last + 1, lane_cm:lane_cm + 1], (1, 2 * dh))
            w_max = jnp.maximum(m_run, a_max)
            sc_s[d, 0, c:c + 1, :] = m_run
            sc_s[d, 1, c:c + 1, :] = w_max
            sc_s[d, 2, c:c + 1, :] = jnp.exp(m_run - w_max)
            m_run = b_last + w_max

    row = lax.broadcasted_iota(jnp.int32, (ch, ch), 0)
    col = lax.broadcasted_iota(jnp.int32, (ch, ch), 1)

    def direction(c, d, mask, state, h_store):
        lane_b, lane_cm = (3 * d) * n_heads, (3 * d + 1) * n_heads
        rs = pl.multiple_of(c * ch, ch)
        qc = q_s[pl.ds(rs, ch), :]
        ktc = kt_s[:, pl.ds(rs, ch)]
        vx = vx_s[pl.ds(rs, ch), :]
        a_row = gr_ref[d, pl.ds(c, 1), :]
        m_row = sc_s[d, 0, pl.ds(c, 1), :]
        w_max = sc_s[d, 1, pl.ds(c, 1), :]
        decay = sc_s[d, 2, pl.ds(c, 1), :]
        g = gc_s[pl.ds(rs, ch), :]
        b_bc = jnp.broadcast_to(g[:, lane_b:lane_b + 1], (ch, dh))
        cm_bc = jnp.broadcast_to(g[:, lane_cm:lane_cm + 1], (ch, dh))
        mm = jnp.maximum(m_row[:, :dh], cm_bc)
        w = jnp.where(mask, jnp.exp(a_row - mm), 0.0)
        s = jnp.dot(qc, ktc, preferred_element_type=F32)
        intra = jnp.dot((s * w).astype(vx.dtype), vx, preferred_element_type=F32)
        inter = jnp.dot(qc, state.astype(qc.dtype), preferred_element_type=F32)
        g_int = jnp.exp(m_row[:, :dh] - mm)
        num = intra[:, :dh] + g_int * inter[:, :dh]
        den = intra[:, dh:] + g_int * inter[:, dh:]
        h_store[pl.ds(rs, ch), :] = num / jnp.maximum(jnp.abs(den), jnp.exp(-(b_bc + mm)))
        kw_t = (ktc.astype(F32) * jnp.exp(a_row - w_max[:, :ch])).astype(vx.dtype)
        return decay * state + jnp.dot(kw_t, vx, preferred_element_type=F32)

    def step(j, carry):
        fwd, bwd = carry
        fwd = direction(j, 0, col <= row, fwd, hf_s)
        bwd = direction(n_chunks - 1 - j, 1, col >= row, bwd, hb_s)
        return fwd, bwd

    zero = jnp.zeros((dh, 2 * dh), F32)
    lax.fori_loop(0, n_chunks, step, (zero, zero), unroll=4)

    hm = hf_s[...] + hb_s[...]
    hm = hm * lax.rsqrt(jnp.mean(hm * hm, axis=-1, keepdims=True) + NORM_EPS)
    o_ref[...] = (hm * hg_ref[...] * jax.nn.sigmoid(mo_ref[...].astype(F32))).astype(o_ref.dtype)


def _mlstm(mall, gc, gr, conv_w, conv_b, head_g, *, batch, seq, n_heads):
    t = mall.shape[0]
    dh = mall.shape[1] // (4 * n_heads)
    taps = conv_w.shape[0]
    n_chunks = seq // MLSTM_CHUNK
    assert dh == MLSTM_CHUNK == LANES

    def col_block(offset):
        return pl.BlockSpec((seq, dh), lambda b, h, offset=offset: (b, offset * n_heads + h))

    in_specs = [col_block(0), col_block(1), col_block(2), col_block(3),
                pl.BlockSpec((seq, LANES), lambda b, h: (b, 0)),
                pl.BlockSpec((None, 2, None, n_chunks, MLSTM_CHUNK), lambda b, h: (b, 0, h, 0, 0)),
                pl.BlockSpec((taps, dh), lambda b, h: (0, h)),
                pl.BlockSpec((taps, dh), lambda b, h: (0, n_heads + h)),
                pl.BlockSpec((1, dh), lambda b, h: (0, h)),
                pl.BlockSpec((1, dh), lambda b, h: (0, n_heads + h)),
                pl.BlockSpec((1, dh), lambda b, h: (0, h))]
    scratch = [pltpu.VMEM((seq, dh), BF16),
               pltpu.VMEM((dh, seq), BF16),
               pltpu.VMEM((seq, 2 * dh), BF16),
               pltpu.VMEM((seq, LANES), F32),
               pltpu.VMEM((2, 3, n_chunks, 2 * dh), F32),
               pltpu.VMEM((seq, dh), F32), pltpu.VMEM((seq, dh), F32),
               pltpu.VMEM((seq + 2 * CONV_HALO, dh), F32)]
    return pl.pallas_call(
        functools.partial(_mlstm_kernel, n_heads=n_heads),
        grid=(batch, n_heads),
        in_specs=in_specs,
        out_specs=pl.BlockSpec((seq, dh), lambda b, h: (b, h)),
        out_shape=jax.ShapeDtypeStruct((t, n_heads * dh), BF16),
        scratch_shapes=scratch,
        compiler_params=_cparams(2),
        name="mlstm",
    )(mall, mall, mall, mall, gc, gr, conv_w, conv_w, conv_b, conv_b, head_g)


def _outproj_kernel(a_ref, m_ref, w_ref, x_ref, gpost_ref, gpre_ref, x1_ref, hn_ref):
    ka = a_ref.shape[1]
    mixed = (jnp.dot(a_ref[...], w_ref[:ka, :], preferred_element_type=F32)
             + jnp.dot(m_ref[...], w_ref[ka:, :], preferred_element_type=F32))
    x1 = x_ref[...] + _rms(mixed, gpost_ref[...])
    x1_ref[...] = x1
    hn_ref[...] = _rms(x1, gpre_ref[...]).astype(hn_ref.dtype)


def _outproj(attn, ml, w_out, x, g_post, g_pre, tm=512):
    t, d = x.shape
    ka, km = attn.shape[1], ml.shape[1]
    return pl.pallas_call(
        _outproj_kernel,
        grid=(t // tm,),
        in_specs=[pl.BlockSpec((tm, ka), lambda i: (i, 0)),
                  pl.BlockSpec((tm, km), lambda i: (i, 0)),
                  pl.BlockSpec((ka + km, d), lambda i: (0, 0)),
                  pl.BlockSpec((tm, d), lambda i: (i, 0)),
                  pl.BlockSpec((1, d), lambda i: (0, 0)),
                  pl.BlockSpec((1, d), lambda i: (0, 0))],
        out_specs=[pl.BlockSpec((tm, d), lambda i: (i, 0)),
                   pl.BlockSpec((tm, d), lambda i: (i, 0))],
        out_shape=[jax.ShapeDtypeStruct((t, d), F32), jax.ShapeDtypeStruct((t, d), BF16)],
        compiler_params=_cparams(1),
        name="outproj",
    )(attn, ml, w_out, x, g_post.reshape(1, d), g_pre.reshape(1, d))


FFN_HALO = 16
FFN_CHUNK = 256


def _ffn_kernel(prev_ref, main_ref, next_ref, x1_ref, wup_ref, wdn_ref, cw_ref, cb_ref,
                gpost_ref, gnext_ref, x2_ref, hn_ref, acc_ref, *, tiles_per_seq):
    i = pl.program_id(0)
    tm = main_ref.shape[0]
    d_ff = wdn_ref.shape[0]
    pos = i % tiles_per_seq
    prev = jnp.where(pos == 0, jnp.zeros_like(prev_ref[...]), prev_ref[...])
    nxt = jnp.where(pos == tiles_per_seq - 1, jnp.zeros_like(next_ref[...]), next_ref[...])
    lhs = jnp.concatenate([prev, main_ref[...], nxt], axis=0)

    def conv(u, c0):
        w = cw_ref[:, c0:c0 + FFN_CHUNK]
        out = u[FFN_HALO:FFN_HALO + tm] * w[1:2] + cb_ref[:, c0:c0 + FFN_CHUNK]
        out = out + u[FFN_HALO - 1:FFN_HALO - 1 + tm] * w[0:1]
        return out + u[FFN_HALO + 1:FFN_HALO + 1 + tm] * w[2:3]

    for c in range(d_ff // FFN_CHUNK):
        c0 = c * FFN_CHUNK
        gate = conv(jnp.dot(lhs, wup_ref[:, c0:c0 + FFN_CHUNK], preferred_element_type=F32), c0)
        val = conv(jnp.dot(lhs, wup_ref[:, d_ff + c0:d_ff + c0 + FFN_CHUNK],
                           preferred_element_type=F32), d_ff + c0)
        inner = np.sqrt(2.0 / np.pi) * (gate + 0.044715 * (gate * gate * gate))
        act = (0.5 * gate * (1.0 + jnp.tanh(inner)) * val).astype(lhs.dtype)
        part = jnp.dot(act, wdn_ref[c0:c0 + FFN_CHUNK, :], preferred_element_type=F32)
        if c == 0:
            acc_ref[...] = part
        else:
            acc_ref[...] += part

    x2 = x1_ref[...] + _rms(acc_ref[...], gpost_ref[...])
    x2_ref[...] = x2
    hn_ref[...] = _rms(x2, gnext_ref[...]).astype(hn_ref.dtype)


def _ffn(hn, x1, w_up, w_down, conv_w, conv_b, g_post, g_next, *, seq, tm=512):
    t, d = x1.shape
    d_ff = w_down.shape[0]
    tiles_per_seq = seq // tm
    halo_per_tile = tm // FFN_HALO
    n_halo_blocks = t // FFN_HALO
    return pl.pallas_call(
        functools.partial(_ffn_kernel, tiles_per_seq=tiles_per_seq),
        grid=(t // tm,),
        in_specs=[pl.BlockSpec((FFN_HALO, d), lambda i: (jnp.maximum(i * halo_per_tile - 1, 0), 0)),
                  pl.BlockSpec((tm, d), lambda i: (i, 0)),
                  pl.BlockSpec((FFN_HALO, d),
                               lambda i: (jnp.minimum((i + 1) * halo_per_tile, n_halo_blocks - 1), 0)),
                  pl.BlockSpec((tm, d), lambda i: (i, 0)),
                  pl.BlockSpec((d, 2 * d_ff), lambda i: (0, 0)),
                  pl.BlockSpec((d_ff, d), lambda i: (0, 0)),
                  pl.BlockSpec(conv_w.shape, lambda i: (0, 0)),
                  pl.BlockSpec((1, 2 * d_ff), lambda i: (0, 0)),
                  pl.BlockSpec((1, d), lambda i: (0, 0)),
                  pl.BlockSpec((1, d), lambda i: (0, 0))],
        out_specs=[pl.BlockSpec((tm, d), lambda i: (i, 0)),
                   pl.BlockSpec((tm, d), lambda i: (i, 0))],
        out_shape=[jax.ShapeDtypeStruct((t, d), F32), jax.ShapeDtypeStruct((t, d), BF16)],
        scratch_shapes=[pltpu.VMEM((tm, d), F32)],
        compiler_params=_cparams(1),
        name="convffn",
    )(hn, hn, hn, x1, w_up, w_down, conv_w, conv_b.reshape(1, -1), g_post.reshape(1, d),
      g_next.reshape(1, d))


def _rotary_tables(seq):
    half = ATTN_HEAD_DIM // 2
    inv_freq = ROPE_THETA ** (-jnp.arange(0, ATTN_HEAD_DIM, 2, dtype=F32) / ATTN_HEAD_DIM)
    ang = jnp.arange(seq, dtype=F32)[:, None] * inv_freq[None, :]
    cos, sin = jnp.cos(ang), jnp.sin(ang)
    reps = LANES // ATTN_HEAD_DIM
    cos_t = jnp.tile(jnp.concatenate([cos, cos], axis=-1), (1, reps))
    sin_t = jnp.tile(jnp.concatenate([-sin, sin], axis=-1), (1, reps))
    return cos_t, sin_t


def kernel(x, mix_pre_g, w_in, mlstm_conv_w, mlstm_conv_b, mlstm_gate_b, mlstm_head_g, w_out,
           mix_post_g, ffn_pre_g, w_up, ffn_conv_w, ffn_conv_b, w_down, ffn_post_g):
    batch, seq, d = x.shape
    depth = w_in.shape[0]
    t = batch * seq
    n_heads = MLSTM_HEADS
    n_gates = mlstm_gate_b.shape[1]
    mix_w = mlstm_head_g.shape[1]
    attn_w = w_out.shape[1] - mix_w
    n_chunks = seq // MLSTM_CHUNK
    assert n_gates == 4 * n_heads and n_gates <= LANES
    assert w_in.shape[2] == 3 * attn_w + 4 * mix_w + n_gates

    cos_t, sin_t = _rotary_tables(seq)
    xf = x.reshape(t, d)
    hn = _prenorm(xf, mix_pre_g[0])
    for l in range(depth):
        w = w_in[l].astype(BF16)
        n_main = 3 * attn_w + 4 * mix_w
        w_gate = jnp.pad(w[:, n_main:], ((0, 0), (0, LANES - n_gates)))
        gate_b = jnp.pad(mlstm_gate_b[l], (0, LANES - n_gates)).reshape(1, LANES)
        qkv, qkv16, mall, gates = _inproj(hn, w[:, :n_main], w_gate, gate_b, cos_t, sin_t,
                                          attn_w=attn_w, mix_w=mix_w, seq=seq)
        attn = _attention(qkv, qkv16, batch=batch, seq=seq)
        gc = _gateprep(gates, n_heads=n_heads)
        a_cols = jnp.concatenate([gc[:, 2 * n_heads:3 * n_heads], gc[:, 5 * n_heads:6 * n_heads]], axis=1)
        gr = a_cols.reshape(batch, seq, 2 * n_heads).transpose(0, 2, 1)
        gr = gr.reshape(batch, 2, n_heads, n_chunks, MLSTM_CHUNK)
        ml = _mlstm(mall, gc, gr, mlstm_conv_w[l], mlstm_conv_b[l].reshape(1, -1),
                    mlstm_head_g[l].reshape(1, -1), batch=batch, seq=seq, n_heads=n_heads)
        x1, hn2 = _outproj(attn, ml, w_out[l].astype(BF16), xf, mix_post_g[l], ffn_pre_g[l])
        g_next = mix_pre_g[l + 1] if l + 1 < depth else mix_pre_g[l]
        xf, hn = _ffn(hn2, x1, w_up[l].astype(BF16), w_down[l].astype(BF16), ffn_conv_w[l],
                      ffn_conv_b[l], ffn_post_g[l], g_next, seq=seq)
    return xf.reshape(batch, seq, d)
```

```python
import functools

import numpy as np
import jax
import jax.numpy as jnp
from jax import lax
from jax.experimental import pallas as pl
from jax.experimental.pallas import tpu as pltpu

F32 = jnp.float32
BF16 = jnp.bfloat16

ATTN_HEAD_DIM = 64
ATTN_HALF_WINDOW = 64
DILATIONS = (1, 4, 16)
MLSTM_HEADS = 4
MLSTM_CHUNK = 128
ROPE_THETA = 10000.0
NORM_EPS = 1e-6
NEG_INF = -1e30

LANES = 128
Q_BLOCK = 128
K_WINDOW = Q_BLOCK + 2 * ATTN_HALF_WINDOW
DIL_MID, DIL_MAX = DILATIONS[1], DILATIONS[2]
DIL_RATIO = DIL_MAX // DIL_MID
ATTN_Q_SCALE = float(ATTN_HEAD_DIM ** -0.5 * np.log2(np.e))
VMEM_LIMIT_BYTES = 56 * 1024 * 1024


def _cparams(n_grid_dims):
    return pltpu.CompilerParams(
        dimension_semantics=("arbitrary",) * n_grid_dims,
        vmem_limit_bytes=VMEM_LIMIT_BYTES)


def _rms(x, g):
    return x * lax.rsqrt(jnp.mean(x * x, axis=-1, keepdims=True) + NORM_EPS) * g


def _prenorm_kernel(x_ref, g_ref, o_ref):
    o_ref[...] = _rms(x_ref[...], g_ref[...]).astype(o_ref.dtype)


def _prenorm(x, g, tm=1024):
    t, d = x.shape
    return pl.pallas_call(
        _prenorm_kernel,
        grid=(t // tm,),
        in_specs=[pl.BlockSpec((tm, d), lambda i: (i, 0)),
                  pl.BlockSpec((1, d), lambda i: (0, 0))],
        out_specs=pl.BlockSpec((tm, d), lambda i: (i, 0)),
        out_shape=jax.ShapeDtypeStruct((t, d), BF16),
        compiler_params=_cparams(1),
        name="prenorm",
    )(x, g.reshape(1, d))


def _inproj_kernel(hn_ref, w_ref, wg_ref, gb_ref, cos_ref, sin_ref,
                   qkv_ref, qkv16_ref, mall_ref, gates_ref, stage_ref, *, attn_w, mix_w):
    hn = hn_ref[...]
    cos = cos_ref[...]
    sin = sin_ref[...]
    tm = hn.shape[0]
    lane = lax.broadcasted_iota(jnp.int32, (tm, LANES), 1)
    first_half = (lane % ATTN_HEAD_DIM) < (ATTN_HEAD_DIM // 2)
    n_pairs = attn_w // LANES
    for grp in range(3):
        res = jnp.dot(hn, w_ref[:, grp * attn_w:(grp + 1) * attn_w],
                      preferred_element_type=F32)
        for hp in range(n_pairs):
            xs = res[:, hp * LANES:(hp + 1) * LANES]
            if grp < 2:
                rot = jnp.where(first_half,
                                pltpu.roll(xs, LANES - ATTN_HEAD_DIM // 2, 1),
                                pltpu.roll(xs, ATTN_HEAD_DIM // 2, 1))
                xs = xs * cos + rot * sin
            if grp == 0:
                xs = xs * ATTN_Q_SCALE
            slab = grp * n_pairs + hp
            qkv_ref[slab] = xs.astype(qkv_ref.dtype)
            stage_ref[slab] = xs
            for r in range(DIL_MAX):
                qkv16_ref[slab, :, r * LANES:(r + 1) * LANES] = (
                    stage_ref[slab, pl.ds(r, tm // DIL_MAX, stride=DIL_MAX), :].astype(qkv16_ref.dtype))
    base = 3 * attn_w
    for grp in range(4):
        res = jnp.dot(hn, w_ref[:, base + grp * mix_w: base + (grp + 1) * mix_w],
                      preferred_element_type=F32)
        mall_ref[:, grp * mix_w:(grp + 1) * mix_w] = res.astype(mall_ref.dtype)
    gates_ref[...] = jnp.dot(hn, wg_ref[...], preferred_element_type=F32) + gb_ref[...]


def _inproj(hn, w_main, w_gate, gate_b, cos, sin, *, attn_w, mix_w, seq, tm=512):
    t, d = hn.shape
    n_pairs = attn_w // LANES
    n_main = w_main.shape[1]
    tiles_per_seq = seq // tm
    kern = functools.partial(_inproj_kernel, attn_w=attn_w, mix_w=mix_w)
    return pl.pallas_call(
        kern,
        grid=(t // tm,),
        in_specs=[pl.BlockSpec((tm, d), lambda i: (i, 0)),
                  pl.BlockSpec((d, n_main), lambda i: (0, 0)),
                  pl.BlockSpec((d, LANES), lambda i: (0, 0)),
                  pl.BlockSpec((1, LANES), lambda i: (0, 0)),
                  pl.BlockSpec((tm, LANES), lambda i: (i % tiles_per_seq, 0)),
                  pl.BlockSpec((tm, LANES), lambda i: (i % tiles_per_seq, 0))],
        out_specs=[pl.BlockSpec((3 * n_pairs, tm, LANES), lambda i: (0, i, 0)),
                   pl.BlockSpec((3 * n_pairs, tm // DIL_MAX, DIL_MAX * LANES), lambda i: (0, i, 0)),
                   pl.BlockSpec((tm, 4 * mix_w), lambda i: (i, 0)),
                   pl.BlockSpec((tm, LANES), lambda i: (i, 0))],
        out_shape=[jax.ShapeDtypeStruct((3 * n_pairs, t, LANES), BF16),
                   jax.ShapeDtypeStruct((3 * n_pairs, t // DIL_MAX, DIL_MAX * LANES), BF16),
                   jax.ShapeDtypeStruct((t, 4 * mix_w), BF16),
                   jax.ShapeDtypeStruct((t, LANES), F32)],
        scratch_shapes=[pltpu.VMEM((3 * n_pairs, tm, LANES), F32)],
        compiler_params=_cparams(1),
        name="inproj",
    )(hn, w_main, w_gate, gate_b, cos, sin)


def _attn_block(q, k, v, bias):
    nk = k.shape[0]

    def head_masks(rows):
        lane = lax.broadcasted_iota(jnp.int32, (rows, LANES), 1)
        h0 = jnp.where(lane < ATTN_HEAD_DIM, 1.0, 0.0).astype(q.dtype)
        return h0, (1.0 - h0.astype(F32)).astype(q.dtype)

    q0, q1 = head_masks(Q_BLOCK)
    v0, v1 = head_masks(nk)
    q_st = jnp.concatenate([q * q0, q * q1], axis=0)
    s = lax.dot_general(q_st, k, (((1,), (1,)), ((), ())), preferred_element_type=F32)
    s = s + jnp.concatenate([bias, bias], axis=0)
    m = jnp.max(s, axis=-1, keepdims=True)
    p = jnp.exp2(s - m).astype(v.dtype)
    p_cat = jnp.concatenate([p[:Q_BLOCK], p[Q_BLOCK:]], axis=1)
    v_ext = jnp.concatenate([jnp.concatenate([v * v0, v0], axis=1),
                             jnp.concatenate([v * v1, v1], axis=1)], axis=0)
    o = jnp.dot(p_cat, v_ext, preferred_element_type=F32)
    lane_o = lax.broadcasted_iota(jnp.int32, (Q_BLOCK, LANES), 1)
    m_lanes = jnp.where(lane_o < ATTN_HEAD_DIM, m[:Q_BLOCK], m[Q_BLOCK:])
    return o[:, :LANES], o[:, LANES:], m_lanes


def _attn_kernel(qn, kn, vn, q16, k16, v16, bias1_ref, bias4_ref, bias16_ref, o_ref,
                 acc_ref, den_ref, max_ref, *, seq):
    n_blocks = seq // Q_BLOCK
    n_mid = (seq // DIL_MID) // Q_BLOCK
    n_max = seq // DIL_MAX
    q_rows = Q_BLOCK // DIL_RATIO
    k_rows = K_WINDOW // DIL_RATIO

    def edge_table(blk, n):
        return jnp.where(blk > 0, 1, 0) + jnp.where(blk == n - 1, 1, 0)

    def store(branch, start, size, stride, vals, rows=slice(None)):
        for ref, val in zip((acc_ref, den_ref, max_ref), vals):
            idx = pl.ds(start, size) if stride == 1 else pl.ds(start, size, stride=stride)
            ref[branch, idx, :] = val[rows]

    def body(i, carry):
        qs = pl.multiple_of(i * Q_BLOCK, Q_BLOCK)
        ks = pl.multiple_of(jnp.clip(qs - ATTN_HALF_WINDOW, 0, seq - K_WINDOW), ATTN_HALF_WINDOW)
        res = _attn_block(qn[pl.ds(qs, Q_BLOCK), :], kn[pl.ds(ks, K_WINDOW), :], vn[pl.ds(ks, K_WINDOW), :],
                          bias1_ref[edge_table(i, n_blocks)])
        store(0, qs, Q_BLOCK, 1, res)

        cls = i // n_mid
        blk = i % n_mid
        r0 = pl.multiple_of(blk * q_rows, q_rows)
        k0 = pl.multiple_of(jnp.clip(r0 - ATTN_HALF_WINDOW // DIL_RATIO, 0, n_max - k_rows),
                            ATTN_HALF_WINDOW // DIL_RATIO)
        lanes = [pl.ds(pl.multiple_of((cls + DIL_MID * m) * LANES, LANES), LANES) for m in range(DIL_RATIO)]
        q = jnp.concatenate([q16[pl.ds(r0, q_rows), ln] for ln in lanes], axis=0)
        k = jnp.concatenate([k16[pl.ds(k0, k_rows), ln] for ln in lanes], axis=0)
        v = jnp.concatenate([v16[pl.ds(k0, k_rows), ln] for ln in lanes], axis=0)
        res = _attn_block(q, k, v, bias4_ref[edge_table(blk, n_mid)])
        for m in range(DIL_RATIO):
            store(1, DIL_MAX * r0 + DIL_MID * m + cls, q_rows, DIL_MAX, res,
                  rows=slice(m * q_rows, (m + 1) * q_rows))

        ln = pl.ds(pl.multiple_of(i * LANES, LANES), LANES)
        res = _attn_block(q16[:, ln], k16[:, ln], v16[:, ln], bias16_ref[...])
        store(2, i, n_max, DIL_MAX, res)
        return carry

    lax.fori_loop(0, n_blocks, body, 0, unroll=4)

    def combine(i, carry):
        rows = pl.ds(pl.multiple_of(i * Q_BLOCK, Q_BLOCK), Q_BLOCK)
        ms = [max_ref[b, rows, :] for b in range(3)]
        mx = jnp.maximum(jnp.maximum(ms[0], ms[1]), ms[2])
        ws = [jnp.exp2(m - mx) for m in ms]
        num = ws[0] * acc_ref[0, rows, :] + ws[1] * acc_ref[1, rows, :] + ws[2] * acc_ref[2, rows, :]
        den = ws[0] * den_ref[0, rows, :] + ws[1] * den_ref[1, rows, :] + ws[2] * den_ref[2, rows, :]
        o_ref[rows, :] = (num / den).astype(o_ref.dtype)
        return carry

    lax.fori_loop(0, n_blocks, combine, 0)


def _attn_bias_tables():
    hw = ATTN_HALF_WINDOW
    row = np.arange(Q_BLOCK)[:, None]
    col = np.arange(K_WINDOW)[None, :]

    def bias(off):
        return np.where(np.abs(off) <= hw, 0.0, NEG_INF).astype(np.float32)

    nat = np.stack([bias(col - shift - row) for shift in (0, hw, 2 * hw)])
    q_rows, k_rows = Q_BLOCK // DIL_RATIO, K_WINDOW // DIL_RATIO
    qpos = DIL_RATIO * (row % q_rows) + row // q_rows
    kpos = DIL_RATIO * (col % k_rows) + col // k_rows
    mid = np.stack([bias(kpos - shift - qpos) for shift in (0, hw, 2 * hw)])
    wide = bias(np.arange(Q_BLOCK)[None, :] - row)
    return nat, mid, wide


def _attention(qkv, qkv16, *, batch, seq):
    n3, t, _ = qkv.shape
    n_pairs = n3 // 3
    n_max = seq // DIL_MAX
    assert DILATIONS[0] == 1 and n_max == Q_BLOCK and seq % (DIL_MID * Q_BLOCK) == 0
    nat, mid, wide = _attn_bias_tables()
    in_specs = ([pl.BlockSpec((None, seq, LANES), lambda b, hp, w=w: (w * n_pairs + hp, b, 0))
                 for w in range(3)]
                + [pl.BlockSpec((None, n_max, DIL_MAX * LANES), lambda b, hp, w=w: (w * n_pairs + hp, b, 0))
                   for w in range(3)]
                + [pl.BlockSpec(nat.shape, lambda b, hp: (0, 0, 0)),
                   pl.BlockSpec(mid.shape, lambda b, hp: (0, 0, 0)),
                   pl.BlockSpec(wide.shape, lambda b, hp: (0, 0))])
    scratch = [pltpu.VMEM((3, seq, LANES), F32) for _ in range(3)]
    return pl.pallas_call(
        functools.partial(_attn_kernel, seq=seq),
        grid=(batch, n_pairs),
        in_specs=in_specs,
        out_specs=pl.BlockSpec((seq, LANES), lambda b, hp: (b, hp)),
        out_shape=jax.ShapeDtypeStruct((t, n_pairs * LANES), BF16),
        scratch_shapes=scratch,
        compiler_params=_cparams(2),
        name="dilated_attn",
    )(qkv, qkv, qkv, qkv16, qkv16, qkv16, nat, mid, wide)


GATE_QUANTITIES = 6


def _gateprep_kernel(g_ref, o_ref, *, n_heads):
    ch = MLSTM_CHUNK
    row = lax.broadcasted_iota(jnp.int32, (ch, ch), 0)
    col = lax.broadcasted_iota(jnp.int32, (ch, ch), 1)
    tri_prefix = (col <= row).astype(F32)
    tri_suffix = (col >= row).astype(F32)
    lane = lax.broadcasted_iota(jnp.int32, (ch, LANES), 1)
    srow = lax.broadcasted_iota(jnp.int32, (ch, LANES), 0)

    def to_lane0(x, src):
        return x if src == 0 else pltpu.roll(x, LANES - src, 1)

    def running_max(a, reverse):
        step = 1
        while step < ch:
            if reverse:
                shifted = jnp.where(srow + step < ch, pltpu.roll(a, ch - step, 0), -jnp.inf)
            else:
                shifted = jnp.where(srow >= step, pltpu.roll(a, step, 0), -jnp.inf)
            a = jnp.maximum(a, shifted)
            step *= 2
        return a

    for c in range(g_ref.shape[0] // ch):
        g = g_ref[c * ch:(c + 1) * ch, :]
        log_f = jnp.minimum(g, 0.0) - jnp.log(1.0 + jnp.exp(-jnp.abs(g)))
        pre = jnp.dot(tri_prefix, log_f, preferred_element_type=F32, precision=lax.Precision.HIGHEST)
        suf = jnp.dot(tri_suffix, log_f, preferred_element_type=F32, precision=lax.Precision.HIGHEST)
        b_f = to_lane0(pre, n_heads)
        a_f = g - b_f
        b_b = to_lane0(suf, 3 * n_heads)
        a_b = to_lane0(g, 2 * n_heads) - b_b
        parts = (b_f, running_max(a_f, False), a_f, b_b, running_max(a_b, True), a_b)
        out = jnp.zeros((ch, LANES), F32)
        for q, part in enumerate(parts):
            placed = part if q == 0 else pltpu.roll(part, q * n_heads, 1)
            out = jnp.where((lane >= q * n_heads) & (lane < (q + 1) * n_heads), placed, out)
        o_ref[c * ch:(c + 1) * ch, :] = out


def _gateprep(gates, *, n_heads, tm=1024):
    t, w = gates.shape
    assert GATE_QUANTITIES * n_heads <= w
    return pl.pallas_call(
        functools.partial(_gateprep_kernel, n_heads=n_heads),
        grid=(t // tm,),
        in_specs=[pl.BlockSpec((tm, w), lambda i: (i, 0))],
        out_specs=pl.BlockSpec((tm, w), lambda i: (i, 0)),
        out_shape=jax.ShapeDtypeStruct((t, w), F32),
        compiler_params=_cparams(1),
        name="gateprep",
    )(gates)


CONV_HALO = 16


def _conv_silu_chunks(x_ref, w_ref, b_ref, stage_ref, emit):
    n = x_ref.shape[0]
    taps = w_ref.shape[0]
    pad = taps // 2
    zeros = jnp.zeros((CONV_HALO, x_ref.shape[1]), F32)
    stage_ref[0:CONV_HALO, :] = zeros
    stage_ref[CONV_HALO + n:, :] = zeros
    stage_ref[CONV_HALO:CONV_HALO + n, :] = x_ref[...].astype(F32)

    def body(c, carry):
        base = pl.multiple_of(c * MLSTM_CHUNK, MLSTM_CHUNK)
        acc = b_ref[...]
        for j in range(taps):
            acc = acc + stage_ref[pl.ds(base + (CONV_HALO + j - pad), MLSTM_CHUNK), :] * w_ref[j:j + 1, :]
        emit(c, acc * jax.nn.sigmoid(acc))
        return carry

    lax.fori_loop(0, n // MLSTM_CHUNK, body, 0, unroll=2)


def _mlstm_kernel(mq_ref, mk_ref, mv_ref, mo_ref, gc_ref, gr_ref, cwq_ref, cwk_ref, cbq_ref, cbk_ref,
                  hg_ref, o_ref, q_s, kt_s, vx_s, gc_s, sc_s, hf_s, hb_s, stage_s, *, n_heads):
    head = pl.program_id(1)
    seq, dh = q_s.shape
    ch = MLSTM_CHUNK
    n_chunks = seq // ch
    def emit_q(c, y):
        q_s[pl.ds(pl.multiple_of(c * ch, ch), ch), :] = y.astype(q_s.dtype)

    def emit_k(c, y):
        kt_s[:, pl.ds(pl.multiple_of(c * ch, ch), ch)] = (y * (dh ** -0.5)).T.astype(kt_s.dtype)

    _conv_silu_chunks(mq_ref, cwq_ref, cbq_ref, stage_s, emit_q)
    _conv_silu_chunks(mk_ref, cwk_ref, cbk_ref, stage_s, emit_k)
    vx_s[:, :dh] = mv_ref[...]
    vx_s[:, dh:] = jnp.ones((seq, dh), vx_s.dtype)
    gc_s[...] = pltpu.roll(gc_ref[...], (LANES - head) % LANES, 1)

    for d in range(2):
        lane_b, lane_cm = (3 * d) * n_heads, (3 * d + 1) * n_heads
        m_run = jnp.zeros((1, 2 * dh), F32)
        for step in range(n_chunks):
            c = step if d == 0 else n_chunks - 1 - step
            last = c * ch + (ch - 1 if d == 0 else 0)
            b_last = jnp.broadcast_to(gc_s[last:last + 1, lane_b:lane_b + 1], (1, 2 * dh))
            a_max = jnp.broadcast_to(gc_s[last:last + 1, lane_cm:lane_cm + 1], (1, 2 * dh))
            w_max = jnp.maximum(m_run, a_max)
            sc_s[d, 0, c:c + 1, :] = m_run
            sc_s[d, 1, c:c + 1, :] = w_max
            sc_s[d, 2, c:c + 1, :] = jnp.exp(m_run - w_max)
            m_run = b_last + w_max

    row = lax.broadcasted_iota(jnp.int32, (ch, ch), 0)
    col = lax.broadcasted_iota(jnp.int32, (ch, ch), 1)

    def direction(c, d, mask, state, h_store):
        lane_b, lane_cm = (3 * d) * n_heads, (3 * d + 1) * n_heads
        rs = pl.multiple_of(c * ch, ch)
        qc = q_s[pl.ds(rs, ch), :]
        ktc = kt_s[:, pl.ds(rs, ch)]
        vx = vx_s[pl.ds(rs, ch), :]
        a_row = gr_ref[d, pl.ds(c, 1), :]
        m_row = sc_s[d, 0, pl.ds(c, 1), :]
        w_max = sc_s[d, 1, pl.ds(c, 1), :]
        decay = sc_s[d, 2, pl.ds(c, 1), :]
        g = gc_s[pl.ds(rs, ch), :]
        b_bc = jnp.broadcast_to(g[:, lane_b:lane_b + 1], (ch, dh))
        cm_bc = jnp.broadcast_to(g[:, lane_cm:lane_cm + 1], (ch, dh))
        mm = jnp.maximum(m_row[:, :dh], cm_bc)
        w = jnp.where(mask, jnp.exp(a_row - mm), 0.0)
        s = jnp.dot(qc, ktc, preferred_element_type=F32)
        intra = jnp.dot((s * w).astype(vx.dtype), vx, preferred_element_type=F32)
        inter = jnp.dot(qc, state.astype(qc.dtype), preferred_element_type=F32)
        g_int = jnp.exp(m_row[:, :dh] - mm)
        num = intra[:, :dh] + g_int * inter[:, :dh]
        den = intra[:, dh:] + g_int * inter[:, dh:]
        h_store[pl.ds(rs, ch), :] = num / jnp.maximum(jnp.abs(den), jnp.exp(-(b_bc + mm)))
        kw_t = (ktc.astype(F32) * jnp.exp(a_row - w_max[:, :ch])).astype(vx.dtype)
        return decay * state + jnp.dot(kw_t, vx, preferred_element_type=F32)

    def step(j, carry):
        fwd, bwd = carry
        fwd = direction(j, 0, col <= row, fwd, hf_s)
        bwd = direction(n_chunks - 1 - j, 1, col >= row, bwd, hb_s)
        return fwd, bwd

    zero = jnp.zeros((dh, 2 * dh), F32)
    lax.fori_loop(0, n_chunks, step, (zero, zero), unroll=4)

    hm = hf_s[...] + hb_s[...]
    hm = hm * lax.rsqrt(jnp.mean(hm * hm, axis=-1, keepdims=True) + NORM_EPS)
    o_ref[...] = (hm * hg_ref[...] * jax.nn.sigmoid(mo_ref[...].astype(F32))).astype(o_ref.dtype)


def _mlstm(mall, gc, gr, conv_w, conv_b, head_g, *, batch, seq, n_heads):
    t = mall.shape[0]
    dh = mall.shape[1] // (4 * n_heads)
    taps = conv_w.shape[0]
    n_chunks = seq // MLSTM_CHUNK
    assert dh == MLSTM_CHUNK == LANES

    def col_block(offset):
        return pl.BlockSpec((seq, dh), lambda b, h, offset=offset: (b, offset * n_heads + h))

    in_specs = [col_block(0), col_block(1), col_block(2), col_block(3),
                pl.BlockSpec((seq, LANES), lambda b, h: (b, 0)),
                pl.BlockSpec((None, 2, None, n_chunks, MLSTM_CHUNK), lambda b, h: (b, 0, h, 0, 0)),
                pl.BlockSpec((taps, dh), lambda b, h: (0, h)),
                pl.BlockSpec((taps, dh), lambda b, h: (0, n_heads + h)),
                pl.BlockSpec((1, dh), lambda b, h: (0, h)),
                pl.BlockSpec((1, dh), lambda b, h: (0, n_heads + h)),
                pl.BlockSpec((1, dh), lambda b, h: (0, h))]
    scratch = [pltpu.VMEM((seq, dh), BF16),
               pltpu.VMEM((dh, seq), BF16),
               pltpu.VMEM((seq, 2 * dh), BF16),
               pltpu.VMEM((seq, LANES), F32),
               pltpu.VMEM((2, 3, n_chunks, 2 * dh), F32),
               pltpu.VMEM((seq, dh), F32), pltpu.VMEM((seq, dh), F32),
               pltpu.VMEM((seq + 2 * CONV_HALO, dh), F32)]
    return pl.pallas_call(
        functools.partial(_mlstm_kernel, n_heads=n_heads),
        grid=(batch, n_heads),
        in_specs=in_specs,
        out_specs=pl.BlockSpec((seq, dh), lambda b, h: (b, h)),
        out_shape=jax.ShapeDtypeStruct((t, n_heads * dh), BF16),
        scratch_shapes=scratch,
        compiler_params=_cparams(2),
        name="mlstm",
    )(mall, mall, mall, mall, gc, gr, conv_w, conv_w, conv_b, conv_b, head_g)


def _outproj_kernel(a_ref, m_ref, w_ref, x_ref, gpost_ref, gpre_ref, x1_ref, hn_ref):
    ka = a_ref.shape[1]
    mixed = (jnp.dot(a_ref[...], w_ref[:ka, :], preferred_element_type=F32)
             + jnp.dot(m_ref[...], w_ref[ka:, :], preferred_element_type=F32))
    x1 = x_ref[...] + _rms(mixed, gpost_ref[...])
    x1_ref[...] = x1
    hn_ref[...] = _rms(x1, gpre_ref[...]).astype(hn_ref.dtype)


def _outproj(attn, ml, w_out, x, g_post, g_pre, tm=512):
    t, d = x.shape
    ka, km = attn.shape[1], ml.shape[1]
    return pl.pallas_call(
        _outproj_kernel,
        grid=(t // tm,),
        in_specs=[pl.BlockSpec((tm, ka), lambda i: (i, 0)),
                  pl.BlockSpec((tm, km), lambda i: (i, 0)),
                  pl.BlockSpec((ka + km, d), lambda i: (0, 0)),
                  pl.BlockSpec((tm, d), lambda i: (i, 0)),
                  pl.BlockSpec((1, d), lambda i: (0, 0)),
                  pl.BlockSpec((1, d), lambda i: (0, 0))],
        out_specs=[pl.BlockSpec((tm, d), lambda i: (i, 0)),
                   pl.BlockSpec((tm, d), lambda i: (i, 0))],
        out_shape=[jax.ShapeDtypeStruct((t, d), F32), jax.ShapeDtypeStruct((t, d), BF16)],
        compiler_params=_cparams(1),
        name="outproj",
    )(attn, ml, w_out, x, g_post.reshape(1, d), g_pre.reshape(1, d))


FFN_HALO = 16
FFN_CHUNK = 256


def _ffn_kernel(prev_ref, main_ref, next_ref, x1_ref, wup_ref, wdn_ref, cw_ref, cb_ref,
                gpost_ref, gnext_ref, x2_ref, *rest, tiles_per_seq, emit_next):
    hn_ref = rest[0] if emit_next else None
    ug_refs, uv_refs, acc_ref = rest[-5:-3], rest[-3:-1], rest[-1]
    i = pl.program_id(0)
    tm = main_ref.shape[0]
    d_ff = wdn_ref.shape[0]
    pos = i % tiles_per_seq
    prev = jnp.where(pos == 0, jnp.zeros_like(prev_ref[...]), prev_ref[...])
    nxt = jnp.where(pos == tiles_per_seq - 1, jnp.zeros_like(next_ref[...]), next_ref[...])
    lhs = jnp.concatenate([prev, main_ref[...], nxt], axis=0)
    taps = cw_ref.shape[0]

    row0 = pl.multiple_of(jnp.minimum(i, 0), FFN_HALO)

    def stage(u_ref, u):
        for k in range(FFN_CHUNK // LANES):
            u_ref[k] = u[:, k * LANES:(k + 1) * LANES]

    def conv(u_ref, k, c0):
        cols = slice(c0 + k * LANES, c0 + (k + 1) * LANES)
        out = cb_ref[:, cols]
        for j in range(taps):
            start = FFN_HALO + j - taps // 2
            out = out + u_ref[k, pl.ds(row0 + start, tm), :] * cw_ref[j:j + 1, cols]
        return out

    gelu_c0 = float(np.sqrt(2.0 / np.pi))
    gelu_c1 = float(np.sqrt(2.0 / np.pi) * 0.044715)
    n_chunks = d_ff // FFN_CHUNK

    def up_project(c):
        c0 = c * FFN_CHUNK
        stage(ug_refs[c % 2], jnp.dot(lhs, wup_ref[:, c0:c0 + FFN_CHUNK], preferred_element_type=F32))
        stage(uv_refs[c % 2], jnp.dot(lhs, wup_ref[:, d_ff + c0:d_ff + c0 + FFN_CHUNK],
                                      preferred_element_type=F32))

    up_project(0)
    for c in range(n_chunks):
        c0 = c * FFN_CHUNK
        slot = c % 2
        if c + 1 < n_chunks:
            up_project(c + 1)
        acts = []
        for k in range(FFN_CHUNK // LANES):
            gate = conv(ug_refs[slot], k, c0)
            val = conv(uv_refs[slot], k, d_ff + c0)
            half = (0.5 * gate) * val
            inner = gate * (gelu_c0 + gelu_c1 * (gate * gate))
            acts.append((half + half * jnp.tanh(inner)).astype(lhs.dtype))
        act = jnp.concatenate(acts, axis=1)
        part = jnp.dot(act, wdn_ref[c0:c0 + FFN_CHUNK, :], preferred_element_type=F32)
        if c == 0:
            acc_ref[...] = part
        else:
            acc_ref[...] += part

    x2 = x1_ref[...] + _rms(acc_ref[...], gpost_ref[...])
    x2_ref[...] = x2
    if emit_next:
        hn_ref[...] = _rms(x2, gnext_ref[...]).astype(hn_ref.dtype)


def _ffn(hn, x1, w_up, w_down, conv_w, conv_b, g_post, g_next, *, seq, emit_next, tm=512):
    t, d = x1.shape
    d_ff = w_down.shape[0]
    tiles_per_seq = seq // tm
    halo_per_tile = tm // FFN_HALO
    n_halo_blocks = t // FFN_HALO
    row_tile = pl.BlockSpec((tm, d), lambda i: (i, 0))
    out_specs = [row_tile, row_tile] if emit_next else [row_tile]
    out_shape = [jax.ShapeDtypeStruct((t, d), F32)] + ([jax.ShapeDtypeStruct((t, d), BF16)] if emit_next else [])
    return pl.pallas_call(
        functools.partial(_ffn_kernel, tiles_per_seq=tiles_per_seq, emit_next=emit_next),
        grid=(t // tm,),
        in_specs=[pl.BlockSpec((FFN_HALO, d), lambda i: (jnp.maximum(i * halo_per_tile - 1, 0), 0)),
                  pl.BlockSpec((tm, d), lambda i: (i, 0)),
                  pl.BlockSpec((FFN_HALO, d),
                               lambda i: (jnp.minimum((i + 1) * halo_per_tile, n_halo_blocks - 1), 0)),
                  pl.BlockSpec((tm, d), lambda i: (i, 0)),
                  pl.BlockSpec((d, 2 * d_ff), lambda i: (0, 0)),
                  pl.BlockSpec((d_ff, d), lambda i: (0, 0)),
                  pl.BlockSpec(conv_w.shape, lambda i: (0, 0)),
                  pl.BlockSpec((1, 2 * d_ff), lambda i: (0, 0)),
                  pl.BlockSpec((1, d), lambda i: (0, 0)),
                  pl.BlockSpec((1, d), lambda i: (0, 0))],
        out_specs=out_specs,
        out_shape=out_shape,
        scratch_shapes=([pltpu.VMEM((FFN_CHUNK // LANES, tm + 2 * FFN_HALO, LANES), F32) for _ in range(4)]
                        + [pltpu.VMEM((tm, d), F32)]),
        compiler_params=_cparams(1),
        name="convffn",
    )(hn, hn, hn, x1, w_up, w_down, conv_w, conv_b.reshape(1, -1), g_post.reshape(1, d),
      g_next.reshape(1, d))


def _rotary_tables(seq):
    half = ATTN_HEAD_DIM // 2
    inv_freq = ROPE_THETA ** (-jnp.arange(0, ATTN_HEAD_DIM, 2, dtype=F32) / ATTN_HEAD_DIM)
    ang = jnp.arange(seq, dtype=F32)[:, None] * inv_freq[None, :]
    cos, sin = jnp.cos(ang), jnp.sin(ang)
    reps = LANES // ATTN_HEAD_DIM
    cos_t = jnp.tile(jnp.concatenate([cos, cos], axis=-1), (1, reps))
    sin_t = jnp.tile(jnp.concatenate([-sin, sin], axis=-1), (1, reps))
    return cos_t, sin_t


def kernel(x, mix_pre_g, w_in, mlstm_conv_w, mlstm_conv_b, mlstm_gate_b, mlstm_head_g, w_out,
           mix_post_g, ffn_pre_g, w_up, ffn_conv_w, ffn_conv_b, w_down, ffn_post_g):
    batch, seq, d = x.shape
    depth = w_in.shape[0]
    t = batch * seq
    n_heads = MLSTM_HEADS
    n_gates = mlstm_gate_b.shape[1]
    mix_w = mlstm_head_g.shape[1]
    attn_w = w_out.shape[1] - mix_w
    n_chunks = seq // MLSTM_CHUNK
    assert n_gates == 4 * n_heads and n_gates <= LANES
    assert w_in.shape[2] == 3 * attn_w + 4 * mix_w + n_gates

    cos_t, sin_t = _rotary_tables(seq)
    xf = x.reshape(t, d)
    hn = _prenorm(xf, mix_pre_g[0])
    for l in range(depth):
        w = w_in[l].astype(BF16)
        n_main = 3 * attn_w + 4 * mix_w
        w_gate = jnp.pad(w[:, n_main:], ((0, 0), (0, LANES - n_gates)))
        gate_b = jnp.pad(mlstm_gate_b[l], (0, LANES - n_gates)).reshape(1, LANES)
        qkv, qkv16, mall, gates = _inproj(hn, w[:, :n_main], w_gate, gate_b, cos_t, sin_t,
                                          attn_w=attn_w, mix_w=mix_w, seq=seq)
        attn = _attention(qkv, qkv16, batch=batch, seq=seq)
        gc = _gateprep(gates, n_heads=n_heads)
        a_cols = jnp.concatenate([gc[:, 2 * n_heads:3 * n_heads], gc[:, 5 * n_heads:6 * n_heads]], axis=1)
        gr = a_cols.reshape(batch, seq, 2 * n_heads).transpose(0, 2, 1)
        gr = gr.reshape(batch, 2, n_heads, n_chunks, MLSTM_CHUNK)
        ml = _mlstm(mall, gc, gr, mlstm_conv_w[l], mlstm_conv_b[l].reshape(1, -1),
                    mlstm_head_g[l].reshape(1, -1), batch=batch, seq=seq, n_heads=n_heads)
        x1, hn2 = _outproj(attn, ml, w_out[l].astype(BF16), xf, mix_post_g[l], ffn_pre_g[l])
        last = l + 1 == depth
        g_next = mix_pre_g[l] if last else mix_pre_g[l + 1]
        outs = _ffn(hn2, x1, w_up[l].astype(BF16), w_down[l].astype(BF16), ffn_conv_w[l],
                    ffn_conv_b[l], ffn_post_g[l], g_next, seq=seq, emit_next=not last)
        xf, hn = (outs[0], None) if last else outs
    return xf.reshape(batch, seq, d)
```

```python
import functools

import numpy as np
import jax
import jax.numpy as jnp
from jax import lax
from jax.experimental import pallas as pl
from jax.experimental.pallas import tpu as pltpu

F32 = jnp.float32
BF16 = jnp.bfloat16

ATTN_HEAD_DIM = 64
ATTN_HALF_WINDOW = 64
DILATIONS = (1, 4, 16)
MLSTM_HEADS = 4
MLSTM_CHUNK = 128
ROPE_THETA = 10000.0
NORM_EPS = 1e-6
NEG_INF = -1e30

LANES = 128
Q_BLOCK = 128
K_WINDOW = Q_BLOCK + 2 * ATTN_HALF_WINDOW
DIL_MID, DIL_MAX = DILATIONS[1], DILATIONS[2]
DIL_RATIO = DIL_MAX // DIL_MID
ATTN_Q_SCALE = float(ATTN_HEAD_DIM ** -0.5 * np.log2(np.e))
VMEM_LIMIT_BYTES = 56 * 1024 * 1024


def _cparams(n_grid_dims):
    return pltpu.CompilerParams(
        dimension_semantics=("arbitrary",) * n_grid_dims,
        vmem_limit_bytes=VMEM_LIMIT_BYTES)


def _rms(x, g):
    return x * lax.rsqrt(jnp.mean(x * x, axis=-1, keepdims=True) + NORM_EPS) * g


def _inproj_kernel(x_ref, g_ref, w_ref, wg_ref, gb_ref, cos_ref, sin_ref,
                   qkv_ref, qkv16_ref, mall_ref, gates_ref, stage_ref, *, attn_w, mix_w):
    hn = _rms(x_ref[...], g_ref[...]).astype(w_ref.dtype)
    cos = cos_ref[...]
    sin = sin_ref[...]
    tm = hn.shape[0]
    lane = lax.broadcasted_iota(jnp.int32, (tm, LANES), 1)
    first_half = (lane % ATTN_HEAD_DIM) < (ATTN_HEAD_DIM // 2)
    n_pairs = attn_w // LANES
    for grp in range(3):
        res = jnp.dot(hn, w_ref[:, grp * attn_w:(grp + 1) * attn_w],
                      preferred_element_type=F32)
        for hp in range(n_pairs):
            xs = res[:, hp * LANES:(hp + 1) * LANES]
            if grp < 2:
                rot = jnp.where(first_half,
                                pltpu.roll(xs, LANES - ATTN_HEAD_DIM // 2, 1),
                                pltpu.roll(xs, ATTN_HEAD_DIM // 2, 1))
                xs = xs * cos + rot * sin
            if grp == 0:
                xs = xs * ATTN_Q_SCALE
            slab = grp * n_pairs + hp
            qkv_ref[slab] = xs.astype(qkv_ref.dtype)
            stage_ref[slab] = xs
            for r in range(DIL_MAX):
                qkv16_ref[slab, :, r * LANES:(r + 1) * LANES] = (
                    stage_ref[slab, pl.ds(r, tm // DIL_MAX, stride=DIL_MAX), :].astype(qkv16_ref.dtype))
    base = 3 * attn_w
    for grp in range(4):
        res = jnp.dot(hn, w_ref[:, base + grp * mix_w: base + (grp + 1) * mix_w],
                      preferred_element_type=F32)
        mall_ref[:, grp * mix_w:(grp + 1) * mix_w] = res.astype(mall_ref.dtype)
    gates_ref[...] = jnp.dot(hn, wg_ref[...], preferred_element_type=F32) + gb_ref[...]


def _inproj(x, g_pre, w_main, w_gate, gate_b, cos, sin, *, attn_w, mix_w, seq, tm=512):
    t, d = x.shape
    n_pairs = attn_w // LANES
    n_main = w_main.shape[1]
    tiles_per_seq = seq // tm
    kern = functools.partial(_inproj_kernel, attn_w=attn_w, mix_w=mix_w)
    return pl.pallas_call(
        kern,
        grid=(t // tm,),
        in_specs=[pl.BlockSpec((tm, d), lambda i: (i, 0)),
                  pl.BlockSpec((1, d), lambda i: (0, 0)),
                  pl.BlockSpec((d, n_main), lambda i: (0, 0)),
                  pl.BlockSpec((d, LANES), lambda i: (0, 0)),
                  pl.BlockSpec((1, LANES), lambda i: (0, 0)),
                  pl.BlockSpec((tm, LANES), lambda i: (i % tiles_per_seq, 0)),
                  pl.BlockSpec((tm, LANES), lambda i: (i % tiles_per_seq, 0))],
        out_specs=[pl.BlockSpec((3 * n_pairs, tm, LANES), lambda i: (0, i, 0)),
                   pl.BlockSpec((3 * n_pairs, tm // DIL_MAX, DIL_MAX * LANES), lambda i: (0, i, 0)),
                   pl.BlockSpec((tm, 4 * mix_w), lambda i: (i, 0)),
                   pl.BlockSpec((tm, LANES), lambda i: (i, 0))],
        out_shape=[jax.ShapeDtypeStruct((3 * n_pairs, t, LANES), BF16),
                   jax.ShapeDtypeStruct((3 * n_pairs, t // DIL_MAX, DIL_MAX * LANES), BF16),
                   jax.ShapeDtypeStruct((t, 4 * mix_w), BF16),
                   jax.ShapeDtypeStruct((t, LANES), F32)],
        scratch_shapes=[pltpu.VMEM((3 * n_pairs, tm, LANES), F32)],
        compiler_params=_cparams(1),
        name="inproj",
    )(x, g_pre.reshape(1, d), w_main, w_gate, gate_b, cos, sin)


def _attn_block(q, k, v, bias):
    nk = k.shape[0]

    def head_masks(rows):
        lane = lax.broadcasted_iota(jnp.int32, (rows, LANES), 1)
        h0 = jnp.where(lane < ATTN_HEAD_DIM, 1.0, 0.0).astype(q.dtype)
        return h0, (1.0 - h0.astype(F32)).astype(q.dtype)

    q0, q1 = head_masks(Q_BLOCK)
    v0, v1 = head_masks(nk)
    q_st = jnp.concatenate([q * q0, q * q1], axis=0)
    s = lax.dot_general(q_st, k, (((1,), (1,)), ((), ())), preferred_element_type=F32)
    s = s + jnp.concatenate([bias, bias], axis=0)
    m = jnp.max(s, axis=-1, keepdims=True)
    p = jnp.exp2(s - m).astype(v.dtype)
    p_cat = jnp.concatenate([p[:Q_BLOCK], p[Q_BLOCK:]], axis=1)
    v_ext = jnp.concatenate([jnp.concatenate([v * v0, v0], axis=1),
                             jnp.concatenate([v * v1, v1], axis=1)], axis=0)
    o = jnp.dot(p_cat, v_ext, preferred_element_type=F32)
    lane_o = lax.broadcasted_iota(jnp.int32, (Q_BLOCK, LANES), 1)
    m_lanes = jnp.where(lane_o < ATTN_HEAD_DIM, m[:Q_BLOCK], m[Q_BLOCK:])
    return o[:, :LANES], o[:, LANES:], m_lanes


def _attn_kernel(qn, kn, vn, q16, k16, v16, bias1_ref, bias4_ref, bias16_ref, o_ref,
                 acc_ref, den_ref, max_ref, *, seq):
    n_blocks = seq // Q_BLOCK
    n_mid = (seq // DIL_MID) // Q_BLOCK
    n_max = seq // DIL_MAX
    q_rows = Q_BLOCK // DIL_RATIO
    k_rows = K_WINDOW // DIL_RATIO

    def edge_table(blk, n):
        return jnp.where(blk > 0, 1, 0) + jnp.where(blk == n - 1, 1, 0)

    def store(branch, start, size, stride, vals, rows=slice(None)):
        for ref, val in zip((acc_ref, den_ref, max_ref), vals):
            idx = pl.ds(start, size) if stride == 1 else pl.ds(start, size, stride=stride)
            ref[branch, idx, :] = val[rows]

    def body(i, carry):
        qs = pl.multiple_of(i * Q_BLOCK, Q_BLOCK)
        ks = pl.multiple_of(jnp.clip(qs - ATTN_HALF_WINDOW, 0, seq - K_WINDOW), ATTN_HALF_WINDOW)
        res = _attn_block(qn[pl.ds(qs, Q_BLOCK), :], kn[pl.ds(ks, K_WINDOW), :], vn[pl.ds(ks, K_WINDOW), :],
                          bias1_ref[edge_table(i, n_blocks)])
        store(0, qs, Q_BLOCK, 1, res)

        cls = i // n_mid
        blk = i % n_mid
        r0 = pl.multiple_of(blk * q_rows, q_rows)
        k0 = pl.multiple_of(jnp.clip(r0 - ATTN_HALF_WINDOW // DIL_RATIO, 0, n_max - k_rows),
                            ATTN_HALF_WINDOW // DIL_RATIO)
        lanes = [pl.ds(pl.multiple_of((cls + DIL_MID * m) * LANES, LANES), LANES) for m in range(DIL_RATIO)]
        q = jnp.concatenate([q16[pl.ds(r0, q_rows), ln] for ln in lanes], axis=0)
        k = jnp.concatenate([k16[pl.ds(k0, k_rows), ln] for ln in lanes], axis=0)
        v = jnp.concatenate([v16[pl.ds(k0, k_rows), ln] for ln in lanes], axis=0)
        res = _attn_block(q, k, v, bias4_ref[edge_table(blk, n_mid)])
        for m in range(DIL_RATIO):
            store(1, DIL_MAX * r0 + DIL_MID * m + cls, q_rows, DIL_MAX, res,
                  rows=slice(m * q_rows, (m + 1) * q_rows))

        ln = pl.ds(pl.multiple_of(i * LANES, LANES), LANES)
        res = _attn_block(q16[:, ln], k16[:, ln], v16[:, ln], bias16_ref[...])
        store(2, i, n_max, DIL_MAX, res)
        return carry

    lax.fori_loop(0, n_blocks, body, 0, unroll=8)

    def combine(i, carry):
        rows = pl.ds(pl.multiple_of(i * Q_BLOCK, Q_BLOCK), Q_BLOCK)
        ms = [max_ref[b, rows, :] for b in range(3)]
        mx = jnp.maximum(jnp.maximum(ms[0], ms[1]), ms[2])
        ws = [jnp.exp2(m - mx) for m in ms]
        num = ws[0] * acc_ref[0, rows, :] + ws[1] * acc_ref[1, rows, :] + ws[2] * acc_ref[2, rows, :]
        den = ws[0] * den_ref[0, rows, :] + ws[1] * den_ref[1, rows, :] + ws[2] * den_ref[2, rows, :]
        o_ref[rows, :] = (num / den).astype(o_ref.dtype)
        return carry

    lax.fori_loop(0, n_blocks, combine, 0)


def _attn_bias_tables():
    hw = ATTN_HALF_WINDOW
    row = np.arange(Q_BLOCK)[:, None]
    col = np.arange(K_WINDOW)[None, :]

    def bias(off):
        return np.where(np.abs(off) <= hw, 0.0, NEG_INF).astype(np.float32)

    nat = np.stack([bias(col - shift - row) for shift in (0, hw, 2 * hw)])
    q_rows, k_rows = Q_BLOCK // DIL_RATIO, K_WINDOW // DIL_RATIO
    qpos = DIL_RATIO * (row % q_rows) + row // q_rows
    kpos = DIL_RATIO * (col % k_rows) + col // k_rows
    mid = np.stack([bias(kpos - shift - qpos) for shift in (0, hw, 2 * hw)])
    wide = bias(np.arange(Q_BLOCK)[None, :] - row)
    return nat, mid, wide


def _attention(qkv, qkv16, *, batch, seq):
    n3, t, _ = qkv.shape
    n_pairs = n3 // 3
    n_max = seq // DIL_MAX
    assert DILATIONS[0] == 1 and n_max == Q_BLOCK and seq % (DIL_MID * Q_BLOCK) == 0
    nat, mid, wide = _attn_bias_tables()
    in_specs = ([pl.BlockSpec((None, seq, LANES), lambda b, hp, w=w: (w * n_pairs + hp, b, 0))
                 for w in range(3)]
                + [pl.BlockSpec((None, n_max, DIL_MAX * LANES), lambda b, hp, w=w: (w * n_pairs + hp, b, 0))
                   for w in range(3)]
                + [pl.BlockSpec(nat.shape, lambda b, hp: (0, 0, 0)),
                   pl.BlockSpec(mid.shape, lambda b, hp: (0, 0, 0)),
                   pl.BlockSpec(wide.shape, lambda b, hp: (0, 0))])
    scratch = [pltpu.VMEM((3, seq, LANES), F32) for _ in range(3)]
    return pl.pallas_call(
        functools.partial(_attn_kernel, seq=seq),
        grid=(batch, n_pairs),
        in_specs=in_specs,
        out_specs=pl.BlockSpec((seq, LANES), lambda b, hp: (b, hp)),
        out_shape=jax.ShapeDtypeStruct((t, n_pairs * LANES), BF16),
        scratch_shapes=scratch,
        compiler_params=_cparams(2),
        name="dilated_attn",
    )(qkv, qkv, qkv, qkv16, qkv16, qkv16, nat, mid, wide)


GATE_QUANTITIES = 6


GATE_ROWS = 32


def _gateprep_kernel(g_ref, o_ref, orow_ref, *, n_heads):
    ch = MLSTM_CHUNK
    row = lax.broadcasted_iota(jnp.int32, (ch, ch), 0)
    col = lax.broadcasted_iota(jnp.int32, (ch, ch), 1)
    tri_prefix = (col <= row).astype(F32)
    tri_suffix = (col >= row).astype(F32)
    lane = lax.broadcasted_iota(jnp.int32, (ch, LANES), 1)
    srow = lax.broadcasted_iota(jnp.int32, (ch, LANES), 0)

    def to_lane0(x, src):
        return x if src == 0 else pltpu.roll(x, LANES - src, 1)

    def running_max(a, reverse):
        step = 1
        while step < ch:
            if reverse:
                shifted = jnp.where(srow + step < ch, pltpu.roll(a, ch - step, 0), -jnp.inf)
            else:
                shifted = jnp.where(srow >= step, pltpu.roll(a, step, 0), -jnp.inf)
            a = jnp.maximum(a, shifted)
            step *= 2
        return a

    for c in range(g_ref.shape[0] // ch):
        g = g_ref[c * ch:(c + 1) * ch, :]
        log_f = jnp.minimum(g, 0.0) - jnp.log(1.0 + jnp.exp(-jnp.abs(g)))
        pre = jnp.dot(tri_prefix, log_f, preferred_element_type=F32, precision=lax.Precision.HIGHEST)
        suf = jnp.dot(tri_suffix, log_f, preferred_element_type=F32, precision=lax.Precision.HIGHEST)
        b_f = to_lane0(pre, n_heads)
        a_f = g - b_f
        b_b = to_lane0(suf, 3 * n_heads)
        a_b = to_lane0(g, 2 * n_heads) - b_b
        parts = (b_f, running_max(a_f, False), a_f, b_b, running_max(a_b, True), a_b)
        out = jnp.zeros((ch, LANES), F32)
        for q, part in enumerate(parts):
            placed = part if q == 0 else pltpu.roll(part, q * n_heads, 1)
            out = jnp.where((lane >= q * n_heads) & (lane < (q + 1) * n_heads), placed, out)
        o_ref[c * ch:(c + 1) * ch, :] = out
        orow_ref[c] = out.T[:GATE_ROWS, :]


def _gateprep(gates, *, n_heads, tm=1024):
    t, w = gates.shape
    assert GATE_QUANTITIES * n_heads <= GATE_ROWS <= w
    cpt = tm // MLSTM_CHUNK
    return pl.pallas_call(
        functools.partial(_gateprep_kernel, n_heads=n_heads),
        grid=(t // tm,),
        in_specs=[pl.BlockSpec((tm, w), lambda i: (i, 0))],
        out_specs=[pl.BlockSpec((tm, w), lambda i: (i, 0)),
                   pl.BlockSpec((cpt, GATE_ROWS, MLSTM_CHUNK), lambda i: (i, 0, 0))],
        out_shape=[jax.ShapeDtypeStruct((t, w), F32),
                   jax.ShapeDtypeStruct((t // MLSTM_CHUNK, GATE_ROWS, MLSTM_CHUNK), F32)],
        compiler_params=_cparams(1),
        name="gateprep",
    )(gates)


CONV_HALO = 16


def _conv_silu_chunks(x_ref, w_ref, b_ref, stage_ref, emit):
    n = x_ref.shape[0]
    taps = w_ref.shape[0]
    pad = taps // 2
    zeros = jnp.zeros((CONV_HALO, x_ref.shape[1]), F32)
    stage_ref[0:CONV_HALO, :] = zeros
    stage_ref[CONV_HALO + n:, :] = zeros
    stage_ref[CONV_HALO:CONV_HALO + n, :] = x_ref[...].astype(F32)

    def body(c, carry):
        base = pl.multiple_of(c * MLSTM_CHUNK, MLSTM_CHUNK)
        acc = b_ref[...]
        for j in range(taps):
            acc = acc + stage_ref[pl.ds(base + (CONV_HALO + j - pad), MLSTM_CHUNK), :] * w_ref[j:j + 1, :]
        emit(c, acc * jax.nn.sigmoid(acc))
        return carry

    lax.fori_loop(0, n // MLSTM_CHUNK, body, 0, unroll=2)


def _mlstm_kernel(mq_ref, mk_ref, mv_ref, mo_ref, gc_ref, gr_ref, cwq_ref, cwk_ref, cbq_ref, cbk_ref,
                  hg_ref, o_ref, q_s, kt_s, vx_s, gc_s, sc_s, hf_s, hb_s, stage_s, *, n_heads):
    head = pl.program_id(1)
    seq, dh = q_s.shape
    ch = MLSTM_CHUNK
    n_chunks = seq // ch
    def emit_q(c, y):
        q_s[pl.ds(pl.multiple_of(c * ch, ch), ch), :] = y.astype(q_s.dtype)

    def emit_k(c, y):
        kt_s[:, pl.ds(pl.multiple_of(c * ch, ch), ch)] = (y * (dh ** -0.5)).T.astype(kt_s.dtype)

    _conv_silu_chunks(mq_ref, cwq_ref, cbq_ref, stage_s, emit_q)
    _conv_silu_chunks(mk_ref, cwk_ref, cbk_ref, stage_s, emit_k)
    vx_s[:, :dh] = mv_ref[...]
    vx_s[:, dh:] = jnp.ones((seq, dh), vx_s.dtype)
    gc_s[...] = pltpu.roll(gc_ref[...], (LANES - head) % LANES, 1)

    for d in range(2):
        lane_b, lane_cm = (3 * d) * n_heads, (3 * d + 1) * n_heads
        m_run = jnp.zeros((1, 2 * dh), F32)
        for step in range(n_chunks):
            c = step if d == 0 else n_chunks - 1 - step
            last = c * ch + (ch - 1 if d == 0 else 0)
            b_last = jnp.broadcast_to(gc_s[last:last + 1, lane_b:lane_b + 1], (1, 2 * dh))
            a_max = jnp.broadcast_to(gc_s[last:last + 1, lane_cm:lane_cm + 1], (1, 2 * dh))
            w_max = jnp.maximum(m_run, a_max)
            sc_s[d, 0, c:c + 1, :] = m_run
            sc_s[d, 1, c:c + 1, :] = w_max
            sc_s[d, 2, c:c + 1, :] = jnp.exp(m_run - w_max)
            m_run = b_last + w_max

    row = lax.broadcasted_iota(jnp.int32, (ch, ch), 0)
    col = lax.broadcasted_iota(jnp.int32, (ch, ch), 1)

    def direction(c, d, mask, state, h_store):
        lane_b, lane_cm = (3 * d) * n_heads, (3 * d + 1) * n_heads
        rs = pl.multiple_of(c * ch, ch)
        qc = q_s[pl.ds(rs, ch), :]
        ktc = kt_s[:, pl.ds(rs, ch)]
        vx = vx_s[pl.ds(rs, ch), :]
        a_row = gr_ref[c, pl.ds((3 * d + 2) * n_heads + head, 1), :]
        m_row = sc_s[d, 0, pl.ds(c, 1), :]
        w_max = sc_s[d, 1, pl.ds(c, 1), :]
        decay = sc_s[d, 2, pl.ds(c, 1), :]
        g = gc_s[pl.ds(rs, ch), :]
        b_bc = jnp.broadcast_to(g[:, lane_b:lane_b + 1], (ch, dh))
        cm_bc = jnp.broadcast_to(g[:, lane_cm:lane_cm + 1], (ch, dh))
        mm = jnp.maximum(m_row[:, :dh], cm_bc)
        w = jnp.where(mask, jnp.exp(a_row - mm), 0.0)
        s = jnp.dot(qc, ktc, preferred_element_type=F32)
        intra = jnp.dot((s * w).astype(vx.dtype), vx, preferred_element_type=F32)
        inter = jnp.dot(qc, state.astype(qc.dtype), preferred_element_type=F32)
        g_int = jnp.exp(m_row[:, :dh] - mm)
        num = intra[:, :dh] + g_int * inter[:, :dh]
        den = intra[:, dh:] + g_int * inter[:, dh:]
        h_store[pl.ds(rs, ch), :] = num / jnp.maximum(jnp.abs(den), jnp.exp(-(b_bc + mm)))
        kw_t = (ktc.astype(F32) * jnp.exp(a_row - w_max[:, :ch])).astype(vx.dtype)
        return decay * state + jnp.dot(kw_t, vx, preferred_element_type=F32)

    def step(j, carry):
        fwd, bwd = carry
        fwd = direction(j, 0, col <= row, fwd, hf_s)
        bwd = direction(n_chunks - 1 - j, 1, col >= row, bwd, hb_s)
        return fwd, bwd

    zero = jnp.zeros((dh, 2 * dh), F32)
    lax.fori_loop(0, n_chunks, step, (zero, zero), unroll=8)

    hm = hf_s[...] + hb_s[...]
    hm = hm * lax.rsqrt(jnp.mean(hm * hm, axis=-1, keepdims=True) + NORM_EPS)
    o_ref[...] = (hm * hg_ref[...] * jax.nn.sigmoid(mo_ref[...].astype(F32))).astype(o_ref.dtype)


def _mlstm(mall, gc, gr, conv_w, conv_b, head_g, *, batch, seq, n_heads):
    t = mall.shape[0]
    dh = mall.shape[1] // (4 * n_heads)
    taps = conv_w.shape[0]
    n_chunks = seq // MLSTM_CHUNK
    assert dh == MLSTM_CHUNK == LANES

    def col_block(offset):
        return pl.BlockSpec((seq, dh), lambda b, h, offset=offset: (b, offset * n_heads + h))

    in_specs = [col_block(0), col_block(1), col_block(2), col_block(3),
                pl.BlockSpec((seq, LANES), lambda b, h: (b, 0)),
                pl.BlockSpec((n_chunks, GATE_ROWS, MLSTM_CHUNK), lambda b, h: (b, 0, 0)),
                pl.BlockSpec((taps, dh), lambda b, h: (0, h)),
                pl.BlockSpec((taps, dh), lambda b, h: (0, n_heads + h)),
                pl.BlockSpec((1, dh), lambda b, h: (0, h)),
                pl.BlockSpec((1, dh), lambda b, h: (0, n_heads + h)),
                pl.BlockSpec((1, dh), lambda b, h: (0, h))]
    scratch = [pltpu.VMEM((seq, dh), BF16),
               pltpu.VMEM((dh, seq), BF16),
               pltpu.VMEM((seq, 2 * dh), BF16),
               pltpu.VMEM((seq, LANES), F32),
               pltpu.VMEM((2, 3, n_chunks, 2 * dh), F32),
               pltpu.VMEM((seq, dh), F32), pltpu.VMEM((seq, dh), F32),
               pltpu.VMEM((seq + 2 * CONV_HALO, dh), F32)]
    return pl.pallas_call(
        functools.partial(_mlstm_kernel, n_heads=n_heads),
        grid=(batch, n_heads),
        in_specs=in_specs,
        out_specs=pl.BlockSpec((seq, dh), lambda b, h: (b, h)),
        out_shape=jax.ShapeDtypeStruct((t, n_heads * dh), BF16),
        scratch_shapes=scratch,
        compiler_params=_cparams(2),
        name="mlstm",
    )(mall, mall, mall, mall, gc, gr, conv_w, conv_w, conv_b, conv_b, head_g)


def _outproj_kernel(a_ref, m_ref, w_ref, x_ref, gpost_ref, gpre_ref, x1_ref, hn_ref):
    ka = a_ref.shape[1]
    mixed = (jnp.dot(a_ref[...], w_ref[:ka, :], preferred_element_type=F32)
             + jnp.dot(m_ref[...], w_ref[ka:, :], preferred_element_type=F32))
    x1 = x_ref[...] + _rms(mixed, gpost_ref[...])
    x1_ref[...] = x1
    hn_ref[...] = _rms(x1, gpre_ref[...]).astype(hn_ref.dtype)


def _outproj(attn, ml, w_out, x, g_post, g_pre, tm=512):
    t, d = x.shape
    ka, km = attn.shape[1], ml.shape[1]
    return pl.pallas_call(
        _outproj_kernel,
        grid=(t // tm,),
        in_specs=[pl.BlockSpec((tm, ka), lambda i: (i, 0)),
                  pl.BlockSpec((tm, km), lambda i: (i, 0)),
                  pl.BlockSpec((ka + km, d), lambda i: (0, 0)),
                  pl.BlockSpec((tm, d), lambda i: (i, 0)),
                  pl.BlockSpec((1, d), lambda i: (0, 0)),
                  pl.BlockSpec((1, d), lambda i: (0, 0))],
        out_specs=[pl.BlockSpec((tm, d), lambda i: (i, 0)),
                   pl.BlockSpec((tm, d), lambda i: (i, 0))],
        out_shape=[jax.ShapeDtypeStruct((t, d), F32), jax.ShapeDtypeStruct((t, d), BF16)],
        compiler_params=_cparams(1),
        name="outproj",
    )(attn, ml, w_out, x, g_post.reshape(1, d), g_pre.reshape(1, d))


FFN_HALO = 16
FFN_CHUNK = 256


def _ffn_kernel(prev_ref, main_ref, next_ref, x1_ref, wup_ref, wdn_ref, cw_ref, cb_ref,
                gpost_ref, x2_ref, ug0_ref, ug1_ref, uv0_ref, uv1_ref, acc_ref, *, tiles_per_seq):
    ug_refs, uv_refs = (ug0_ref, ug1_ref), (uv0_ref, uv1_ref)
    i = pl.program_id(0)
    tm = main_ref.shape[0]
    d_ff = wdn_ref.shape[0]
    pos = i % tiles_per_seq
    prev = jnp.where(pos == 0, jnp.zeros_like(prev_ref[...]), prev_ref[...])
    nxt = jnp.where(pos == tiles_per_seq - 1, jnp.zeros_like(next_ref[...]), next_ref[...])
    lhs = jnp.concatenate([prev, main_ref[...], nxt], axis=0)
    taps = cw_ref.shape[0]

    row0 = pl.multiple_of(jnp.minimum(i, 0), FFN_HALO)

    def stage(u_ref, u):
        for k in range(FFN_CHUNK // LANES):
            u_ref[k] = u[:, k * LANES:(k + 1) * LANES]

    def conv(u_ref, k, c0):
        cols = slice(c0 + k * LANES, c0 + (k + 1) * LANES)
        out = cb_ref[:, cols]
        for j in range(taps):
            start = FFN_HALO + j - taps // 2
            out = out + u_ref[k, pl.ds(row0 + start, tm), :] * cw_ref[j:j + 1, cols]
        return out

    gelu_c0 = float(np.sqrt(2.0 / np.pi))
    gelu_c1 = float(np.sqrt(2.0 / np.pi) * 0.044715)
    n_chunks = d_ff // FFN_CHUNK

    def up_project(c):
        c0 = c * FFN_CHUNK
        stage(ug_refs[c % 2], jnp.dot(lhs, wup_ref[:, c0:c0 + FFN_CHUNK], preferred_element_type=F32))
        stage(uv_refs[c % 2], jnp.dot(lhs, wup_ref[:, d_ff + c0:d_ff + c0 + FFN_CHUNK],
                                      preferred_element_type=F32))

    up_project(0)
    for c in range(n_chunks):
        c0 = c * FFN_CHUNK
        slot = c % 2
        if c + 1 < n_chunks:
            up_project(c + 1)
        acts = []
        for k in range(FFN_CHUNK // LANES):
            gate = conv(ug_refs[slot], k, c0)
            val = conv(uv_refs[slot], k, d_ff + c0)
            half = (0.5 * gate) * val
            inner = gate * (gelu_c0 + gelu_c1 * (gate * gate))
            acts.append((half + half * jnp.tanh(inner)).astype(lhs.dtype))
        act = jnp.concatenate(acts, axis=1)
        part = jnp.dot(act, wdn_ref[c0:c0 + FFN_CHUNK, :], preferred_element_type=F32)
        if c == 0:
            acc_ref[...] = part
        else:
            acc_ref[...] += part

    x2_ref[...] = x1_ref[...] + _rms(acc_ref[...], gpost_ref[...])


def _ffn(hn, x1, w_up, w_down, conv_w, conv_b, g_post, *, seq, tm=512):
    t, d = x1.shape
    d_ff = w_down.shape[0]
    tiles_per_seq = seq // tm
    halo_per_tile = tm // FFN_HALO
    n_halo_blocks = t // FFN_HALO
    return pl.pallas_call(
        functools.partial(_ffn_kernel, tiles_per_seq=tiles_per_seq),
        grid=(t // tm,),
        in_specs=[pl.BlockSpec((FFN_HALO, d), lambda i: (jnp.maximum(i * halo_per_tile - 1, 0), 0)),
                  pl.BlockSpec((tm, d), lambda i: (i, 0)),
                  pl.BlockSpec((FFN_HALO, d),
                               lambda i: (jnp.minimum((i + 1) * halo_per_tile, n_halo_blocks - 1), 0)),
                  pl.BlockSpec((tm, d), lambda i: (i, 0)),
                  pl.BlockSpec((d, 2 * d_ff), lambda i: (0, 0)),
                  pl.BlockSpec((d_ff, d), lambda i: (0, 0)),
                  pl.BlockSpec(conv_w.shape, lambda i: (0, 0)),
                  pl.BlockSpec((1, 2 * d_ff), lambda i: (0, 0)),
                  pl.BlockSpec((1, d), lambda i: (0, 0))],
        out_specs=pl.BlockSpec((tm, d), lambda i: (i, 0)),
        out_shape=jax.ShapeDtypeStruct((t, d), F32),
        scratch_shapes=([pltpu.VMEM((FFN_CHUNK // LANES, tm + 2 * FFN_HALO, LANES), F32) for _ in range(4)]
                        + [pltpu.VMEM((tm, d), F32)]),
        compiler_params=_cparams(1),
        name="convffn",
    )(hn, hn, hn, x1, w_up, w_down, conv_w, conv_b.reshape(1, -1), g_post.reshape(1, d))


def _rotary_tables(seq):
    half = ATTN_HEAD_DIM // 2
    inv_freq = ROPE_THETA ** (-jnp.arange(0, ATTN_HEAD_DIM, 2, dtype=F32) / ATTN_HEAD_DIM)
    ang = jnp.arange(seq, dtype=F32)[:, None] * inv_freq[None, :]
    cos, sin = jnp.cos(ang), jnp.sin(ang)
    reps = LANES // ATTN_HEAD_DIM
    cos_t = jnp.tile(jnp.concatenate([cos, cos], axis=-1), (1, reps))
    sin_t = jnp.tile(jnp.concatenate([-sin, sin], axis=-1), (1, reps))
    return cos_t, sin_t


def kernel(x, mix_pre_g, w_in, mlstm_conv_w, mlstm_conv_b, mlstm_gate_b, mlstm_head_g, w_out,
           mix_post_g, ffn_pre_g, w_up, ffn_conv_w, ffn_conv_b, w_down, ffn_post_g):
    batch, seq, d = x.shape
    depth = w_in.shape[0]
    t = batch * seq
    n_heads = MLSTM_HEADS
    n_gates = mlstm_gate_b.shape[1]
    mix_w = mlstm_head_g.shape[1]
    attn_w = w_out.shape[1] - mix_w
    assert n_gates == 4 * n_heads and n_gates <= LANES
    assert w_in.shape[2] == 3 * attn_w + 4 * mix_w + n_gates

    cos_t, sin_t = _rotary_tables(seq)
    xf = x.reshape(t, d)
    for l in range(depth):
        w = w_in[l].astype(BF16)
        n_main = 3 * attn_w + 4 * mix_w
        w_gate = jnp.pad(w[:, n_main:], ((0, 0), (0, LANES - n_gates)))
        gate_b = jnp.pad(mlstm_gate_b[l], (0, LANES - n_gates)).reshape(1, LANES)
        qkv, qkv16, mall, gates = _inproj(xf, mix_pre_g[l], w[:, :n_main], w_gate, gate_b, cos_t, sin_t,
                                          attn_w=attn_w, mix_w=mix_w, seq=seq)
        attn = _attention(qkv, qkv16, batch=batch, seq=seq)
        gc, gr = _gateprep(gates, n_heads=n_heads)
        ml = _mlstm(mall, gc, gr, mlstm_conv_w[l], mlstm_conv_b[l].reshape(1, -1),
                    mlstm_head_g[l].reshape(1, -1), batch=batch, seq=seq, n_heads=n_heads)
        x1, hn2 = _outproj(attn, ml, w_out[l].astype(BF16), xf, mix_post_g[l], ffn_pre_g[l])
        xf = _ffn(hn2, x1, w_up[l].astype(BF16), w_down[l].astype(BF16), ffn_conv_w[l],
                  ffn_conv_b[l], ffn_post_g[l], seq=seq)
    return xf.reshape(batch, seq, d)
```

```python
import functools

import numpy as np
import jax
import jax.numpy as jnp
from jax import lax
from jax.experimental import pallas as pl
from jax.experimental.pallas import tpu as pltpu

F32 = jnp.float32
BF16 = jnp.bfloat16

ATTN_HEAD_DIM = 64
ATTN_HALF_WINDOW = 64
DILATIONS = (1, 4, 16)
MLSTM_HEADS = 4
MLSTM_CHUNK = 128
ROPE_THETA = 10000.0
NORM_EPS = 1e-6
NEG_INF = -1e30

LANES = 128
Q_BLOCK = 128
K_WINDOW = Q_BLOCK + 2 * ATTN_HALF_WINDOW
DIL_MID, DIL_MAX = DILATIONS[1], DILATIONS[2]
DIL_RATIO = DIL_MAX // DIL_MID
LOG2_E = float(np.log2(np.e))
ATTN_Q_SCALE = float(ATTN_HEAD_DIM ** -0.5) * LOG2_E
VMEM_LIMIT_BYTES = 56 * 1024 * 1024


def _cparams(n_grid_dims):
    return pltpu.CompilerParams(
        dimension_semantics=("arbitrary",) * n_grid_dims,
        vmem_limit_bytes=VMEM_LIMIT_BYTES)


def _rms(x, g):
    return x * lax.rsqrt(jnp.mean(x * x, axis=-1, keepdims=True) + NORM_EPS) * g


def _inproj_kernel(x_ref, g_ref, w_ref, wg_ref, gb_ref, cos_ref, sin_ref,
                   qkv_ref, qkv16_ref, mall_ref, gates_ref, stage_ref, *, attn_w, mix_w):
    hn = _rms(x_ref[...], g_ref[...]).astype(w_ref.dtype)
    cos = cos_ref[...]
    sin = sin_ref[...]
    tm = hn.shape[0]
    lane = lax.broadcasted_iota(jnp.int32, (tm, LANES), 1)
    first_half = (lane % ATTN_HEAD_DIM) < (ATTN_HEAD_DIM // 2)
    n_pairs = attn_w // LANES
    for grp in range(3):
        res = jnp.dot(hn, w_ref[:, grp * attn_w:(grp + 1) * attn_w],
                      preferred_element_type=F32)
        for hp in range(n_pairs):
            xs = res[:, hp * LANES:(hp + 1) * LANES]
            if grp < 2:
                rot = jnp.where(first_half,
                                pltpu.roll(xs, LANES - ATTN_HEAD_DIM // 2, 1),
                                pltpu.roll(xs, ATTN_HEAD_DIM // 2, 1))
                xs = xs * cos + rot * sin
            if grp == 0:
                xs = xs * ATTN_Q_SCALE
            slab = grp * n_pairs + hp
            qkv_ref[slab] = xs.astype(qkv_ref.dtype)
            stage_ref[slab] = xs
            for r in range(DIL_MAX):
                qkv16_ref[slab, :, r * LANES:(r + 1) * LANES] = (
                    stage_ref[slab, pl.ds(r, tm // DIL_MAX, stride=DIL_MAX), :].astype(qkv16_ref.dtype))
    base = 3 * attn_w
    for grp in range(4):
        res = jnp.dot(hn, w_ref[:, base + grp * mix_w: base + (grp + 1) * mix_w],
                      preferred_element_type=F32)
        mall_ref[:, grp * mix_w:(grp + 1) * mix_w] = res.astype(mall_ref.dtype)
    gates_ref[...] = jnp.dot(hn, wg_ref[...], preferred_element_type=F32) + gb_ref[...]


def _inproj(x, g_pre, w_main, w_gate, gate_b, cos, sin, *, attn_w, mix_w, seq, tm=512):
    t, d = x.shape
    n_pairs = attn_w // LANES
    n_main = w_main.shape[1]
    tiles_per_seq = seq // tm
    kern = functools.partial(_inproj_kernel, attn_w=attn_w, mix_w=mix_w)
    return pl.pallas_call(
        kern,
        grid=(t // tm,),
        in_specs=[pl.BlockSpec((tm, d), lambda i: (i, 0)),
                  pl.BlockSpec((1, d), lambda i: (0, 0)),
                  pl.BlockSpec((d, n_main), lambda i: (0, 0)),
                  pl.BlockSpec((d, LANES), lambda i: (0, 0)),
                  pl.BlockSpec((1, LANES), lambda i: (0, 0)),
                  pl.BlockSpec((tm, LANES), lambda i: (i % tiles_per_seq, 0)),
                  pl.BlockSpec((tm, LANES), lambda i: (i % tiles_per_seq, 0))],
        out_specs=[pl.BlockSpec((3 * n_pairs, tm, LANES), lambda i: (0, i, 0)),
                   pl.BlockSpec((3 * n_pairs, tm // DIL_MAX, DIL_MAX * LANES), lambda i: (0, i, 0)),
                   pl.BlockSpec((tm, 4 * mix_w), lambda i: (i, 0)),
                   pl.BlockSpec((tm, LANES), lambda i: (i, 0))],
        out_shape=[jax.ShapeDtypeStruct((3 * n_pairs, t, LANES), BF16),
                   jax.ShapeDtypeStruct((3 * n_pairs, t // DIL_MAX, DIL_MAX * LANES), BF16),
                   jax.ShapeDtypeStruct((t, 4 * mix_w), BF16),
                   jax.ShapeDtypeStruct((t, LANES), F32)],
        scratch_shapes=[pltpu.VMEM((3 * n_pairs, tm, LANES), F32)],
        compiler_params=_cparams(1),
        name="inproj",
    )(x, g_pre.reshape(1, d), w_main, w_gate, gate_b, cos, sin)


def _attn_block(q, k, v, bias):
    nk = k.shape[0]

    def head_masks(rows):
        lane = lax.broadcasted_iota(jnp.int32, (rows, LANES), 1)
        h0 = jnp.where(lane < ATTN_HEAD_DIM, 1.0, 0.0).astype(q.dtype)
        return h0, (1.0 - h0.astype(F32)).astype(q.dtype)

    q0, q1 = head_masks(Q_BLOCK)
    v0, v1 = head_masks(nk)
    q_st = jnp.concatenate([q * q0, q * q1], axis=0)
    s = lax.dot_general(q_st, k, (((1,), (1,)), ((), ())), preferred_element_type=F32)
    s = s + jnp.concatenate([bias, bias], axis=0)
    m = jnp.max(s, axis=-1, keepdims=True)
    p = jnp.exp2(s - m).astype(v.dtype)
    p_cat = jnp.concatenate([p[:Q_BLOCK], p[Q_BLOCK:]], axis=1)
    v_ext = jnp.concatenate([jnp.concatenate([v * v0, v0], axis=1),
                             jnp.concatenate([v * v1, v1], axis=1)], axis=0)
    o = jnp.dot(p_cat, v_ext, preferred_element_type=F32)
    lane_o = lax.broadcasted_iota(jnp.int32, (Q_BLOCK, LANES), 1)
    m_lanes = jnp.where(lane_o < ATTN_HEAD_DIM, m[:Q_BLOCK], m[Q_BLOCK:])
    return o[:, :LANES], o[:, LANES:], m_lanes


def _attn_kernel(qn, kn, vn, q16, k16, v16, bias1_ref, bias4_ref, bias16_ref, o_ref,
                 acc_ref, den_ref, max_ref, *, seq):
    n_blocks = seq // Q_BLOCK
    n_mid = (seq // DIL_MID) // Q_BLOCK
    n_max = seq // DIL_MAX
    q_rows = Q_BLOCK // DIL_RATIO
    k_rows = K_WINDOW // DIL_RATIO

    def edge_table(blk, n):
        return jnp.where(blk > 0, 1, 0) + jnp.where(blk == n - 1, 1, 0)

    def store(branch, start, size, stride, vals, rows=slice(None)):
        for ref, val in zip((acc_ref, den_ref, max_ref), vals):
            idx = pl.ds(start, size) if stride == 1 else pl.ds(start, size, stride=stride)
            ref[branch, idx, :] = val[rows]

    def body(i, carry):
        qs = pl.multiple_of(i * Q_BLOCK, Q_BLOCK)
        ks = pl.multiple_of(jnp.clip(qs - ATTN_HALF_WINDOW, 0, seq - K_WINDOW), ATTN_HALF_WINDOW)
        res = _attn_block(qn[pl.ds(qs, Q_BLOCK), :], kn[pl.ds(ks, K_WINDOW), :], vn[pl.ds(ks, K_WINDOW), :],
                          bias1_ref[edge_table(i, n_blocks)])
        store(0, qs, Q_BLOCK, 1, res)

        cls = i // n_mid
        blk = i % n_mid
        r0 = pl.multiple_of(blk * q_rows, q_rows)
        k0 = pl.multiple_of(jnp.clip(r0 - ATTN_HALF_WINDOW // DIL_RATIO, 0, n_max - k_rows),
                            ATTN_HALF_WINDOW // DIL_RATIO)
        lanes = [pl.ds(pl.multiple_of((cls + DIL_MID * m) * LANES, LANES), LANES) for m in range(DIL_RATIO)]
        q = jnp.concatenate([q16[pl.ds(r0, q_rows), ln] for ln in lanes], axis=0)
        k = jnp.concatenate([k16[pl.ds(k0, k_rows), ln] for ln in lanes], axis=0)
        v = jnp.concatenate([v16[pl.ds(k0, k_rows), ln] for ln in lanes], axis=0)
        res = _attn_block(q, k, v, bias4_ref[edge_table(blk, n_mid)])
        for m in range(DIL_RATIO):
            store(1, DIL_MAX * r0 + DIL_MID * m + cls, q_rows, DIL_MAX, res,
                  rows=slice(m * q_rows, (m + 1) * q_rows))

        ln = pl.ds(pl.multiple_of(i * LANES, LANES), LANES)
        res = _attn_block(q16[:, ln], k16[:, ln], v16[:, ln], bias16_ref[...])
        store(2, i, n_max, DIL_MAX, res)
        return carry

    lax.fori_loop(0, n_blocks, body, 0, unroll=8)

    def combine(i, carry):
        rows = pl.ds(pl.multiple_of(i * Q_BLOCK, Q_BLOCK), Q_BLOCK)
        ms = [max_ref[b, rows, :] for b in range(3)]
        mx = jnp.maximum(jnp.maximum(ms[0], ms[1]), ms[2])
        ws = [jnp.exp2(m - mx) for m in ms]
        num = ws[0] * acc_ref[0, rows, :] + ws[1] * acc_ref[1, rows, :] + ws[2] * acc_ref[2, rows, :]
        den = ws[0] * den_ref[0, rows, :] + ws[1] * den_ref[1, rows, :] + ws[2] * den_ref[2, rows, :]
        o_ref[rows, :] = (num / den).astype(o_ref.dtype)
        return carry

    lax.fori_loop(0, n_blocks, combine, 0)


def _attn_bias_tables():
    hw = ATTN_HALF_WINDOW
    row = np.arange(Q_BLOCK)[:, None]
    col = np.arange(K_WINDOW)[None, :]

    def bias(off):
        return np.where(np.abs(off) <= hw, 0.0, NEG_INF).astype(np.float32)

    nat = np.stack([bias(col - shift - row) for shift in (0, hw, 2 * hw)])
    q_rows, k_rows = Q_BLOCK // DIL_RATIO, K_WINDOW // DIL_RATIO
    qpos = DIL_RATIO * (row % q_rows) + row // q_rows
    kpos = DIL_RATIO * (col % k_rows) + col // k_rows
    mid = np.stack([bias(kpos - shift - qpos) for shift in (0, hw, 2 * hw)])
    wide = bias(np.arange(Q_BLOCK)[None, :] - row)
    return nat, mid, wide


def _attention(qkv, qkv16, *, batch, seq):
    n3, t, _ = qkv.shape
    n_pairs = n3 // 3
    n_max = seq // DIL_MAX
    assert DILATIONS[0] == 1 and n_max == Q_BLOCK and seq % (DIL_MID * Q_BLOCK) == 0
    nat, mid, wide = _attn_bias_tables()
    in_specs = ([pl.BlockSpec((None, seq, LANES), lambda b, hp, w=w: (w * n_pairs + hp, b, 0))
                 for w in range(3)]
                + [pl.BlockSpec((None, n_max, DIL_MAX * LANES), lambda b, hp, w=w: (w * n_pairs + hp, b, 0))
                   for w in range(3)]
                + [pl.BlockSpec(nat.shape, lambda b, hp: (0, 0, 0)),
                   pl.BlockSpec(mid.shape, lambda b, hp: (0, 0, 0)),
                   pl.BlockSpec(wide.shape, lambda b, hp: (0, 0))])
    scratch = [pltpu.VMEM((3, seq, LANES), F32) for _ in range(3)]
    return pl.pallas_call(
        functools.partial(_attn_kernel, seq=seq),
        grid=(batch, n_pairs),
        in_specs=in_specs,
        out_specs=pl.BlockSpec((seq, LANES), lambda b, hp: (b, hp)),
        out_shape=jax.ShapeDtypeStruct((t, n_pairs * LANES), BF16),
        scratch_shapes=scratch,
        compiler_params=_cparams(2),
        name="dilated_attn",
    )(qkv, qkv, qkv, qkv16, qkv16, qkv16, nat, mid, wide)


GATE_GROUP = 16
GATE_ROWS = 3 * GATE_GROUP


def _gateprep_kernel(g_ref, o_ref, orow_ref, *, n_heads):
    ch = MLSTM_CHUNK
    row = lax.broadcasted_iota(jnp.int32, (ch, ch), 0)
    col = lax.broadcasted_iota(jnp.int32, (ch, ch), 1)
    tri_prefix = (col <= row).astype(F32)
    tri_suffix = (col >= row).astype(F32)
    lane = lax.broadcasted_iota(jnp.int32, (ch, LANES), 1)
    srow = lax.broadcasted_iota(jnp.int32, (ch, LANES), 0)
    is_fwd = lane < 2 * n_heads

    for c in range(g_ref.shape[0] // ch):
        g = g_ref[c * ch:(c + 1) * ch, :]
        log_f = jnp.minimum(g, 0.0) - jnp.log(1.0 + jnp.exp(-jnp.abs(g)))
        pre = jnp.dot(tri_prefix, log_f, preferred_element_type=F32, precision=lax.Precision.HIGHEST)
        suf = jnp.dot(tri_suffix, log_f, preferred_element_type=F32, precision=lax.Precision.HIGHEST)
        b = jnp.where(is_fwd, pre, suf)
        a = pltpu.roll(g, n_heads, 1) - b
        cm = a
        step = 1
        while step < ch:
            ahead = jnp.where(srow >= step, pltpu.roll(cm, step, 0), -jnp.inf)
            behind = jnp.where(srow + step < ch, pltpu.roll(cm, ch - step, 0), -jnp.inf)
            cm = jnp.maximum(cm, jnp.where(is_fwd, ahead, behind))
            step *= 2
        out = jnp.where(lane < GATE_GROUP, b,
                        jnp.where(lane < 2 * GATE_GROUP, pltpu.roll(a, GATE_GROUP, 1),
                                  pltpu.roll(cm, 2 * GATE_GROUP, 1)))
        out = out * LOG2_E
        o_ref[c * ch:(c + 1) * ch, :] = out
        orow_ref[c] = out.T[:GATE_ROWS, :]


def _gateprep(gates, *, n_heads, tm=1024):
    t, w = gates.shape
    assert 4 * n_heads == GATE_GROUP and GATE_ROWS <= w
    cpt = tm // MLSTM_CHUNK
    return pl.pallas_call(
        functools.partial(_gateprep_kernel, n_heads=n_heads),
        grid=(t // tm,),
        in_specs=[pl.BlockSpec((tm, w), lambda i: (i, 0))],
        out_specs=[pl.BlockSpec((tm, w), lambda i: (i, 0)),
                   pl.BlockSpec((cpt, GATE_ROWS, MLSTM_CHUNK), lambda i: (i, 0, 0))],
        out_shape=[jax.ShapeDtypeStruct((t, w), F32),
                   jax.ShapeDtypeStruct((t // MLSTM_CHUNK, GATE_ROWS, MLSTM_CHUNK), F32)],
        compiler_params=_cparams(1),
        name="gateprep",
    )(gates)


CONV_HALO = 16


def _conv_silu_chunks(x_ref, w_ref, b_ref, stage_ref, emit):
    n = x_ref.shape[0]
    taps = w_ref.shape[0]
    pad = taps // 2
    zeros = jnp.zeros((CONV_HALO, x_ref.shape[1]), F32)
    stage_ref[0:CONV_HALO, :] = zeros
    stage_ref[CONV_HALO + n:, :] = zeros
    stage_ref[CONV_HALO:CONV_HALO + n, :] = x_ref[...].astype(F32)

    def body(c, carry):
        base = pl.multiple_of(c * MLSTM_CHUNK, MLSTM_CHUNK)
        acc = b_ref[...]
        for j in range(taps):
            acc = acc + stage_ref[pl.ds(base + (CONV_HALO + j - pad), MLSTM_CHUNK), :] * w_ref[j:j + 1, :]
        emit(c, acc * jax.nn.sigmoid(acc))
        return carry

    lax.fori_loop(0, n // MLSTM_CHUNK, body, 0, unroll=2)


def _mlstm_kernel(mq_ref, mk_ref, mv_ref, mo_ref, gc_ref, gr_ref, cwq_ref, cwk_ref, cbq_ref, cbk_ref,
                  hg_ref, o_ref, q_s, kt_s, vx_s, gc_s, sc_s, hf_s, hb_s, stage_s, *, n_heads):
    head = pl.program_id(1)
    seq, dh = q_s.shape
    ch = MLSTM_CHUNK
    n_chunks = seq // ch
    def emit_q(c, y):
        q_s[pl.ds(pl.multiple_of(c * ch, ch), ch), :] = y.astype(q_s.dtype)

    def emit_k(c, y):
        kt_s[:, pl.ds(pl.multiple_of(c * ch, ch), ch)] = (y * (dh ** -0.5)).T.astype(kt_s.dtype)

    _conv_silu_chunks(mq_ref, cwq_ref, cbq_ref, stage_s, emit_q)
    _conv_silu_chunks(mk_ref, cwk_ref, cbk_ref, stage_s, emit_k)
    vx_s[:, :dh] = mv_ref[...]
    vx_s[:, dh:] = jnp.ones((seq, dh), vx_s.dtype)
    gc_s[...] = pltpu.roll(gc_ref[...], (LANES - head) % LANES, 1)

    def gate_lane(group, d):
        return group * GATE_GROUP + (2 * d + 1) * n_heads

    for d in range(2):
        lane_b, lane_cm = gate_lane(0, d), gate_lane(2, d)
        m_run = jnp.zeros((1, 2 * dh), F32)
        for step in range(n_chunks):
            c = step if d == 0 else n_chunks - 1 - step
            last = c * ch + (ch - 1 if d == 0 else 0)
            b_last = jnp.broadcast_to(gc_s[last:last + 1, lane_b:lane_b + 1], (1, 2 * dh))
            a_max = jnp.broadcast_to(gc_s[last:last + 1, lane_cm:lane_cm + 1], (1, 2 * dh))
            w_max = jnp.maximum(m_run, a_max)
            sc_s[d, 0, c:c + 1, :] = m_run
            sc_s[d, 1, c:c + 1, :] = w_max
            sc_s[d, 2, c:c + 1, :] = jnp.exp2(m_run - w_max)
            m_run = b_last + w_max

    row = lax.broadcasted_iota(jnp.int32, (ch, ch), 0)
    col = lax.broadcasted_iota(jnp.int32, (ch, ch), 1)

    def direction(c, d, mask, state, h_store):
        lane_b, lane_cm = gate_lane(0, d), gate_lane(2, d)
        rs = pl.multiple_of(c * ch, ch)
        qc = q_s[pl.ds(rs, ch), :]
        ktc = kt_s[:, pl.ds(rs, ch)]
        vx = vx_s[pl.ds(rs, ch), :]
        a_row = gr_ref[c, pl.ds(gate_lane(1, d) + head, 1), :]
        m_row = sc_s[d, 0, pl.ds(c, 1), :]
        w_max = sc_s[d, 1, pl.ds(c, 1), :]
        decay = sc_s[d, 2, pl.ds(c, 1), :]
        g = gc_s[pl.ds(rs, ch), :]
        b_bc = jnp.broadcast_to(g[:, lane_b:lane_b + 1], (ch, dh))
        cm_bc = jnp.broadcast_to(g[:, lane_cm:lane_cm + 1], (ch, dh))
        mm = jnp.maximum(m_row[:, :dh], cm_bc)
        w = jnp.where(mask, jnp.exp2(a_row - mm), 0.0)
        s = jnp.dot(qc, ktc, preferred_element_type=F32)
        intra = jnp.dot((s * w).astype(vx.dtype), vx, preferred_element_type=F32)
        inter = jnp.dot(qc, state.astype(qc.dtype), preferred_element_type=F32)
        g_int = jnp.exp2(m_row[:, :dh] - mm)
        num = intra[:, :dh] + g_int * inter[:, :dh]
        den = intra[:, dh:] + g_int * inter[:, dh:]
        h_store[pl.ds(rs, ch), :] = num / jnp.maximum(jnp.abs(den), jnp.exp2(-(b_bc + mm)))
        kw_t = (ktc.astype(F32) * jnp.exp2(a_row - w_max[:, :ch])).astype(vx.dtype)
        return decay * state + jnp.dot(kw_t, vx, preferred_element_type=F32)

    def step(j, carry):
        fwd, bwd = carry
        fwd = direction(j, 0, col <= row, fwd, hf_s)
        bwd = direction(n_chunks - 1 - j, 1, col >= row, bwd, hb_s)
        return fwd, bwd

    zero = jnp.zeros((dh, 2 * dh), F32)
    lax.fori_loop(0, n_chunks, step, (zero, zero), unroll=8)

    hm = hf_s[...] + hb_s[...]
    hm = hm * lax.rsqrt(jnp.mean(hm * hm, axis=-1, keepdims=True) + NORM_EPS)
    o_ref[...] = (hm * hg_ref[...] * jax.nn.sigmoid(mo_ref[...].astype(F32))).astype(o_ref.dtype)


def _mlstm(mall, gc, gr, conv_w, conv_b, head_g, *, batch, seq, n_heads):
    t = mall.shape[0]
    dh = mall.shape[1] // (4 * n_heads)
    taps = conv_w.shape[0]
    n_chunks = seq // MLSTM_CHUNK
    assert dh == MLSTM_CHUNK == LANES

    def col_block(offset):
        return pl.BlockSpec((seq, dh), lambda b, h, offset=offset: (b, offset * n_heads + h))

    in_specs = [col_block(0), col_block(1), col_block(2), col_block(3),
                pl.BlockSpec((seq, LANES), lambda b, h: (b, 0)),
                pl.BlockSpec((n_chunks, GATE_ROWS, MLSTM_CHUNK), lambda b, h: (b, 0, 0)),
                pl.BlockSpec((taps, dh), lambda b, h: (0, h)),
                pl.BlockSpec((taps, dh), lambda b, h: (0, n_heads + h)),
                pl.BlockSpec((1, dh), lambda b, h: (0, h)),
                pl.BlockSpec((1, dh), lambda b, h: (0, n_heads + h)),
                pl.BlockSpec((1, dh), lambda b, h: (0, h))]
    scratch = [pltpu.VMEM((seq, dh), BF16),
               pltpu.VMEM((dh, seq), BF16),
               pltpu.VMEM((seq, 2 * dh), BF16),
               pltpu.VMEM((seq, LANES), F32),
               pltpu.VMEM((2, 3, n_chunks, 2 * dh), F32),
               pltpu.VMEM((seq, dh), F32), pltpu.VMEM((seq, dh), F32),
               pltpu.VMEM((seq + 2 * CONV_HALO, dh), F32)]
    return pl.pallas_call(
        functools.partial(_mlstm_kernel, n_heads=n_heads),
        grid=(batch, n_heads),
        in_specs=in_specs,
        out_specs=pl.BlockSpec((seq, dh), lambda b, h: (b, h)),
        out_shape=jax.ShapeDtypeStruct((t, n_heads * dh), BF16),
        scratch_shapes=scratch,
        compiler_params=_cparams(2),
        name="mlstm",
    )(mall, mall, mall, mall, gc, gr, conv_w, conv_w, conv_b, conv_b, head_g)


def _outproj_kernel(a_ref, m_ref, w_ref, x_ref, gpost_ref, gpre_ref, x1_ref, hn_ref):
    ka = a_ref.shape[1]
    mixed = (jnp.dot(a_ref[...], w_ref[:ka, :], preferred_element_type=F32)
             + jnp.dot(m_ref[...], w_ref[ka:, :], preferred_element_type=F32))
    x1 = x_ref[...] + _rms(mixed, gpost_ref[...])
    x1_ref[...] = x1
    hn_ref[...] = _rms(x1, gpre_ref[...]).astype(hn_ref.dtype)


def _outproj(attn, ml, w_out, x, g_post, g_pre, tm=1024):
    t, d = x.shape
    ka, km = attn.shape[1], ml.shape[1]
    return pl.pallas_call(
        _outproj_kernel,
        grid=(t // tm,),
        in_specs=[pl.BlockSpec((tm, ka), lambda i: (i, 0)),
                  pl.BlockSpec((tm, km), lambda i: (i, 0)),
                  pl.BlockSpec((ka + km, d), lambda i: (0, 0)),
                  pl.BlockSpec((tm, d), lambda i: (i, 0)),
                  pl.BlockSpec((1, d), lambda i: (0, 0)),
                  pl.BlockSpec((1, d), lambda i: (0, 0))],
        out_specs=[pl.BlockSpec((tm, d), lambda i: (i, 0)),
                   pl.BlockSpec((tm, d), lambda i: (i, 0))],
        out_shape=[jax.ShapeDtypeStruct((t, d), F32), jax.ShapeDtypeStruct((t, d), BF16)],
        compiler_params=_cparams(1),
        name="outproj",
    )(attn, ml, w_out, x, g_post.reshape(1, d), g_pre.reshape(1, d))


FFN_HALO = 16
FFN_CHUNK = 256


def _ffn_kernel(prev_ref, main_ref, next_ref, x1_ref, wup_ref, wdn_ref, cw_ref, cb_ref,
                gpost_ref, x2_ref, ug0_ref, ug1_ref, uv0_ref, uv1_ref, acc_ref, *, tiles_per_seq):
    ug_refs, uv_refs = (ug0_ref, ug1_ref), (uv0_ref, uv1_ref)
    i = pl.program_id(0)
    tm = main_ref.shape[0]
    d_ff = wdn_ref.shape[0]
    pos = i % tiles_per_seq
    prev = jnp.where(pos == 0, jnp.zeros_like(prev_ref[...]), prev_ref[...])
    nxt = jnp.where(pos == tiles_per_seq - 1, jnp.zeros_like(next_ref[...]), next_ref[...])
    lhs = jnp.concatenate([prev, main_ref[...], nxt], axis=0)
    taps = cw_ref.shape[0]

    row0 = pl.multiple_of(jnp.minimum(i, 0), FFN_HALO)

    def stage(u_ref, u):
        for k in range(FFN_CHUNK // LANES):
            u_ref[k] = u[:, k * LANES:(k + 1) * LANES]

    def conv(u_ref, k, c0):
        cols = slice(c0 + k * LANES, c0 + (k + 1) * LANES)
        out = cb_ref[:, cols]
        for j in range(taps):
            start = FFN_HALO + j - taps // 2
            out = out + u_ref[k, pl.ds(row0 + start, tm), :] * cw_ref[j:j + 1, cols]
        return out

    gelu_c0 = float(np.sqrt(2.0 / np.pi))
    gelu_c1 = float(np.sqrt(2.0 / np.pi) * 0.044715)
    n_chunks = d_ff // FFN_CHUNK

    def up_project(c):
        c0 = c * FFN_CHUNK
        stage(ug_refs[c % 2], jnp.dot(lhs, wup_ref[:, c0:c0 + FFN_CHUNK], preferred_element_type=F32))
        stage(uv_refs[c % 2], jnp.dot(lhs, wup_ref[:, d_ff + c0:d_ff + c0 + FFN_CHUNK],
                                      preferred_element_type=F32))

    up_project(0)
    for c in range(n_chunks):
        c0 = c * FFN_CHUNK
        slot = c % 2
        if c + 1 < n_chunks:
            up_project(c + 1)
        acts = []
        for k in range(FFN_CHUNK // LANES):
            gate = conv(ug_refs[slot], k, c0)
            val = conv(uv_refs[slot], k, d_ff + c0)
            half = (0.5 * gate) * val
            inner = gate * (gelu_c0 + gelu_c1 * (gate * gate))
            acts.append((half + half * jnp.tanh(inner)).astype(lhs.dtype))
        act = jnp.concatenate(acts, axis=1)
        part = jnp.dot(act, wdn_ref[c0:c0 + FFN_CHUNK, :], preferred_element_type=F32)
        if c == 0:
            acc_ref[...] = part
        else:
            acc_ref[...] += part

    x2_ref[...] = x1_ref[...] + _rms(acc_ref[...], gpost_ref[...])


def _ffn(hn, x1, w_up, w_down, conv_w, conv_b, g_post, *, seq, tm=512):
    t, d = x1.shape
    d_ff = w_down.shape[0]
    tiles_per_seq = seq // tm
    halo_per_tile = tm // FFN_HALO
    n_halo_blocks = t // FFN_HALO
    return pl.pallas_call(
        functools.partial(_ffn_kernel, tiles_per_seq=tiles_per_seq),
        grid=(t // tm,),
        in_specs=[pl.BlockSpec((FFN_HALO, d), lambda i: (jnp.maximum(i * halo_per_tile - 1, 0), 0)),
                  pl.BlockSpec((tm, d), lambda i: (i, 0)),
                  pl.BlockSpec((FFN_HALO, d),
                               lambda i: (jnp.minimum((i + 1) * halo_per_tile, n_halo_blocks - 1), 0)),
                  pl.BlockSpec((tm, d), lambda i: (i, 0)),
                  pl.BlockSpec((d, 2 * d_ff), lambda i: (0, 0)),
                  pl.BlockSpec((d_ff, d), lambda i: (0, 0)),
                  pl.BlockSpec(conv_w.shape, lambda i: (0, 0)),
                  pl.BlockSpec((1, 2 * d_ff), lambda i: (0, 0)),
                  pl.BlockSpec((1, d), lambda i: (0, 0))],
        out_specs=pl.BlockSpec((tm, d), lambda i: (i, 0)),
        out_shape=jax.ShapeDtypeStruct((t, d), F32),
        scratch_shapes=([pltpu.VMEM((FFN_CHUNK // LANES, tm + 2 * FFN_HALO, LANES), F32) for _ in range(4)]
                        + [pltpu.VMEM((tm, d), F32)]),
        compiler_params=_cparams(1),
        name="convffn",
    )(hn, hn, hn, x1, w_up, w_down, conv_w, conv_b.reshape(1, -1), g_post.reshape(1, d))


def _rotary_tables(seq):
    half = ATTN_HEAD_DIM // 2
    inv_freq = ROPE_THETA ** (-jnp.arange(0, ATTN_HEAD_DIM, 2, dtype=F32) / ATTN_HEAD_DIM)
    ang = jnp.arange(seq, dtype=F32)[:, None] * inv_freq[None, :]
    cos, sin = jnp.cos(ang), jnp.sin(ang)
    reps = LANES // ATTN_HEAD_DIM
    cos_t = jnp.tile(jnp.concatenate([cos, cos], axis=-1), (1, reps))
    sin_t = jnp.tile(jnp.concatenate([-sin, sin], axis=-1), (1, reps))
    return cos_t, sin_t


def kernel(x, mix_pre_g, w_in, mlstm_conv_w, mlstm_conv_b, mlstm_gate_b, mlstm_head_g, w_out,
           mix_post_g, ffn_pre_g, w_up, ffn_conv_w, ffn_conv_b, w_down, ffn_post_g):
    batch, seq, d = x.shape
    depth = w_in.shape[0]
    t = batch * seq
    n_heads = MLSTM_HEADS
    n_gates = mlstm_gate_b.shape[1]
    mix_w = mlstm_head_g.shape[1]
    attn_w = w_out.shape[1] - mix_w
    assert n_gates == 4 * n_heads and n_gates <= LANES
    assert w_in.shape[2] == 3 * attn_w + 4 * mix_w + n_gates

    cos_t, sin_t = _rotary_tables(seq)
    xf = x.reshape(t, d)
    for l in range(depth):
        w = w_in[l].astype(BF16)
        n_main = 3 * attn_w + 4 * mix_w
        w_gate = jnp.pad(w[:, n_main:], ((0, 0), (0, LANES - n_gates)))
        gate_b = jnp.pad(mlstm_gate_b[l], (0, LANES - n_gates)).reshape(1, LANES)
        qkv, qkv16, mall, gates = _inproj(xf, mix_pre_g[l], w[:, :n_main], w_gate, gate_b, cos_t, sin_t,
                                          attn_w=attn_w, mix_w=mix_w, seq=seq)
        attn = _attention(qkv, qkv16, batch=batch, seq=seq)
        gc, gr = _gateprep(gates, n_heads=n_heads)
        ml = _mlstm(mall, gc, gr, mlstm_conv_w[l], mlstm_conv_b[l].reshape(1, -1),
                    mlstm_head_g[l].reshape(1, -1), batch=batch, seq=seq, n_heads=n_heads)
        x1, hn2 = _outproj(attn, ml, w_out[l].astype(BF16), xf, mix_post_g[l], ffn_pre_g[l])
        xf = _ffn(hn2, x1, w_up[l].astype(BF16), w_down[l].astype(BF16), ffn_conv_w[l],
                  ffn_conv_b[l], ffn_post_g[l], seq=seq)
    return xf.reshape(batch, seq, d)
```

```python
import functools

import numpy as np
import jax
import jax.numpy as jnp
from jax import lax
from jax.experimental import pallas as pl
from jax.experimental.pallas import tpu as pltpu

F32 = jnp.float32
BF16 = jnp.bfloat16

ATTN_HEAD_DIM = 64
ATTN_HALF_WINDOW = 64
DILATIONS = (1, 4, 16)
MLSTM_HEADS = 4
MLSTM_CHUNK = 128
ROPE_THETA = 10000.0
NORM_EPS = 1e-6
NEG_INF = -1e30

LANES = 128
Q_BLOCK = 128
K_WINDOW = Q_BLOCK + 2 * ATTN_HALF_WINDOW
DIL_MID, DIL_MAX = DILATIONS[1], DILATIONS[2]
DIL_RATIO = DIL_MAX // DIL_MID
LOG2_E = float(np.log2(np.e))
ATTN_Q_SCALE = float(ATTN_HEAD_DIM ** -0.5) * LOG2_E
VMEM_LIMIT_BYTES = 56 * 1024 * 1024


def _cparams(n_grid_dims):
    return pltpu.CompilerParams(
        dimension_semantics=("arbitrary",) * n_grid_dims,
        vmem_limit_bytes=VMEM_LIMIT_BYTES)


def _rms(x, g):
    return x * lax.rsqrt(jnp.mean(x * x, axis=-1, keepdims=True) + NORM_EPS) * g


def _inproj_kernel(x_ref, g_ref, w_ref, wg_ref, gb_ref, cos_ref, sin_ref,
                   qkv_ref, qkv16_ref, mall_ref, gates_ref, *stage_refs, attn_w, mix_w):
    hn = _rms(x_ref[...], g_ref[...]).astype(w_ref.dtype)
    cos = cos_ref[...]
    sin = sin_ref[...]
    tm = hn.shape[0]
    lane = lax.broadcasted_iota(jnp.int32, (tm, LANES), 1)
    first_half = (lane % ATTN_HEAD_DIM) < (ATTN_HEAD_DIM // 2)
    n_pairs = attn_w // LANES
    base = 3 * attn_w

    def mix_group(grp):
        res = jnp.dot(hn, w_ref[:, base + grp * mix_w: base + (grp + 1) * mix_w],
                      preferred_element_type=F32)
        mall_ref[:, grp * mix_w:(grp + 1) * mix_w] = res.astype(mall_ref.dtype)

    for grp in range(3):
        res = jnp.dot(hn, w_ref[:, grp * attn_w:(grp + 1) * attn_w],
                      preferred_element_type=F32)
        mix_group(grp)
        for hp in range(n_pairs):
            xs = res[:, hp * LANES:(hp + 1) * LANES]
            if grp < 2:
                rot = jnp.where(first_half,
                                pltpu.roll(xs, LANES - ATTN_HEAD_DIM // 2, 1),
                                pltpu.roll(xs, ATTN_HEAD_DIM // 2, 1))
                xs = xs * cos + rot * sin
            if grp == 0:
                xs = xs * ATTN_Q_SCALE
            slab = grp * n_pairs + hp
            qkv_ref[slab] = xs.astype(qkv_ref.dtype)
            stage_ref = stage_refs[slab]
            stage_ref[...] = xs
            for r in range(DIL_MAX):
                qkv16_ref[slab, :, r * LANES:(r + 1) * LANES] = (
                    stage_ref[pl.ds(r, tm // DIL_MAX, stride=DIL_MAX), :].astype(qkv16_ref.dtype))
    mix_group(3)
    gates_ref[...] = jnp.dot(hn, wg_ref[...], preferred_element_type=F32) + gb_ref[...]


def _inproj(x, g_pre, w_main, w_gate, gate_b, cos, sin, *, attn_w, mix_w, seq, tm=512):
    t, d = x.shape
    n_pairs = attn_w // LANES
    n_main = w_main.shape[1]
    tiles_per_seq = seq // tm
    kern = functools.partial(_inproj_kernel, attn_w=attn_w, mix_w=mix_w)
    return pl.pallas_call(
        kern,
        grid=(t // tm,),
        in_specs=[pl.BlockSpec((tm, d), lambda i: (i, 0)),
                  pl.BlockSpec((1, d), lambda i: (0, 0)),
                  pl.BlockSpec((d, n_main), lambda i: (0, 0)),
                  pl.BlockSpec((d, LANES), lambda i: (0, 0)),
                  pl.BlockSpec((1, LANES), lambda i: (0, 0)),
                  pl.BlockSpec((tm, LANES), lambda i: (i % tiles_per_seq, 0)),
                  pl.BlockSpec((tm, LANES), lambda i: (i % tiles_per_seq, 0))],
        out_specs=[pl.BlockSpec((3 * n_pairs, tm, LANES), lambda i: (0, i, 0)),
                   pl.BlockSpec((3 * n_pairs, tm // DIL_MAX, DIL_MAX * LANES), lambda i: (0, i, 0)),
                   pl.BlockSpec((tm, 4 * mix_w), lambda i: (i, 0)),
                   pl.BlockSpec((tm, LANES), lambda i: (i, 0))],
        out_shape=[jax.ShapeDtypeStruct((3 * n_pairs, t, LANES), BF16),
                   jax.ShapeDtypeStruct((3 * n_pairs, t // DIL_MAX, DIL_MAX * LANES), BF16),
                   jax.ShapeDtypeStruct((t, 4 * mix_w), BF16),
                   jax.ShapeDtypeStruct((t, LANES), F32)],
        scratch_shapes=[pltpu.VMEM((tm, LANES), F32) for _ in range(3 * n_pairs)],
        compiler_params=_cparams(1),
        name="inproj",
    )(x, g_pre.reshape(1, d), w_main, w_gate, gate_b, cos, sin)


def _head_masks(rows, dtype):
    lane = lax.broadcasted_iota(jnp.int32, (rows, LANES), 1)
    h0 = jnp.where(lane < ATTN_HEAD_DIM, 1.0, 0.0).astype(dtype)
    return h0, (1.0 - h0.astype(F32)).astype(dtype)


def _attn_scores(q, k, bias):
    q0, q1 = _head_masks(Q_BLOCK, q.dtype)
    q_st = jnp.concatenate([q * q0, q * q1], axis=0)
    s = lax.dot_general(q_st, k, (((1,), (1,)), ((), ())), preferred_element_type=F32)
    return s + jnp.concatenate([bias, bias], axis=0)


def _attn_finish(s, v):
    v0, v1 = _head_masks(v.shape[0], v.dtype)
    m = jnp.max(s, axis=-1, keepdims=True)
    p = jnp.exp2(s - m).astype(v.dtype)
    p_cat = jnp.concatenate([p[:Q_BLOCK], p[Q_BLOCK:]], axis=1)
    v_ext = jnp.concatenate([jnp.concatenate([v * v0, v0], axis=1),
                             jnp.concatenate([v * v1, v1], axis=1)], axis=0)
    o = jnp.dot(p_cat, v_ext, preferred_element_type=F32)
    lane_o = lax.broadcasted_iota(jnp.int32, (Q_BLOCK, LANES), 1)
    m_lanes = jnp.where(lane_o < ATTN_HEAD_DIM, m[:Q_BLOCK], m[Q_BLOCK:])
    return o[:, :LANES], o[:, LANES:], m_lanes


def _attn_kernel(qn, kn, vn, q16, k16, v16, bias1_ref, bias4_ref, bias16_ref, o_ref,
                 acc_ref, den_ref, max_ref, *, seq):
    n_blocks = seq // Q_BLOCK
    n_mid = (seq // DIL_MID) // Q_BLOCK
    n_max = seq // DIL_MAX
    q_rows = Q_BLOCK // DIL_RATIO
    k_rows = K_WINDOW // DIL_RATIO

    def edge_table(blk, n):
        return jnp.where(blk > 0, 1, 0) + jnp.where(blk == n - 1, 1, 0)

    def nat_window(i):
        qs = pl.multiple_of(i * Q_BLOCK, Q_BLOCK)
        ks = pl.multiple_of(jnp.clip(qs - ATTN_HALF_WINDOW, 0, seq - K_WINDOW), ATTN_HALF_WINDOW)
        return qs, ks

    def mid_window(i):
        cls, blk = i // n_mid, i % n_mid
        r0 = pl.multiple_of(blk * q_rows, q_rows)
        k0 = pl.multiple_of(jnp.clip(r0 - ATTN_HALF_WINDOW // DIL_RATIO, 0, n_max - k_rows),
                            ATTN_HALF_WINDOW // DIL_RATIO)
        lanes = [pl.ds(pl.multiple_of((cls + DIL_MID * m) * LANES, LANES), LANES) for m in range(DIL_RATIO)]
        return cls, blk, r0, k0, lanes

    def gather(ref, r0, rows, lanes):
        return jnp.concatenate([ref[pl.ds(r0, rows), ln] for ln in lanes], axis=0)

    def store(branch, start, size, stride, vals, rows=slice(None)):
        for ref, val in zip((acc_ref, den_ref, max_ref), vals):
            idx = pl.ds(start, size) if stride == 1 else pl.ds(start, size, stride=stride)
            ref[branch, idx, :] = val[rows]

    def body(i, carry):
        qs, ks = nat_window(i)
        s = _attn_scores(qn[pl.ds(qs, Q_BLOCK), :], kn[pl.ds(ks, K_WINDOW), :],
                         bias1_ref[edge_table(i, n_blocks)])
        store(0, qs, Q_BLOCK, 1, _attn_finish(s, vn[pl.ds(ks, K_WINDOW), :]))

        cls, blk, r0, k0, lanes = mid_window(i)
        s = _attn_scores(gather(q16, r0, q_rows, lanes), gather(k16, k0, k_rows, lanes),
                         bias4_ref[edge_table(blk, n_mid)])
        res = _attn_finish(s, gather(v16, k0, k_rows, lanes))
        for m in range(DIL_RATIO):
            store(1, DIL_MAX * r0 + DIL_MID * m + cls, q_rows, DIL_MAX, res,
                  rows=slice(m * q_rows, (m + 1) * q_rows))

        ln = pl.ds(pl.multiple_of(i * LANES, LANES), LANES)
        s = _attn_scores(q16[:, ln], k16[:, ln], bias16_ref[...])
        store(2, i, n_max, DIL_MAX, _attn_finish(s, v16[:, ln]))
        return carry

    lax.fori_loop(0, n_blocks, body, 0, unroll=8)

    def combine(i, carry):
        rows = pl.ds(pl.multiple_of(i * Q_BLOCK, Q_BLOCK), Q_BLOCK)
        ms = [max_ref[b, rows, :] for b in range(3)]
        mx = jnp.maximum(jnp.maximum(ms[0], ms[1]), ms[2])
        ws = [jnp.exp2(m - mx) for m in ms]
        num = ws[0] * acc_ref[0, rows, :] + ws[1] * acc_ref[1, rows, :] + ws[2] * acc_ref[2, rows, :]
        den = ws[0] * den_ref[0, rows, :] + ws[1] * den_ref[1, rows, :] + ws[2] * den_ref[2, rows, :]
        o_ref[rows, :] = (num / den).astype(o_ref.dtype)
        return carry

    lax.fori_loop(0, n_blocks, combine, 0)


def _attn_bias_tables():
    hw = ATTN_HALF_WINDOW
    row = np.arange(Q_BLOCK)[:, None]
    col = np.arange(K_WINDOW)[None, :]

    def bias(off):
        return np.where(np.abs(off) <= hw, 0.0, NEG_INF).astype(np.float32)

    nat = np.stack([bias(col - shift - row) for shift in (0, hw, 2 * hw)])
    q_rows, k_rows = Q_BLOCK // DIL_RATIO, K_WINDOW // DIL_RATIO
    qpos = DIL_RATIO * (row % q_rows) + row // q_rows
    kpos = DIL_RATIO * (col % k_rows) + col // k_rows
    mid = np.stack([bias(kpos - shift - qpos) for shift in (0, hw, 2 * hw)])
    wide = bias(np.arange(Q_BLOCK)[None, :] - row)
    return nat, mid, wide


def _attention(qkv, qkv16, *, batch, seq):
    n3, t, _ = qkv.shape
    n_pairs = n3 // 3
    n_max = seq // DIL_MAX
    assert DILATIONS[0] == 1 and n_max == Q_BLOCK and seq % (DIL_MID * Q_BLOCK) == 0
    nat, mid, wide = _attn_bias_tables()
    in_specs = ([pl.BlockSpec((None, seq, LANES), lambda b, hp, w=w: (w * n_pairs + hp, b, 0))
                 for w in range(3)]
                + [pl.BlockSpec((None, n_max, DIL_MAX * LANES), lambda b, hp, w=w: (w * n_pairs + hp, b, 0))
                   for w in range(3)]
                + [pl.BlockSpec(nat.shape, lambda b, hp: (0, 0, 0)),
                   pl.BlockSpec(mid.shape, lambda b, hp: (0, 0, 0)),
                   pl.BlockSpec(wide.shape, lambda b, hp: (0, 0))])
    scratch = [pltpu.VMEM((3, seq, LANES), F32) for _ in range(3)]
    return pl.pallas_call(
        functools.partial(_attn_kernel, seq=seq),
        grid=(batch, n_pairs),
        in_specs=in_specs,
        out_specs=pl.BlockSpec((seq, LANES), lambda b, hp: (b, hp)),
        out_shape=jax.ShapeDtypeStruct((t, n_pairs * LANES), BF16),
        scratch_shapes=scratch,
        compiler_params=_cparams(2),
        name="dilated_attn",
    )(qkv, qkv, qkv, qkv16, qkv16, qkv16, nat, mid, wide)


GATE_GROUP = 16
GATE_ROWS = 3 * GATE_GROUP


def _gateprep_kernel(g_ref, o_ref, orow_ref, *, n_heads):
    ch = MLSTM_CHUNK
    row = lax.broadcasted_iota(jnp.int32, (ch, ch), 0)
    col = lax.broadcasted_iota(jnp.int32, (ch, ch), 1)
    tri_prefix = (col <= row).astype(F32)
    tri_suffix = (col >= row).astype(F32)
    lane = lax.broadcasted_iota(jnp.int32, (ch, LANES), 1)
    srow = lax.broadcasted_iota(jnp.int32, (ch, LANES), 0)
    is_fwd = lane < 2 * n_heads

    for c in range(g_ref.shape[0] // ch):
        g = g_ref[c * ch:(c + 1) * ch, :]
        log_f = jnp.minimum(g, 0.0) - jnp.log(1.0 + jnp.exp(-jnp.abs(g)))
        pre = jnp.dot(tri_prefix, log_f, preferred_element_type=F32, precision=lax.Precision.HIGHEST)
        suf = jnp.dot(tri_suffix, log_f, preferred_element_type=F32, precision=lax.Precision.HIGHEST)
        b = jnp.where(is_fwd, pre, suf)
        a = pltpu.roll(g, n_heads, 1) - b
        cm = a
        step = 1
        while step < ch:
            ahead = jnp.where(srow >= step, pltpu.roll(cm, step, 0), -jnp.inf)
            behind = jnp.where(srow + step < ch, pltpu.roll(cm, ch - step, 0), -jnp.inf)
            cm = jnp.maximum(cm, jnp.where(is_fwd, ahead, behind))
            step *= 2
        out = jnp.where(lane < GATE_GROUP, b,
                        jnp.where(lane < 2 * GATE_GROUP, pltpu.roll(a, GATE_GROUP, 1),
                                  pltpu.roll(cm, 2 * GATE_GROUP, 1)))
        out = out * LOG2_E
        o_ref[c * ch:(c + 1) * ch, :] = out
        orow_ref[c] = out.T[:GATE_ROWS, :]


def _gateprep(gates, *, n_heads, tm=1024):
    t, w = gates.shape
    assert 4 * n_heads == GATE_GROUP and GATE_ROWS <= w
    cpt = tm // MLSTM_CHUNK
    return pl.pallas_call(
        functools.partial(_gateprep_kernel, n_heads=n_heads),
        grid=(t // tm,),
        in_specs=[pl.BlockSpec((tm, w), lambda i: (i, 0))],
        out_specs=[pl.BlockSpec((tm, w), lambda i: (i, 0)),
                   pl.BlockSpec((cpt, GATE_ROWS, MLSTM_CHUNK), lambda i: (i, 0, 0))],
        out_shape=[jax.ShapeDtypeStruct((t, w), F32),
                   jax.ShapeDtypeStruct((t // MLSTM_CHUNK, GATE_ROWS, MLSTM_CHUNK), F32)],
        compiler_params=_cparams(1),
        name="gateprep",
    )(gates)


CONV_HALO = 16


def _conv_silu_chunks(x_ref, w_ref, b_ref, stage_ref, emit):
    n = x_ref.shape[0]
    taps = w_ref.shape[0]
    pad = taps // 2
    zeros = jnp.zeros((CONV_HALO, x_ref.shape[1]), F32)
    stage_ref[0:CONV_HALO, :] = zeros
    stage_ref[CONV_HALO + n:, :] = zeros
    stage_ref[CONV_HALO:CONV_HALO + n, :] = x_ref[...].astype(F32)

    def body(c, carry):
        base = pl.multiple_of(c * MLSTM_CHUNK, MLSTM_CHUNK)
        acc = b_ref[...]
        for j in range(taps):
            acc = acc + stage_ref[pl.ds(base + (CONV_HALO + j - pad), MLSTM_CHUNK), :] * w_ref[j:j + 1, :]
        emit(c, acc * jax.nn.sigmoid(acc))
        return carry

    lax.fori_loop(0, n // MLSTM_CHUNK, body, 0, unroll=2)


def _mlstm_kernel(mq_ref, mk_ref, mv_ref, mo_ref, gc_ref, gr_ref, cwq_ref, cwk_ref, cbq_ref, cbk_ref,
                  hg_ref, o_ref, q_s, kt_s, vx_s, gc_s, sc_s, hf_s, hb_s, stage_s, *, n_heads):
    head = pl.program_id(1)
    seq, dh = q_s.shape
    ch = MLSTM_CHUNK
    n_chunks = seq // ch
    def emit_q(c, y):
        q_s[pl.ds(pl.multiple_of(c * ch, ch), ch), :] = y.astype(q_s.dtype)

    def emit_k(c, y):
        kt_s[:, pl.ds(pl.multiple_of(c * ch, ch), ch)] = (y * (dh ** -0.5)).T.astype(kt_s.dtype)

    _conv_silu_chunks(mq_ref, cwq_ref, cbq_ref, stage_s, emit_q)
    _conv_silu_chunks(mk_ref, cwk_ref, cbk_ref, stage_s, emit_k)
    vx_s[:, :dh] = mv_ref[...]
    vx_s[:, dh:] = jnp.ones((seq, dh), vx_s.dtype)
    gc_s[...] = pltpu.roll(gc_ref[...], (LANES - head) % LANES, 1)

    def gate_lane(group, d):
        return group * GATE_GROUP + (2 * d + 1) * n_heads

    for d in range(2):
        lane_b, lane_cm = gate_lane(0, d), gate_lane(2, d)
        m_run = jnp.zeros((1, 2 * dh), F32)
        for step in range(n_chunks):
            c = step if d == 0 else n_chunks - 1 - step
            last = c * ch + (ch - 1 if d == 0 else 0)
            b_last = jnp.broadcast_to(gc_s[last:last + 1, lane_b:lane_b + 1], (1, 2 * dh))
            a_max = jnp.broadcast_to(gc_s[last:last + 1, lane_cm:lane_cm + 1], (1, 2 * dh))
            w_max = jnp.maximum(m_run, a_max)
            sc_s[d, 0, c:c + 1, :] = m_run
            sc_s[d, 1, c:c + 1, :] = w_max
            sc_s[d, 2, c:c + 1, :] = jnp.exp2(m_run - w_max)
            m_run = b_last + w_max

    row = lax.broadcasted_iota(jnp.int32, (ch, ch), 0)
    col = lax.broadcasted_iota(jnp.int32, (ch, ch), 1)

    def direction(c, d, mask, state, h_store):
        lane_b, lane_cm = gate_lane(0, d), gate_lane(2, d)
        rs = pl.multiple_of(c * ch, ch)
        qc = q_s[pl.ds(rs, ch), :]
        ktc = kt_s[:, pl.ds(rs, ch)]
        vx = vx_s[pl.ds(rs, ch), :]
        a_row = gr_ref[c, pl.ds(gate_lane(1, d) + head, 1), :]
        m_row = sc_s[d, 0, pl.ds(c, 1), :]
        w_max = sc_s[d, 1, pl.ds(c, 1), :]
        decay = sc_s[d, 2, pl.ds(c, 1), :]
        g = gc_s[pl.ds(rs, ch), :]
        b_bc = jnp.broadcast_to(g[:, lane_b:lane_b + 1], (ch, dh))
        cm_bc = jnp.broadcast_to(g[:, lane_cm:lane_cm + 1], (ch, dh))
        mm = jnp.maximum(m_row[:, :dh], cm_bc)
        w = jnp.where(mask, jnp.exp2(a_row - mm), 0.0)
        s = jnp.dot(qc, ktc, preferred_element_type=F32)
        intra = jnp.dot((s * w).astype(vx.dtype), vx, preferred_element_type=F32)
        inter = jnp.dot(qc, state.astype(qc.dtype), preferred_element_type=F32)
        g_int = jnp.exp2(m_row[:, :dh] - mm)
        num = intra[:, :dh] + g_int * inter[:, :dh]
        den = intra[:, dh:] + g_int * inter[:, dh:]
        h_store[pl.ds(rs, ch), :] = num / jnp.maximum(jnp.abs(den), jnp.exp2(-(b_bc + mm)))
        kw_t = (ktc.astype(F32) * jnp.exp2(a_row - w_max[:, :ch])).astype(vx.dtype)
        return decay * state + jnp.dot(kw_t, vx, preferred_element_type=F32)

    def step(j, carry):
        fwd, bwd = carry
        fwd = direction(j, 0, col <= row, fwd, hf_s)
        bwd = direction(n_chunks - 1 - j, 1, col >= row, bwd, hb_s)
        return fwd, bwd

    zero = jnp.zeros((dh, 2 * dh), F32)
    lax.fori_loop(0, n_chunks, step, (zero, zero), unroll=8)

    hm = hf_s[...] + hb_s[...]
    hm = hm * lax.rsqrt(jnp.mean(hm * hm, axis=-1, keepdims=True) + NORM_EPS)
    o_ref[...] = (hm * hg_ref[...] * jax.nn.sigmoid(mo_ref[...].astype(F32))).astype(o_ref.dtype)


def _mlstm(mall, gc, gr, conv_w, conv_b, head_g, *, batch, seq, n_heads):
    t = mall.shape[0]
    dh = mall.shape[1] // (4 * n_heads)
    taps = conv_w.shape[0]
    n_chunks = seq // MLSTM_CHUNK
    assert dh == MLSTM_CHUNK == LANES

    def col_block(offset):
        return pl.BlockSpec((seq, dh), lambda b, h, offset=offset: (b, offset * n_heads + h))

    in_specs = [col_block(0), col_block(1), col_block(2), col_block(3),
                pl.BlockSpec((seq, LANES), lambda b, h: (b, 0)),
                pl.BlockSpec((n_chunks, GATE_ROWS, MLSTM_CHUNK), lambda b, h: (b, 0, 0)),
                pl.BlockSpec((taps, dh), lambda b, h: (0, h)),
                pl.BlockSpec((taps, dh), lambda b, h: (0, n_heads + h)),
                pl.BlockSpec((1, dh), lambda b, h: (0, h)),
                pl.BlockSpec((1, dh), lambda b, h: (0, n_heads + h)),
                pl.BlockSpec((1, dh), lambda b, h: (0, h))]
    scratch = [pltpu.VMEM((seq, dh), BF16),
               pltpu.VMEM((dh, seq), BF16),
               pltpu.VMEM((seq, 2 * dh), BF16),
               pltpu.VMEM((seq, LANES), F32),
               pltpu.VMEM((2, 3, n_chunks, 2 * dh), F32),
               pltpu.VMEM((seq, dh), F32), pltpu.VMEM((seq, dh), F32),
               pltpu.VMEM((seq + 2 * CONV_HALO, dh), F32)]
    return pl.pallas_call(
        functools.partial(_mlstm_kernel, n_heads=n_heads),
        grid=(batch, n_heads),
        in_specs=in_specs,
        out_specs=pl.BlockSpec((seq, dh), lambda b, h: (b, h)),
        out_shape=jax.ShapeDtypeStruct((t, n_heads * dh), BF16),
        scratch_shapes=scratch,
        compiler_params=_cparams(2),
        name="mlstm",
    )(mall, mall, mall, mall, gc, gr, conv_w, conv_w, conv_b, conv_b, head_g)


def _outproj_kernel(a_ref, m_ref, w_ref, x_ref, gpost_ref, gpre_ref, x1_ref, hn_ref):
    ka = a_ref.shape[1]
    mixed = (jnp.dot(a_ref[...], w_ref[:ka, :], preferred_element_type=F32)
             + jnp.dot(m_ref[...], w_ref[ka:, :], preferred_element_type=F32))
    x1 = x_ref[...] + _rms(mixed, gpost_ref[...])
    x1_ref[...] = x1
    hn_ref[...] = _rms(x1, gpre_ref[...]).astype(hn_ref.dtype)


def _outproj(attn, ml, w_out, x, g_post, g_pre, tm=1024):
    t, d = x.shape
    ka, km = attn.shape[1], ml.shape[1]
    return pl.pallas_call(
        _outproj_kernel,
        grid=(t // tm,),
        in_specs=[pl.BlockSpec((tm, ka), lambda i: (i, 0)),
                  pl.BlockSpec((tm, km), lambda i: (i, 0)),
                  pl.BlockSpec((ka + km, d), lambda i: (0, 0)),
                  pl.BlockSpec((tm, d), lambda i: (i, 0)),
                  pl.BlockSpec((1, d), lambda i: (0, 0)),
                  pl.BlockSpec((1, d), lambda i: (0, 0))],
        out_specs=[pl.BlockSpec((tm, d), lambda i: (i, 0)),
                   pl.BlockSpec((tm, d), lambda i: (i, 0))],
        out_shape=[jax.ShapeDtypeStruct((t, d), F32), jax.ShapeDtypeStruct((t, d), BF16)],
        compiler_params=_cparams(1),
        name="outproj",
    )(attn, ml, w_out, x, g_post.reshape(1, d), g_pre.reshape(1, d))


FFN_HALO = 16
FFN_CHUNK = 256
FFN_STAGES = 4
FFN_SUBTILES = 1


def _ffn_kernel(prev_ref, main_ref, next_ref, x1_ref, wup_ref, wdn_ref, cw_ref, cb_ref,
                gpost_ref, x2_ref, *scratch, blocks_per_seq):
    ug_refs = scratch[:FFN_STAGES]
    uv_refs = scratch[FFN_STAGES:2 * FFN_STAGES]
    acc_refs = scratch[2 * FFN_STAGES:]
    i = pl.program_id(0)
    tm = main_ref.shape[0] // FFN_SUBTILES
    d_ff = wdn_ref.shape[0]
    pos = i % blocks_per_seq
    prev = jnp.where(pos == 0, jnp.zeros_like(prev_ref[...]), prev_ref[...])
    nxt = jnp.where(pos == blocks_per_seq - 1, jnp.zeros_like(next_ref[...]), next_ref[...])
    lhs = []
    for s in range(FFN_SUBTILES):
        before = prev if s == 0 else main_ref[s * tm - FFN_HALO:s * tm, :]
        after = nxt if s == FFN_SUBTILES - 1 else main_ref[(s + 1) * tm:(s + 1) * tm + FFN_HALO, :]
        lhs.append(jnp.concatenate([before, main_ref[s * tm:(s + 1) * tm, :], after], axis=0))
    taps = cw_ref.shape[0]

    row0 = pl.multiple_of(jnp.minimum(i, 0), FFN_HALO)

    def stage(u_ref, u):
        for k in range(FFN_CHUNK // LANES):
            u_ref[k] = u[:, k * LANES:(k + 1) * LANES]

    def conv(u_ref, k, c0):
        cols = slice(c0 + k * LANES, c0 + (k + 1) * LANES)
        out = cb_ref[:, cols]
        for j in range(taps):
            start = FFN_HALO + j - taps // 2
            out = out + u_ref[k, pl.ds(row0 + start, tm), :] * cw_ref[j:j + 1, cols]
        return out

    gelu_c0 = float(np.sqrt(2.0 / np.pi))
    gelu_c1 = float(np.sqrt(2.0 / np.pi) * 0.044715)
    n_chunks = d_ff // FFN_CHUNK
    n_total = FFN_SUBTILES * n_chunks

    def up_project(g):
        s, c0 = g // n_chunks, (g % n_chunks) * FFN_CHUNK
        stage(ug_refs[g % FFN_STAGES], jnp.dot(lhs[s], wup_ref[:, c0:c0 + FFN_CHUNK],
                                               preferred_element_type=F32))
        stage(uv_refs[g % FFN_STAGES], jnp.dot(lhs[s], wup_ref[:, d_ff + c0:d_ff + c0 + FFN_CHUNK],
                                               preferred_element_type=F32))

    for g in range(FFN_STAGES - 1):
        up_project(g)
    for g in range(n_total):
        s, c = g // n_chunks, g % n_chunks
        c0 = c * FFN_CHUNK
        slot = g % FFN_STAGES
        if g + FFN_STAGES - 1 < n_total:
            up_project(g + FFN_STAGES - 1)
        acts = []
        for k in range(FFN_CHUNK // LANES):
            gate = conv(ug_refs[slot], k, c0)
            val = conv(uv_refs[slot], k, d_ff + c0)
            half = (0.5 * gate) * val
            inner = gate * (gelu_c0 + gelu_c1 * (gate * gate))
            acts.append((half + half * jnp.tanh(inner)).astype(lhs[s].dtype))
        act = jnp.concatenate(acts, axis=1)
        part = jnp.dot(act, wdn_ref[c0:c0 + FFN_CHUNK, :], preferred_element_type=F32)
        if c == 0:
            acc_refs[s][...] = part
        else:
            acc_refs[s][...] += part
        if c == n_chunks - 1:
            rows = slice(s * tm, (s + 1) * tm)
            x2_ref[rows, :] = x1_ref[rows, :] + _rms(acc_refs[s][...], gpost_ref[...])


def _ffn(hn, x1, w_up, w_down, conv_w, conv_b, g_post, *, seq, tm=512):
    t, d = x1.shape
    d_ff = w_down.shape[0]
    rows = FFN_SUBTILES * tm
    blocks_per_seq = seq // rows
    halo_per_block = rows // FFN_HALO
    n_halo_blocks = t // FFN_HALO
    return pl.pallas_call(
        functools.partial(_ffn_kernel, blocks_per_seq=blocks_per_seq),
        grid=(t // rows,),
        in_specs=[pl.BlockSpec((FFN_HALO, d), lambda i: (jnp.maximum(i * halo_per_block - 1, 0), 0)),
                  pl.BlockSpec((rows, d), lambda i: (i, 0)),
                  pl.BlockSpec((FFN_HALO, d),
                               lambda i: (jnp.minimum((i + 1) * halo_per_block, n_halo_blocks - 1), 0)),
                  pl.BlockSpec((rows, d), lambda i: (i, 0)),
                  pl.BlockSpec((d, 2 * d_ff), lambda i: (0, 0), pipeline_mode=pl.Buffered(1)),
                  pl.BlockSpec((d_ff, d), lambda i: (0, 0), pipeline_mode=pl.Buffered(1)),
                  pl.BlockSpec(conv_w.shape, lambda i: (0, 0)),
                  pl.BlockSpec((1, 2 * d_ff), lambda i: (0, 0)),
                  pl.BlockSpec((1, d), lambda i: (0, 0))],
        out_specs=pl.BlockSpec((rows, d), lambda i: (i, 0)),
        out_shape=jax.ShapeDtypeStruct((t, d), F32),
        scratch_shapes=([pltpu.VMEM((FFN_CHUNK // LANES, tm + 2 * FFN_HALO, LANES), F32)
                         for _ in range(2 * FFN_STAGES)]
                        + [pltpu.VMEM((tm, d), F32) for _ in range(FFN_SUBTILES)]),
        compiler_params=_cparams(1),
        name="convffn",
    )(hn, hn, hn, x1, w_up, w_down, conv_w, conv_b.reshape(1, -1), g_post.reshape(1, d))


def _rotary_tables(seq):
    half = ATTN_HEAD_DIM // 2
    inv_freq = ROPE_THETA ** (-jnp.arange(0, ATTN_HEAD_DIM, 2, dtype=F32) / ATTN_HEAD_DIM)
    ang = jnp.arange(seq, dtype=F32)[:, None] * inv_freq[None, :]
    cos, sin = jnp.cos(ang), jnp.sin(ang)
    reps = LANES // ATTN_HEAD_DIM
    cos_t = jnp.tile(jnp.concatenate([cos, cos], axis=-1), (1, reps))
    sin_t = jnp.tile(jnp.concatenate([-sin, sin], axis=-1), (1, reps))
    return cos_t, sin_t


def kernel(x, mix_pre_g, w_in, mlstm_conv_w, mlstm_conv_b, mlstm_gate_b, mlstm_head_g, w_out,
           mix_post_g, ffn_pre_g, w_up, ffn_conv_w, ffn_conv_b, w_down, ffn_post_g):
    batch, seq, d = x.shape
    depth = w_in.shape[0]
    t = batch * seq
    n_heads = MLSTM_HEADS
    n_gates = mlstm_gate_b.shape[1]
    mix_w = mlstm_head_g.shape[1]
    attn_w = w_out.shape[1] - mix_w
    assert n_gates == 4 * n_heads and n_gates <= LANES
    assert w_in.shape[2] == 3 * attn_w + 4 * mix_w + n_gates

    cos_t, sin_t = _rotary_tables(seq)
    xf = x.reshape(t, d)
    for l in range(depth):
        w = w_in[l].astype(BF16)
        n_main = 3 * attn_w + 4 * mix_w
        w_gate = jnp.pad(w[:, n_main:], ((0, 0), (0, LANES - n_gates)))
        gate_b = jnp.pad(mlstm_gate_b[l], (0, LANES - n_gates)).reshape(1, LANES)
        qkv, qkv16, mall, gates = _inproj(xf, mix_pre_g[l], w[:, :n_main], w_gate, gate_b, cos_t, sin_t,
                                          attn_w=attn_w, mix_w=mix_w, seq=seq)
        attn = _attention(qkv, qkv16, batch=batch, seq=seq)
        gc, gr = _gateprep(gates, n_heads=n_heads)
        ml = _mlstm(mall, gc, gr, mlstm_conv_w[l], mlstm_conv_b[l].reshape(1, -1),
                    mlstm_head_g[l].reshape(1, -1), batch=batch, seq=seq, n_heads=n_heads)
        x1, hn2 = _outproj(attn, ml, w_out[l].astype(BF16), xf, mix_post_g[l], ffn_pre_g[l])
        xf = _ffn(hn2, x1, w_up[l].astype(BF16), w_down[l].astype(BF16), ffn_conv_w[l],
                  ffn_conv_b[l], ffn_post_g[l], seq=seq)
    return xf.reshape(batch, seq, d)
```

```python
import functools

import numpy as np
import jax
import jax.numpy as jnp
from jax import lax
from jax.experimental import pallas as pl
from jax.experimental.pallas import tpu as pltpu

F32 = jnp.float32
BF16 = jnp.bfloat16

ATTN_HEAD_DIM = 64
ATTN_HALF_WINDOW = 64
DILATIONS = (1, 4, 16)
MLSTM_HEADS = 4
MLSTM_CHUNK = 128
ROPE_THETA = 10000.0
NORM_EPS = 1e-6
NEG_INF = -1e30

LANES = 128
Q_BLOCK = 128
K_WINDOW = Q_BLOCK + 2 * ATTN_HALF_WINDOW
DIL_MID, DIL_MAX = DILATIONS[1], DILATIONS[2]
DIL_RATIO = DIL_MAX // DIL_MID
LOG2_E = float(np.log2(np.e))
ATTN_Q_SCALE = float(ATTN_HEAD_DIM ** -0.5) * LOG2_E
VMEM_LIMIT_BYTES = 56 * 1024 * 1024


def _cparams(n_grid_dims):
    return pltpu.CompilerParams(
        dimension_semantics=("arbitrary",) * n_grid_dims,
        vmem_limit_bytes=VMEM_LIMIT_BYTES)


def _rms(x, g):
    return x * lax.rsqrt(jnp.mean(x * x, axis=-1, keepdims=True) + NORM_EPS) * g


def _inproj_kernel(x_ref, g_ref, w_ref, wg_ref, gb_ref, cos_ref, sin_ref,
                   qkv_ref, qkv16_ref, mall_ref, gates_ref, *stage_refs, attn_w, mix_w):
    hn = _rms(x_ref[...], g_ref[...]).astype(w_ref.dtype)
    cos = cos_ref[...]
    sin = sin_ref[...]
    tm = hn.shape[0]
    lane = lax.broadcasted_iota(jnp.int32, (tm, LANES), 1)
    first_half = (lane % ATTN_HEAD_DIM) < (ATTN_HEAD_DIM // 2)
    n_pairs = attn_w // LANES
    base = 3 * attn_w

    def mix_group(grp):
        res = jnp.dot(hn, w_ref[:, base + grp * mix_w: base + (grp + 1) * mix_w],
                      preferred_element_type=F32)
        mall_ref[:, grp * mix_w:(grp + 1) * mix_w] = res.astype(mall_ref.dtype)

    for grp in range(3):
        res = jnp.dot(hn, w_ref[:, grp * attn_w:(grp + 1) * attn_w],
                      preferred_element_type=F32)
        mix_group(grp)
        for hp in range(n_pairs):
            xs = res[:, hp * LANES:(hp + 1) * LANES]
            if grp < 2:
                rot = jnp.where(first_half,
                                pltpu.roll(xs, LANES - ATTN_HEAD_DIM // 2, 1),
                                pltpu.roll(xs, ATTN_HEAD_DIM // 2, 1))
                xs = xs * cos + rot * sin
            if grp == 0:
                xs = xs * ATTN_Q_SCALE
            slab = grp * n_pairs + hp
            qkv_ref[slab] = xs.astype(qkv_ref.dtype)
            stage_ref = stage_refs[slab]
            stage_ref[...] = xs
            for r in range(DIL_MAX):
                qkv16_ref[slab, :, r * LANES:(r + 1) * LANES] = (
                    stage_ref[pl.ds(r, tm // DIL_MAX, stride=DIL_MAX), :].astype(qkv16_ref.dtype))
    mix_group(3)
    gates_ref[...] = jnp.dot(hn, wg_ref[...], preferred_element_type=F32) + gb_ref[...]


def _inproj(x, g_pre, w_main, w_gate, gate_b, cos, sin, *, attn_w, mix_w, seq, tm=512):
    t, d = x.shape
    n_pairs = attn_w // LANES
    n_main = w_main.shape[1]
    tiles_per_seq = seq // tm
    kern = functools.partial(_inproj_kernel, attn_w=attn_w, mix_w=mix_w)
    return pl.pallas_call(
        kern,
        grid=(t // tm,),
        in_specs=[pl.BlockSpec((tm, d), lambda i: (i, 0)),
                  pl.BlockSpec((1, d), lambda i: (0, 0)),
                  pl.BlockSpec((d, n_main), lambda i: (0, 0)),
                  pl.BlockSpec((d, LANES), lambda i: (0, 0)),
                  pl.BlockSpec((1, LANES), lambda i: (0, 0)),
                  pl.BlockSpec((tm, LANES), lambda i: (i % tiles_per_seq, 0)),
                  pl.BlockSpec((tm, LANES), lambda i: (i % tiles_per_seq, 0))],
        out_specs=[pl.BlockSpec((3 * n_pairs, tm, LANES), lambda i: (0, i, 0)),
                   pl.BlockSpec((3 * n_pairs, tm // DIL_MAX, DIL_MAX * LANES), lambda i: (0, i, 0)),
                   pl.BlockSpec((tm, 4 * mix_w), lambda i: (i, 0)),
                   pl.BlockSpec((tm, LANES), lambda i: (i, 0))],
        out_shape=[jax.ShapeDtypeStruct((3 * n_pairs, t, LANES), BF16),
                   jax.ShapeDtypeStruct((3 * n_pairs, t // DIL_MAX, DIL_MAX * LANES), BF16),
                   jax.ShapeDtypeStruct((t, 4 * mix_w), BF16),
                   jax.ShapeDtypeStruct((t, LANES), F32)],
        scratch_shapes=[pltpu.VMEM((tm, LANES), F32) for _ in range(3 * n_pairs)],
        compiler_params=_cparams(1),
        name="inproj",
    )(x, g_pre.reshape(1, d), w_main, w_gate, gate_b, cos, sin)


def _head_masks(rows, dtype):
    lane = lax.broadcasted_iota(jnp.int32, (rows, LANES), 1)
    h0 = jnp.where(lane < ATTN_HEAD_DIM, 1.0, 0.0).astype(dtype)
    return h0, (1.0 - h0.astype(F32)).astype(dtype)


def _attn_scores(q, k, bias):
    q0, q1 = _head_masks(Q_BLOCK, q.dtype)
    q_st = jnp.concatenate([q * q0, q * q1], axis=0)
    s = lax.dot_general(q_st, k, (((1,), (1,)), ((), ())), preferred_element_type=F32)
    return s + jnp.concatenate([bias, bias], axis=0)


def _attn_finish(s, v):
    v0, v1 = _head_masks(v.shape[0], v.dtype)
    m = jnp.max(s, axis=-1, keepdims=True)
    p = jnp.exp2(s - m).astype(v.dtype)
    p_cat = jnp.concatenate([p[:Q_BLOCK], p[Q_BLOCK:]], axis=1)
    v_ext = jnp.concatenate([jnp.concatenate([v * v0, v0], axis=1),
                             jnp.concatenate([v * v1, v1], axis=1)], axis=0)
    o = jnp.dot(p_cat, v_ext, preferred_element_type=F32)
    lane_o = lax.broadcasted_iota(jnp.int32, (Q_BLOCK, LANES), 1)
    m_lanes = jnp.where(lane_o < ATTN_HEAD_DIM, m[:Q_BLOCK], m[Q_BLOCK:])
    return o[:, :LANES], o[:, LANES:], m_lanes


def _attn_kernel(qn, kn, vn, q16, k16, v16, bias1_ref, bias4_ref, bias16_ref, o_ref,
                 acc_ref, den_ref, max_ref, *, seq):
    n_blocks = seq // Q_BLOCK
    n_mid = (seq // DIL_MID) // Q_BLOCK
    n_max = seq // DIL_MAX
    q_rows = Q_BLOCK // DIL_RATIO
    k_rows = K_WINDOW // DIL_RATIO

    def edge_table(blk, n):
        return jnp.where(blk > 0, 1, 0) + jnp.where(blk == n - 1, 1, 0)

    def nat_window(i):
        qs = pl.multiple_of(i * Q_BLOCK, Q_BLOCK)
        ks = pl.multiple_of(jnp.clip(qs - ATTN_HALF_WINDOW, 0, seq - K_WINDOW), ATTN_HALF_WINDOW)
        return qs, ks

    def mid_window(i):
        cls, blk = i // n_mid, i % n_mid
        r0 = pl.multiple_of(blk * q_rows, q_rows)
        k0 = pl.multiple_of(jnp.clip(r0 - ATTN_HALF_WINDOW // DIL_RATIO, 0, n_max - k_rows),
                            ATTN_HALF_WINDOW // DIL_RATIO)
        lanes = [pl.ds(pl.multiple_of((cls + DIL_MID * m) * LANES, LANES), LANES) for m in range(DIL_RATIO)]
        return cls, blk, r0, k0, lanes

    def gather(ref, r0, rows, lanes):
        return jnp.concatenate([ref[pl.ds(r0, rows), ln] for ln in lanes], axis=0)

    def store(branch, start, size, stride, vals, rows=slice(None)):
        for ref, val in zip((acc_ref, den_ref, max_ref), vals):
            idx = pl.ds(start, size) if stride == 1 else pl.ds(start, size, stride=stride)
            ref[branch, idx, :] = val[rows]

    def body(i, carry):
        qs, ks = nat_window(i)
        s = _attn_scores(qn[pl.ds(qs, Q_BLOCK), :], kn[pl.ds(ks, K_WINDOW), :],
                         bias1_ref[edge_table(i, n_blocks)])
        store(0, qs, Q_BLOCK, 1, _attn_finish(s, vn[pl.ds(ks, K_WINDOW), :]))

        cls, blk, r0, k0, lanes = mid_window(i)
        s = _attn_scores(gather(q16, r0, q_rows, lanes), gather(k16, k0, k_rows, lanes),
                         bias4_ref[edge_table(blk, n_mid)])
        res = _attn_finish(s, gather(v16, k0, k_rows, lanes))
        for m in range(DIL_RATIO):
            store(1, DIL_MAX * r0 + DIL_MID * m + cls, q_rows, DIL_MAX, res,
                  rows=slice(m * q_rows, (m + 1) * q_rows))

        ln = pl.ds(pl.multiple_of(i * LANES, LANES), LANES)
        s = _attn_scores(q16[:, ln], k16[:, ln], bias16_ref[...])
        store(2, i, n_max, DIL_MAX, _attn_finish(s, v16[:, ln]))
        return carry

    lax.fori_loop(0, n_blocks, body, 0, unroll=8)

    def combine(i, carry):
        rows = pl.ds(pl.multiple_of(i * Q_BLOCK, Q_BLOCK), Q_BLOCK)
        ms = [max_ref[b, rows, :] for b in range(3)]
        mx = jnp.maximum(jnp.maximum(ms[0], ms[1]), ms[2])
        ws = [jnp.exp2(m - mx) for m in ms]
        num = ws[0] * acc_ref[0, rows, :] + ws[1] * acc_ref[1, rows, :] + ws[2] * acc_ref[2, rows, :]
        den = ws[0] * den_ref[0, rows, :] + ws[1] * den_ref[1, rows, :] + ws[2] * den_ref[2, rows, :]
        o_ref[rows, :] = (num / den).astype(o_ref.dtype)
        return carry

    lax.fori_loop(0, n_blocks, combine, 0)


def _attn_bias_tables():
    hw = ATTN_HALF_WINDOW
    row = np.arange(Q_BLOCK)[:, None]
    col = np.arange(K_WINDOW)[None, :]

    def bias(off):
        return np.where(np.abs(off) <= hw, 0.0, NEG_INF).astype(np.float32)

    nat = np.stack([bias(col - shift - row) for shift in (0, hw, 2 * hw)])
    q_rows, k_rows = Q_BLOCK // DIL_RATIO, K_WINDOW // DIL_RATIO
    qpos = DIL_RATIO * (row % q_rows) + row // q_rows
    kpos = DIL_RATIO * (col % k_rows) + col // k_rows
    mid = np.stack([bias(kpos - shift - qpos) for shift in (0, hw, 2 * hw)])
    wide = bias(np.arange(Q_BLOCK)[None, :] - row)
    return nat, mid, wide


def _attention(qkv, qkv16, *, batch, seq):
    n3, t, _ = qkv.shape
    n_pairs = n3 // 3
    n_max = seq // DIL_MAX
    assert DILATIONS[0] == 1 and n_max == Q_BLOCK and seq % (DIL_MID * Q_BLOCK) == 0
    nat, mid, wide = _attn_bias_tables()
    in_specs = ([pl.BlockSpec((None, seq, LANES), lambda b, hp, w=w: (w * n_pairs + hp, b, 0))
                 for w in range(3)]
                + [pl.BlockSpec((None, n_max, DIL_MAX * LANES), lambda b, hp, w=w: (w * n_pairs + hp, b, 0))
                   for w in range(3)]
                + [pl.BlockSpec(nat.shape, lambda b, hp: (0, 0, 0)),
                   pl.BlockSpec(mid.shape, lambda b, hp: (0, 0, 0)),
                   pl.BlockSpec(wide.shape, lambda b, hp: (0, 0))])
    scratch = [pltpu.VMEM((3, seq, LANES), F32) for _ in range(3)]
    return pl.pallas_call(
        functools.partial(_attn_kernel, seq=seq),
        grid=(batch, n_pairs),
        in_specs=in_specs,
        out_specs=pl.BlockSpec((seq, LANES), lambda b, hp: (b, hp)),
        out_shape=jax.ShapeDtypeStruct((t, n_pairs * LANES), BF16),
        scratch_shapes=scratch,
        compiler_params=_cparams(2),
        name="dilated_attn",
    )(qkv, qkv, qkv, qkv16, qkv16, qkv16, nat, mid, wide)


GATE_GROUP = 16
GATE_ROWS = 3 * GATE_GROUP


def _gateprep_kernel(g_ref, o_ref, orow_ref, *, n_heads):
    ch = MLSTM_CHUNK
    row = lax.broadcasted_iota(jnp.int32, (ch, ch), 0)
    col = lax.broadcasted_iota(jnp.int32, (ch, ch), 1)
    tri_prefix = (col <= row).astype(F32)
    tri_suffix = (col >= row).astype(F32)
    lane = lax.broadcasted_iota(jnp.int32, (ch, LANES), 1)
    srow = lax.broadcasted_iota(jnp.int32, (ch, LANES), 0)
    is_fwd = lane < 2 * n_heads

    for c in range(g_ref.shape[0] // ch):
        g = g_ref[c * ch:(c + 1) * ch, :]
        log_f = jnp.minimum(g, 0.0) - jnp.log(1.0 + jnp.exp(-jnp.abs(g)))
        pre = jnp.dot(tri_prefix, log_f, preferred_element_type=F32, precision=lax.Precision.HIGHEST)
        suf = jnp.dot(tri_suffix, log_f, preferred_element_type=F32, precision=lax.Precision.HIGHEST)
        b = jnp.where(is_fwd, pre, suf)
        a = pltpu.roll(g, n_heads, 1) - b
        cm = a
        step = 1
        while step < ch:
            ahead = jnp.where(srow >= step, pltpu.roll(cm, step, 0), -jnp.inf)
            behind = jnp.where(srow + step < ch, pltpu.roll(cm, ch - step, 0), -jnp.inf)
            cm = jnp.maximum(cm, jnp.where(is_fwd, ahead, behind))
            step *= 2
        out = jnp.where(lane < GATE_GROUP, b,
                        jnp.where(lane < 2 * GATE_GROUP, pltpu.roll(a, GATE_GROUP, 1),
                                  pltpu.roll(cm, 2 * GATE_GROUP, 1)))
        out = out * LOG2_E
        o_ref[c * ch:(c + 1) * ch, :] = out
        orow_ref[c] = out.T[:GATE_ROWS, :]


def _gateprep(gates, *, n_heads, tm=1024):
    t, w = gates.shape
    assert 4 * n_heads == GATE_GROUP and GATE_ROWS <= w
    cpt = tm // MLSTM_CHUNK
    return pl.pallas_call(
        functools.partial(_gateprep_kernel, n_heads=n_heads),
        grid=(t // tm,),
        in_specs=[pl.BlockSpec((tm, w), lambda i: (i, 0))],
        out_specs=[pl.BlockSpec((tm, w), lambda i: (i, 0)),
                   pl.BlockSpec((cpt, GATE_ROWS, MLSTM_CHUNK), lambda i: (i, 0, 0))],
        out_shape=[jax.ShapeDtypeStruct((t, w), F32),
                   jax.ShapeDtypeStruct((t // MLSTM_CHUNK, GATE_ROWS, MLSTM_CHUNK), F32)],
        compiler_params=_cparams(1),
        name="gateprep",
    )(gates)


CONV_HALO = 16


def _conv_silu_chunks(x_ref, w_ref, b_ref, stage_ref, emit):
    n = x_ref.shape[0]
    taps = w_ref.shape[0]
    pad = taps // 2
    zeros = jnp.zeros((CONV_HALO, x_ref.shape[1]), F32)
    stage_ref[0:CONV_HALO, :] = zeros
    stage_ref[CONV_HALO + n:, :] = zeros
    stage_ref[CONV_HALO:CONV_HALO + n, :] = x_ref[...].astype(F32)

    def body(c, carry):
        base = pl.multiple_of(c * MLSTM_CHUNK, MLSTM_CHUNK)
        acc = b_ref[...]
        for j in range(taps):
            acc = acc + stage_ref[pl.ds(base + (CONV_HALO + j - pad), MLSTM_CHUNK), :] * w_ref[j:j + 1, :]
        emit(c, acc * jax.nn.sigmoid(acc))
        return carry

    lax.fori_loop(0, n // MLSTM_CHUNK, body, 0, unroll=2)


def _mlstm_kernel(mq_ref, mk_ref, mv_ref, mo_ref, gc_ref, gr_ref, cwq_ref, cwk_ref, cbq_ref, cbk_ref,
                  hg_ref, o_ref, q_s, kt_s, vx_s, gc_s, sc_s, hf_s, hb_s, stage_s, *, n_heads):
    head = pl.program_id(1)
    seq, dh = q_s.shape
    ch = MLSTM_CHUNK
    n_chunks = seq // ch
    def emit_q(c, y):
        q_s[pl.ds(pl.multiple_of(c * ch, ch), ch), :] = y.astype(q_s.dtype)

    def emit_k(c, y):
        kt_s[:, pl.ds(pl.multiple_of(c * ch, ch), ch)] = (y * (dh ** -0.5)).T.astype(kt_s.dtype)

    _conv_silu_chunks(mq_ref, cwq_ref, cbq_ref, stage_s, emit_q)
    _conv_silu_chunks(mk_ref, cwk_ref, cbk_ref, stage_s, emit_k)
    vx_s[:, :dh] = mv_ref[...]
    vx_s[:, dh:] = jnp.ones((seq, dh), vx_s.dtype)
    gc_s[...] = pltpu.roll(gc_ref[...], (LANES - head) % LANES, 1)

    def gate_lane(group, d):
        return group * GATE_GROUP + (2 * d + 1) * n_heads

    for d in range(2):
        lane_b, lane_cm = gate_lane(0, d), gate_lane(2, d)
        m_run = jnp.zeros((1, 2 * dh), F32)
        for step in range(n_chunks):
            c = step if d == 0 else n_chunks - 1 - step
            last = c * ch + (ch - 1 if d == 0 else 0)
            b_last = jnp.broadcast_to(gc_s[last:last + 1, lane_b:lane_b + 1], (1, 2 * dh))
            a_max = jnp.broadcast_to(gc_s[last:last + 1, lane_cm:lane_cm + 1], (1, 2 * dh))
            w_max = jnp.maximum(m_run, a_max)
            sc_s[d, 0, c:c + 1, :] = m_run
            sc_s[d, 1, c:c + 1, :] = w_max
            sc_s[d, 2, c:c + 1, :] = jnp.exp2(m_run - w_max)
            m_run = b_last + w_max

    row = lax.broadcasted_iota(jnp.int32, (ch, ch), 0)
    col = lax.broadcasted_iota(jnp.int32, (ch, ch), 1)

    def direction(c, d, mask, state, h_store):
        lane_b, lane_cm = gate_lane(0, d), gate_lane(2, d)
        rs = pl.multiple_of(c * ch, ch)
        qc = q_s[pl.ds(rs, ch), :]
        ktc = kt_s[:, pl.ds(rs, ch)]
        vx = vx_s[pl.ds(rs, ch), :]
        a_row = gr_ref[c, pl.ds(gate_lane(1, d) + head, 1), :]
        m_row = sc_s[d, 0, pl.ds(c, 1), :]
        w_max = sc_s[d, 1, pl.ds(c, 1), :]
        decay = sc_s[d, 2, pl.ds(c, 1), :]
        g = gc_s[pl.ds(rs, ch), :]
        b_bc = jnp.broadcast_to(g[:, lane_b:lane_b + 1], (ch, dh))
        cm_bc = jnp.broadcast_to(g[:, lane_cm:lane_cm + 1], (ch, dh))
        mm = jnp.maximum(m_row[:, :dh], cm_bc)
        w = jnp.where(mask, jnp.exp2(a_row - mm), 0.0)
        s = jnp.dot(qc, ktc, preferred_element_type=F32)
        inter = jnp.dot(qc, state.astype(qc.dtype), preferred_element_type=F32)
        kw_t = (ktc.astype(F32) * jnp.exp2(a_row - w_max[:, :ch])).astype(vx.dtype)
        new_state = decay * state + jnp.dot(kw_t, vx, preferred_element_type=F32)
        intra = jnp.dot((s * w).astype(vx.dtype), vx, preferred_element_type=F32)
        g_int = jnp.exp2(m_row[:, :dh] - mm)
        num = intra[:, :dh] + g_int * inter[:, :dh]
        den = intra[:, dh:] + g_int * inter[:, dh:]
        h_store[pl.ds(rs, ch), :] = num / jnp.maximum(jnp.abs(den), jnp.exp2(-(b_bc + mm)))
        return new_state

    def step(j, carry):
        fwd, bwd = carry
        fwd = direction(j, 0, col <= row, fwd, hf_s)
        bwd = direction(n_chunks - 1 - j, 1, col >= row, bwd, hb_s)
        return fwd, bwd

    zero = jnp.zeros((dh, 2 * dh), F32)
    lax.fori_loop(0, n_chunks, step, (zero, zero), unroll=8)

    hm = hf_s[...] + hb_s[...]
    hm = hm * lax.rsqrt(jnp.mean(hm * hm, axis=-1, keepdims=True) + NORM_EPS)
    o_ref[...] = (hm * hg_ref[...] * jax.nn.sigmoid(mo_ref[...].astype(F32))).astype(o_ref.dtype)


def _mlstm(mall, gc, gr, conv_w, conv_b, head_g, *, batch, seq, n_heads):
    t = mall.shape[0]
    dh = mall.shape[1] // (4 * n_heads)
    taps = conv_w.shape[0]
    n_chunks = seq // MLSTM_CHUNK
    assert dh == MLSTM_CHUNK == LANES

    def col_block(offset):
        return pl.BlockSpec((seq, dh), lambda b, h, offset=offset: (b, offset * n_heads + h))

    in_specs = [col_block(0), col_block(1), col_block(2), col_block(3),
                pl.BlockSpec((seq, LANES), lambda b, h: (b, 0)),
                pl.BlockSpec((n_chunks, GATE_ROWS, MLSTM_CHUNK), lambda b, h: (b, 0, 0)),
                pl.BlockSpec((taps, dh), lambda b, h: (0, h)),
                pl.BlockSpec((taps, dh), lambda b, h: (0, n_heads + h)),
                pl.BlockSpec((1, dh), lambda b, h: (0, h)),
                pl.BlockSpec((1, dh), lambda b, h: (0, n_heads + h)),
                pl.BlockSpec((1, dh), lambda b, h: (0, h))]
    scratch = [pltpu.VMEM((seq, dh), BF16),
               pltpu.VMEM((dh, seq), BF16),
               pltpu.VMEM((seq, 2 * dh), BF16),
               pltpu.VMEM((seq, LANES), F32),
               pltpu.VMEM((2, 3, n_chunks, 2 * dh), F32),
               pltpu.VMEM((seq, dh), F32), pltpu.VMEM((seq, dh), F32),
               pltpu.VMEM((seq + 2 * CONV_HALO, dh), F32)]
    return pl.pallas_call(
        functools.partial(_mlstm_kernel, n_heads=n_heads),
        grid=(batch, n_heads),
        in_specs=in_specs,
        out_specs=pl.BlockSpec((seq, dh), lambda b, h: (b, h)),
        out_shape=jax.ShapeDtypeStruct((t, n_heads * dh), BF16),
        scratch_shapes=scratch,
        compiler_params=_cparams(2),
        name="mlstm",
    )(mall, mall, mall, mall, gc, gr, conv_w, conv_w, conv_b, conv_b, head_g)


def _outproj_kernel(a_ref, m_ref, w_ref, x_ref, gpost_ref, gpre_ref, x1_ref, hn_ref):
    ka = a_ref.shape[1]
    mixed = (jnp.dot(a_ref[...], w_ref[:ka, :], preferred_element_type=F32)
             + jnp.dot(m_ref[...], w_ref[ka:, :], preferred_element_type=F32))
    x1 = x_ref[...] + _rms(mixed, gpost_ref[...])
    x1_ref[...] = x1
    hn_ref[...] = _rms(x1, gpre_ref[...]).astype(hn_ref.dtype)


def _outproj(attn, ml, w_out, x, g_post, g_pre, tm=1024):
    t, d = x.shape
    ka, km = attn.shape[1], ml.shape[1]
    return pl.pallas_call(
        _outproj_kernel,
        grid=(t // tm,),
        in_specs=[pl.BlockSpec((tm, ka), lambda i: (i, 0)),
                  pl.BlockSpec((tm, km), lambda i: (i, 0)),
                  pl.BlockSpec((ka + km, d), lambda i: (0, 0)),
                  pl.BlockSpec((tm, d), lambda i: (i, 0)),
                  pl.BlockSpec((1, d), lambda i: (0, 0)),
                  pl.BlockSpec((1, d), lambda i: (0, 0))],
        out_specs=[pl.BlockSpec((tm, d), lambda i: (i, 0)),
                   pl.BlockSpec((tm, d), lambda i: (i, 0))],
        out_shape=[jax.ShapeDtypeStruct((t, d), F32), jax.ShapeDtypeStruct((t, d), BF16)],
        compiler_params=_cparams(1),
        name="outproj",
    )(attn, ml, w_out, x, g_post.reshape(1, d), g_pre.reshape(1, d))


FFN_HALO = 16
FFN_CHUNK = 256
FFN_STAGES = 4
FFN_SUBTILES = 1


def _ffn_kernel(prev_ref, main_ref, next_ref, x1_ref, wup_ref, wdn_ref, cw_ref, cb_ref,
                gpost_ref, x2_ref, *scratch, blocks_per_seq):
    ug_refs = scratch[:FFN_STAGES]
    uv_refs = scratch[FFN_STAGES:2 * FFN_STAGES]
    acc_refs = scratch[2 * FFN_STAGES:]
    i = pl.program_id(0)
    tm = main_ref.shape[0] // FFN_SUBTILES
    d_ff = wdn_ref.shape[0]
    pos = i % blocks_per_seq
    prev = jnp.where(pos == 0, jnp.zeros_like(prev_ref[...]), prev_ref[...])
    nxt = jnp.where(pos == blocks_per_seq - 1, jnp.zeros_like(next_ref[...]), next_ref[...])
    lhs = []
    for s in range(FFN_SUBTILES):
        before = prev if s == 0 else main_ref[s * tm - FFN_HALO:s * tm, :]
        after = nxt if s == FFN_SUBTILES - 1 else main_ref[(s + 1) * tm:(s + 1) * tm + FFN_HALO, :]
        lhs.append(jnp.concatenate([before, main_ref[s * tm:(s + 1) * tm, :], after], axis=0))
    taps = cw_ref.shape[0]

    row0 = pl.multiple_of(jnp.minimum(i, 0), FFN_HALO)

    def stage(u_ref, u):
        for k in range(FFN_CHUNK // LANES):
            u_ref[k] = u[:, k * LANES:(k + 1) * LANES]

    def conv(u_ref, k, c0):
        cols = slice(c0 + k * LANES, c0 + (k + 1) * LANES)
        out = cb_ref[:, cols]
        for j in range(taps):
            start = FFN_HALO + j - taps // 2
            out = out + u_ref[k, pl.ds(row0 + start, tm), :] * cw_ref[j:j + 1, cols]
        return out

    gelu_c0 = float(np.sqrt(2.0 / np.pi))
    gelu_c1 = float(np.sqrt(2.0 / np.pi) * 0.044715)
    n_chunks = d_ff // FFN_CHUNK
    n_total = FFN_SUBTILES * n_chunks

    def up_project(g):
        s, c0 = g // n_chunks, (g % n_chunks) * FFN_CHUNK
        stage(ug_refs[g % FFN_STAGES], jnp.dot(lhs[s], wup_ref[:, c0:c0 + FFN_CHUNK],
                                               preferred_element_type=F32))
        stage(uv_refs[g % FFN_STAGES], jnp.dot(lhs[s], wup_ref[:, d_ff + c0:d_ff + c0 + FFN_CHUNK],
                                               preferred_element_type=F32))

    for g in range(FFN_STAGES - 1):
        up_project(g)
    for g in range(n_total):
        s, c = g // n_chunks, g % n_chunks
        c0 = c * FFN_CHUNK
        slot = g % FFN_STAGES
        if g + FFN_STAGES - 1 < n_total:
            up_project(g + FFN_STAGES - 1)
        acts = []
        for k in range(FFN_CHUNK // LANES):
            gate = conv(ug_refs[slot], k, c0)
            val = conv(uv_refs[slot], k, d_ff + c0)
            half = (0.5 * gate) * val
            inner = gate * (gelu_c0 + gelu_c1 * (gate * gate))
            acts.append((half + half * jnp.tanh(inner)).astype(lhs[s].dtype))
        act = jnp.concatenate(acts, axis=1)
        part = jnp.dot(act, wdn_ref[c0:c0 + FFN_CHUNK, :], preferred_element_type=F32)
        if c == 0:
            acc_refs[s][...] = part
        else:
            acc_refs[s][...] += part
        if c == n_chunks - 1:
            rows = slice(s * tm, (s + 1) * tm)
            x2_ref[rows, :] = x1_ref[rows, :] + _rms(acc_refs[s][...], gpost_ref[...])


def _ffn(hn, x1, w_up, w_down, conv_w, conv_b, g_post, *, seq, tm=512):
    t, d = x1.shape
    d_ff = w_down.shape[0]
    rows = FFN_SUBTILES * tm
    blocks_per_seq = seq // rows
    halo_per_block = rows // FFN_HALO
    n_halo_blocks = t // FFN_HALO
    return pl.pallas_call(
        functools.partial(_ffn_kernel, blocks_per_seq=blocks_per_seq),
        grid=(t // rows,),
        in_specs=[pl.BlockSpec((FFN_HALO, d), lambda i: (jnp.maximum(i * halo_per_block - 1, 0), 0)),
                  pl.BlockSpec((rows, d), lambda i: (i, 0)),
                  pl.BlockSpec((FFN_HALO, d),
                               lambda i: (jnp.minimum((i + 1) * halo_per_block, n_halo_blocks - 1), 0)),
                  pl.BlockSpec((rows, d), lambda i: (i, 0)),
                  pl.BlockSpec((d, 2 * d_ff), lambda i: (0, 0), pipeline_mode=pl.Buffered(1)),
                  pl.BlockSpec((d_ff, d), lambda i: (0, 0), pipeline_mode=pl.Buffered(1)),
                  pl.BlockSpec(conv_w.shape, lambda i: (0, 0)),
                  pl.BlockSpec((1, 2 * d_ff), lambda i: (0, 0)),
                  pl.BlockSpec((1, d), lambda i: (0, 0))],
        out_specs=pl.BlockSpec((rows, d), lambda i: (i, 0)),
        out_shape=jax.ShapeDtypeStruct((t, d), F32),
        scratch_shapes=([pltpu.VMEM((FFN_CHUNK // LANES, tm + 2 * FFN_HALO, LANES), F32)
                         for _ in range(2 * FFN_STAGES)]
                        + [pltpu.VMEM((tm, d), F32) for _ in range(FFN_SUBTILES)]),
        compiler_params=_cparams(1),
        name="convffn",
    )(hn, hn, hn, x1, w_up, w_down, conv_w, conv_b.reshape(1, -1), g_post.reshape(1, d))


def _rotary_tables(seq):
    half = ATTN_HEAD_DIM // 2
    inv_freq = ROPE_THETA ** (-jnp.arange(0, ATTN_HEAD_DIM, 2, dtype=F32) / ATTN_HEAD_DIM)
    ang = jnp.arange(seq, dtype=F32)[:, None] * inv_freq[None, :]
    cos, sin = jnp.cos(ang), jnp.sin(ang)
    reps = LANES // ATTN_HEAD_DIM
    cos_t = jnp.tile(jnp.concatenate([cos, cos], axis=-1), (1, reps))
    sin_t = jnp.tile(jnp.concatenate([-sin, sin], axis=-1), (1, reps))
    return cos_t, sin_t


def kernel(x, mix_pre_g, w_in, mlstm_conv_w, mlstm_conv_b, mlstm_gate_b, mlstm_head_g, w_out,
           mix_post_g, ffn_pre_g, w_up, ffn_conv_w, ffn_conv_b, w_down, ffn_post_g):
    batch, seq, d = x.shape
    depth = w_in.shape[0]
    t = batch * seq
    n_heads = MLSTM_HEADS
    n_gates = mlstm_gate_b.shape[1]
    mix_w = mlstm_head_g.shape[1]
    attn_w = w_out.shape[1] - mix_w
    assert n_gates == 4 * n_heads and n_gates <= LANES
    assert w_in.shape[2] == 3 * attn_w + 4 * mix_w + n_gates

    cos_t, sin_t = _rotary_tables(seq)
    xf = x.reshape(t, d)
    for l in range(depth):
        w = w_in[l].astype(BF16)
        n_main = 3 * attn_w + 4 * mix_w
        w_gate = jnp.pad(w[:, n_main:], ((0, 0), (0, LANES - n_gates)))
        gate_b = jnp.pad(mlstm_gate_b[l], (0, LANES - n_gates)).reshape(1, LANES)
        qkv, qkv16, mall, gates = _inproj(xf, mix_pre_g[l], w[:, :n_main], w_gate, gate_b, cos_t, sin_t,
                                          attn_w=attn_w, mix_w=mix_w, seq=seq)
        attn = _attention(qkv, qkv16, batch=batch, seq=seq)
        gc, gr = _gateprep(gates, n_heads=n_heads)
        ml = _mlstm(mall, gc, gr, mlstm_conv_w[l], mlstm_conv_b[l].reshape(1, -1),
                    mlstm_head_g[l].reshape(1, -1), batch=batch, seq=seq, n_heads=n_heads)
        x1, hn2 = _outproj(attn, ml, w_out[l].astype(BF16), xf, mix_post_g[l], ffn_pre_g[l])
        xf = _ffn(hn2, x1, w_up[l].astype(BF16), w_down[l].astype(BF16), ffn_conv_w[l],
                  ffn_conv_b[l], ffn_post_g[l], seq=seq)
    return xf.reshape(batch, seq, d)
```

```python
import functools

import numpy as np
import jax
import jax.numpy as jnp
from jax import lax
from jax.experimental import pallas as pl
from jax.experimental.pallas import tpu as pltpu

F32 = jnp.float32
BF16 = jnp.bfloat16

ATTN_HEAD_DIM = 64
ATTN_HALF_WINDOW = 64
DILATIONS = (1, 4, 16)
MLSTM_HEADS = 4
MLSTM_CHUNK = 128
ROPE_THETA = 10000.0
NORM_EPS = 1e-6
NEG_INF = -1e30

LANES = 128
Q_BLOCK = 128
K_WINDOW = Q_BLOCK + 2 * ATTN_HALF_WINDOW
DIL_MID, DIL_MAX = DILATIONS[1], DILATIONS[2]
DIL_RATIO = DIL_MAX // DIL_MID
LOG2_E = float(np.log2(np.e))
ATTN_Q_SCALE = float(ATTN_HEAD_DIM ** -0.5) * LOG2_E
VMEM_LIMIT_BYTES = 56 * 1024 * 1024


def _cparams(n_grid_dims):
    return pltpu.CompilerParams(
        dimension_semantics=("arbitrary",) * n_grid_dims,
        vmem_limit_bytes=VMEM_LIMIT_BYTES)


def _rms(x, g):
    return x * lax.rsqrt(jnp.mean(x * x, axis=-1, keepdims=True) + NORM_EPS) * g


def _inproj_kernel(x_ref, g_ref, w32_ref, gb_ref, cos_ref, sin_ref,
                   qkv_ref, qkv16_ref, mall_ref, gates_ref, w_ref, wg_ref, *stage_refs, attn_w, mix_w):
    n_main = w_ref.shape[1]

    @pl.when(pl.program_id(0) == 0)
    def _():
        w_ref[...] = w32_ref[:, :n_main].astype(w_ref.dtype)
        wg_ref[...] = jnp.zeros(wg_ref.shape, wg_ref.dtype)
        n_gates = w32_ref.shape[1] - n_main
        wg_ref[:, :n_gates] = w32_ref[:, n_main:].astype(wg_ref.dtype)

    hn = _rms(x_ref[...], g_ref[...]).astype(w_ref.dtype)
    cos = cos_ref[...]
    sin = sin_ref[...]
    tm = hn.shape[0]
    lane = lax.broadcasted_iota(jnp.int32, (tm, LANES), 1)
    first_half = (lane % ATTN_HEAD_DIM) < (ATTN_HEAD_DIM // 2)
    n_pairs = attn_w // LANES
    base = 3 * attn_w

    def mix_group(grp):
        res = jnp.dot(hn, w_ref[:, base + grp * mix_w: base + (grp + 1) * mix_w],
                      preferred_element_type=F32)
        mall_ref[:, grp * mix_w:(grp + 1) * mix_w] = res.astype(mall_ref.dtype)

    for grp in range(3):
        res = jnp.dot(hn, w_ref[:, grp * attn_w:(grp + 1) * attn_w],
                      preferred_element_type=F32)
        mix_group(grp)
        for hp in range(n_pairs):
            xs = res[:, hp * LANES:(hp + 1) * LANES]
            if grp < 2:
                rot = jnp.where(first_half,
                                pltpu.roll(xs, LANES - ATTN_HEAD_DIM // 2, 1),
                                pltpu.roll(xs, ATTN_HEAD_DIM // 2, 1))
                xs = xs * cos + rot * sin
            if grp == 0:
                xs = xs * ATTN_Q_SCALE
            slab = grp * n_pairs + hp
            qkv_ref[slab] = xs.astype(qkv_ref.dtype)
            stage_ref = stage_refs[slab]
            stage_ref[...] = xs
            for r in range(DIL_MAX):
                qkv16_ref[slab, :, r * LANES:(r + 1) * LANES] = (
                    stage_ref[pl.ds(r, tm // DIL_MAX, stride=DIL_MAX), :].astype(qkv16_ref.dtype))
    mix_group(3)
    gates_ref[...] = jnp.dot(hn, wg_ref[...], preferred_element_type=F32) + gb_ref[...]


def _inproj(x, g_pre, w_in, layer, gate_b, cos, sin, *, attn_w, mix_w, seq, tm=512):
    t, d = x.shape
    n_pairs = attn_w // LANES
    n_cols = w_in.shape[2]
    n_main = 3 * attn_w + 4 * mix_w
    tiles_per_seq = seq // tm
    kern = functools.partial(_inproj_kernel, attn_w=attn_w, mix_w=mix_w)
    return pl.pallas_call(
        kern,
        grid=(t // tm,),
        in_specs=[pl.BlockSpec((tm, d), lambda i: (i, 0)),
                  pl.BlockSpec((1, d), lambda i: (0, 0)),
                  pl.BlockSpec((None, d, n_cols), lambda i: (layer, 0, 0), pipeline_mode=pl.Buffered(1)),
                  pl.BlockSpec((1, LANES), lambda i: (0, 0)),
                  pl.BlockSpec((tm, LANES), lambda i: (i % tiles_per_seq, 0)),
                  pl.BlockSpec((tm, LANES), lambda i: (i % tiles_per_seq, 0))],
        out_specs=[pl.BlockSpec((3 * n_pairs, tm, LANES), lambda i: (0, i, 0)),
                   pl.BlockSpec((3 * n_pairs, tm // DIL_MAX, DIL_MAX * LANES), lambda i: (0, i, 0)),
                   pl.BlockSpec((tm, 4 * mix_w), lambda i: (i, 0)),
                   pl.BlockSpec((tm, LANES), lambda i: (i, 0))],
        out_shape=[jax.ShapeDtypeStruct((3 * n_pairs, t, LANES), BF16),
                   jax.ShapeDtypeStruct((3 * n_pairs, t // DIL_MAX, DIL_MAX * LANES), BF16),
                   jax.ShapeDtypeStruct((t, 4 * mix_w), BF16),
                   jax.ShapeDtypeStruct((t, LANES), F32)],
        scratch_shapes=([pltpu.VMEM((d, n_main), BF16), pltpu.VMEM((d, LANES), BF16)]
                        + [pltpu.VMEM((tm, LANES), F32) for _ in range(3 * n_pairs)]),
        compiler_params=_cparams(1),
        name="inproj",
    )(x, g_pre.reshape(1, d), w_in, gate_b, cos, sin)


def _head_masks(rows, dtype):
    lane = lax.broadcasted_iota(jnp.int32, (rows, LANES), 1)
    h0 = jnp.where(lane < ATTN_HEAD_DIM, 1.0, 0.0).astype(dtype)
    return h0, (1.0 - h0.astype(F32)).astype(dtype)


def _attn_scores(q, k, bias):
    q0, q1 = _head_masks(Q_BLOCK, q.dtype)
    q_st = jnp.concatenate([q * q0, q * q1], axis=0)
    s = lax.dot_general(q_st, k, (((1,), (1,)), ((), ())), preferred_element_type=F32)
    return s + jnp.concatenate([bias, bias], axis=0)


def _attn_finish(s, v):
    v0, v1 = _head_masks(v.shape[0], v.dtype)
    m = jnp.max(s, axis=-1, keepdims=True)
    p = jnp.exp2(s - m).astype(v.dtype)
    p_cat = jnp.concatenate([p[:Q_BLOCK], p[Q_BLOCK:]], axis=1)
    v_ext = jnp.concatenate([jnp.concatenate([v * v0, v0], axis=1),
                             jnp.concatenate([v * v1, v1], axis=1)], axis=0)
    o = jnp.dot(p_cat, v_ext, preferred_element_type=F32)
    lane_o = lax.broadcasted_iota(jnp.int32, (Q_BLOCK, LANES), 1)
    m_lanes = jnp.where(lane_o < ATTN_HEAD_DIM, m[:Q_BLOCK], m[Q_BLOCK:])
    return o[:, :LANES], o[:, LANES:], m_lanes


def _attn_kernel(qn, kn, vn, q16, k16, v16, bias1_ref, bias4_ref, bias16_ref, o_ref,
                 acc_ref, den_ref, max_ref, *, seq):
    n_blocks = seq // Q_BLOCK
    n_mid = (seq // DIL_MID) // Q_BLOCK
    n_max = seq // DIL_MAX
    q_rows = Q_BLOCK // DIL_RATIO
    k_rows = K_WINDOW // DIL_RATIO

    def edge_table(blk, n):
        return jnp.where(blk > 0, 1, 0) + jnp.where(blk == n - 1, 1, 0)

    def nat_window(i):
        qs = pl.multiple_of(i * Q_BLOCK, Q_BLOCK)
        ks = pl.multiple_of(jnp.clip(qs - ATTN_HALF_WINDOW, 0, seq - K_WINDOW), ATTN_HALF_WINDOW)
        return qs, ks

    def mid_window(i):
        cls, blk = i // n_mid, i % n_mid
        r0 = pl.multiple_of(blk * q_rows, q_rows)
        k0 = pl.multiple_of(jnp.clip(r0 - ATTN_HALF_WINDOW // DIL_RATIO, 0, n_max - k_rows),
                            ATTN_HALF_WINDOW // DIL_RATIO)
        lanes = [pl.ds(pl.multiple_of((cls + DIL_MID * m) * LANES, LANES), LANES) for m in range(DIL_RATIO)]
        return cls, blk, r0, k0, lanes

    def gather(ref, r0, rows, lanes):
        return jnp.concatenate([ref[pl.ds(r0, rows), ln] for ln in lanes], axis=0)

    def store(branch, start, size, stride, vals, rows=slice(None)):
        for ref, val in zip((acc_ref, den_ref, max_ref), vals):
            idx = pl.ds(start, size) if stride == 1 else pl.ds(start, size, stride=stride)
            ref[branch, idx, :] = val[rows]

    def body(i, carry):
        qs, ks = nat_window(i)
        s = _attn_scores(qn[pl.ds(qs, Q_BLOCK), :], kn[pl.ds(ks, K_WINDOW), :],
                         bias1_ref[edge_table(i, n_blocks)])
        store(0, qs, Q_BLOCK, 1, _attn_finish(s, vn[pl.ds(ks, K_WINDOW), :]))

        cls, blk, r0, k0, lanes = mid_window(i)
        s = _attn_scores(gather(q16, r0, q_rows, lanes), gather(k16, k0, k_rows, lanes),
                         bias4_ref[edge_table(blk, n_mid)])
        res = _attn_finish(s, gather(v16, k0, k_rows, lanes))
        for m in range(DIL_RATIO):
            store(1, DIL_MAX * r0 + DIL_MID * m + cls, q_rows, DIL_MAX, res,
                  rows=slice(m * q_rows, (m + 1) * q_rows))

        ln = pl.ds(pl.multiple_of(i * LANES, LANES), LANES)
        s = _attn_scores(q16[:, ln], k16[:, ln], bias16_ref[...])
        store(2, i, n_max, DIL_MAX, _attn_finish(s, v16[:, ln]))
        return carry

    lax.fori_loop(0, n_blocks, body, 0, unroll=8)

    def combine(i, carry):
        rows = pl.ds(pl.multiple_of(i * Q_BLOCK, Q_BLOCK), Q_BLOCK)
        ms = [max_ref[b, rows, :] for b in range(3)]
        mx = jnp.maximum(jnp.maximum(ms[0], ms[1]), ms[2])
        ws = [jnp.exp2(m - mx) for m in ms]
        num = ws[0] * acc_ref[0, rows, :] + ws[1] * acc_ref[1, rows, :] + ws[2] * acc_ref[2, rows, :]
        den = ws[0] * den_ref[0, rows, :] + ws[1] * den_ref[1, rows, :] + ws[2] * den_ref[2, rows, :]
        o_ref[rows, :] = (num / den).astype(o_ref.dtype)
        return carry

    lax.fori_loop(0, n_blocks, combine, 0)


def _attn_bias_tables():
    hw = ATTN_HALF_WINDOW
    row = np.arange(Q_BLOCK)[:, None]
    col = np.arange(K_WINDOW)[None, :]

    def bias(off):
        return np.where(np.abs(off) <= hw, 0.0, NEG_INF).astype(np.float32)

    nat = np.stack([bias(col - shift - row) for shift in (0, hw, 2 * hw)])
    q_rows, k_rows = Q_BLOCK // DIL_RATIO, K_WINDOW // DIL_RATIO
    qpos = DIL_RATIO * (row % q_rows) + row // q_rows
    kpos = DIL_RATIO * (col % k_rows) + col // k_rows
    mid = np.stack([bias(kpos - shift - qpos) for shift in (0, hw, 2 * hw)])
    wide = bias(np.arange(Q_BLOCK)[None, :] - row)
    return nat, mid, wide


def _attention(qkv, qkv16, *, batch, seq):
    n3, t, _ = qkv.shape
    n_pairs = n3 // 3
    n_max = seq // DIL_MAX
    assert DILATIONS[0] == 1 and n_max == Q_BLOCK and seq % (DIL_MID * Q_BLOCK) == 0
    nat, mid, wide = _attn_bias_tables()
    in_specs = ([pl.BlockSpec((None, seq, LANES), lambda b, hp, w=w: (w * n_pairs + hp, b, 0))
                 for w in range(3)]
                + [pl.BlockSpec((None, n_max, DIL_MAX * LANES), lambda b, hp, w=w: (w * n_pairs + hp, b, 0))
                   for w in range(3)]
                + [pl.BlockSpec(nat.shape, lambda b, hp: (0, 0, 0)),
                   pl.BlockSpec(mid.shape, lambda b, hp: (0, 0, 0)),
                   pl.BlockSpec(wide.shape, lambda b, hp: (0, 0))])
    scratch = [pltpu.VMEM((3, seq, LANES), F32) for _ in range(3)]
    return pl.pallas_call(
        functools.partial(_attn_kernel, seq=seq),
        grid=(batch, n_pairs),
        in_specs=in_specs,
        out_specs=pl.BlockSpec((seq, LANES), lambda b, hp: (b, hp)),
        out_shape=jax.ShapeDtypeStruct((t, n_pairs * LANES), BF16),
        scratch_shapes=scratch,
        compiler_params=_cparams(2),
        name="dilated_attn",
    )(qkv, qkv, qkv, qkv16, qkv16, qkv16, nat, mid, wide)


GATE_GROUP = 16
GATE_ROWS = 3 * GATE_GROUP


def _gateprep_kernel(g_ref, o_ref, orow_ref, *, n_heads):
    ch = MLSTM_CHUNK
    row = lax.broadcasted_iota(jnp.int32, (ch, ch), 0)
    col = lax.broadcasted_iota(jnp.int32, (ch, ch), 1)
    tri_prefix = (col <= row).astype(F32)
    tri_suffix = (col >= row).astype(F32)
    lane = lax.broadcasted_iota(jnp.int32, (ch, LANES), 1)
    srow = lax.broadcasted_iota(jnp.int32, (ch, LANES), 0)
    is_fwd = lane < 2 * n_heads

    for c in range(g_ref.shape[0] // ch):
        g = g_ref[c * ch:(c + 1) * ch, :]
        log_f = jnp.minimum(g, 0.0) - jnp.log(1.0 + jnp.exp(-jnp.abs(g)))
        pre = jnp.dot(tri_prefix, log_f, preferred_element_type=F32, precision=lax.Precision.HIGHEST)
        suf = jnp.dot(tri_suffix, log_f, preferred_element_type=F32, precision=lax.Precision.HIGHEST)
        b = jnp.where(is_fwd, pre, suf)
        a = pltpu.roll(g, n_heads, 1) - b
        cm = a
        step = 1
        while step < ch:
            ahead = jnp.where(srow >= step, pltpu.roll(cm, step, 0), -jnp.inf)
            behind = jnp.where(srow + step < ch, pltpu.roll(cm, ch - step, 0), -jnp.inf)
            cm = jnp.maximum(cm, jnp.where(is_fwd, ahead, behind))
            step *= 2
        out = jnp.where(lane < GATE_GROUP, b,
                        jnp.where(lane < 2 * GATE_GROUP, pltpu.roll(a, GATE_GROUP, 1),
                                  pltpu.roll(cm, 2 * GATE_GROUP, 1)))
        out = out * LOG2_E
        o_ref[c * ch:(c + 1) * ch, :] = out
        orow_ref[c] = out.T[:GATE_ROWS, :]


def _gateprep(gates, *, n_heads, tm=1024):
    t, w = gates.shape
    assert 4 * n_heads == GATE_GROUP and GATE_ROWS <= w
    cpt = tm // MLSTM_CHUNK
    return pl.pallas_call(
        functools.partial(_gateprep_kernel, n_heads=n_heads),
        grid=(t // tm,),
        in_specs=[pl.BlockSpec((tm, w), lambda i: (i, 0))],
        out_specs=[pl.BlockSpec((tm, w), lambda i: (i, 0)),
                   pl.BlockSpec((cpt, GATE_ROWS, MLSTM_CHUNK), lambda i: (i, 0, 0))],
        out_shape=[jax.ShapeDtypeStruct((t, w), F32),
                   jax.ShapeDtypeStruct((t // MLSTM_CHUNK, GATE_ROWS, MLSTM_CHUNK), F32)],
        compiler_params=_cparams(1),
        name="gateprep",
    )(gates)


CONV_HALO = 16


def _conv_silu_chunks(x_ref, w_ref, b_ref, stage_ref, emit):
    n = x_ref.shape[0]
    taps = w_ref.shape[0]
    pad = taps // 2
    zeros = jnp.zeros((CONV_HALO, x_ref.shape[1]), F32)
    stage_ref[0:CONV_HALO, :] = zeros
    stage_ref[CONV_HALO + n:, :] = zeros
    stage_ref[CONV_HALO:CONV_HALO + n, :] = x_ref[...].astype(F32)

    def body(c, carry):
        base = pl.multiple_of(c * MLSTM_CHUNK, MLSTM_CHUNK)
        acc = b_ref[...]
        for j in range(taps):
            acc = acc + stage_ref[pl.ds(base + (CONV_HALO + j - pad), MLSTM_CHUNK), :] * w_ref[j:j + 1, :]
        emit(c, acc * jax.nn.sigmoid(acc))
        return carry

    lax.fori_loop(0, n // MLSTM_CHUNK, body, 0, unroll=2)


def _mlstm_kernel(mq_ref, mk_ref, mv_ref, mo_ref, gc_ref, gr_ref, cwq_ref, cwk_ref, cbq_ref, cbk_ref,
                  hg_ref, o_ref, q_s, kt_s, vx_s, gc_s, sc_s, hf_s, hb_s, stage_s, *, n_heads):
    head = pl.program_id(1)
    seq, dh = q_s.shape
    ch = MLSTM_CHUNK
    n_chunks = seq // ch
    def emit_q(c, y):
        q_s[pl.ds(pl.multiple_of(c * ch, ch), ch), :] = y.astype(q_s.dtype)

    def emit_k(c, y):
        kt_s[:, pl.ds(pl.multiple_of(c * ch, ch), ch)] = (y * (dh ** -0.5)).T.astype(kt_s.dtype)

    _conv_silu_chunks(mq_ref, cwq_ref, cbq_ref, stage_s, emit_q)
    _conv_silu_chunks(mk_ref, cwk_ref, cbk_ref, stage_s, emit_k)
    vx_s[:, :dh] = mv_ref[...]
    vx_s[:, dh:] = jnp.ones((seq, dh), vx_s.dtype)
    gc_s[...] = pltpu.roll(gc_ref[...], (LANES - head) % LANES, 1)

    def gate_lane(group, d):
        return group * GATE_GROUP + (2 * d + 1) * n_heads

    for d in range(2):
        lane_b, lane_cm = gate_lane(0, d), gate_lane(2, d)
        m_run = jnp.zeros((1, 2 * dh), F32)
        for step in range(n_chunks):
            c = step if d == 0 else n_chunks - 1 - step
            last = c * ch + (ch - 1 if d == 0 else 0)
            b_last = jnp.broadcast_to(gc_s[last:last + 1, lane_b:lane_b + 1], (1, 2 * dh))
            a_max = jnp.broadcast_to(gc_s[last:last + 1, lane_cm:lane_cm + 1], (1, 2 * dh))
            w_max = jnp.maximum(m_run, a_max)
            sc_s[d, 0, c:c + 1, :] = m_run
            sc_s[d, 1, c:c + 1, :] = w_max
            sc_s[d, 2, c:c + 1, :] = jnp.exp2(m_run - w_max)
            m_run = b_last + w_max

    row = lax.broadcasted_iota(jnp.int32, (ch, ch), 0)
    col = lax.broadcasted_iota(jnp.int32, (ch, ch), 1)

    def direction(c, d, mask, state, h_store):
        lane_b, lane_cm = gate_lane(0, d), gate_lane(2, d)
        rs = pl.multiple_of(c * ch, ch)
        qc = q_s[pl.ds(rs, ch), :]
        ktc = kt_s[:, pl.ds(rs, ch)]
        vx = vx_s[pl.ds(rs, ch), :]
        a_row = gr_ref[c, pl.ds(gate_lane(1, d) + head, 1), :]
        m_row = sc_s[d, 0, pl.ds(c, 1), :]
        w_max = sc_s[d, 1, pl.ds(c, 1), :]
        decay = sc_s[d, 2, pl.ds(c, 1), :]
        g = gc_s[pl.ds(rs, ch), :]
        b_bc = jnp.broadcast_to(g[:, lane_b:lane_b + 1], (ch, dh))
        cm_bc = jnp.broadcast_to(g[:, lane_cm:lane_cm + 1], (ch, dh))
        mm = jnp.maximum(m_row[:, :dh], cm_bc)
        w = jnp.where(mask, jnp.exp2(a_row - mm), 0.0)
        s = jnp.dot(qc, ktc, preferred_element_type=F32)
        inter = jnp.dot(qc, state.astype(qc.dtype), preferred_element_type=F32)
        kw_t = (ktc.astype(F32) * jnp.exp2(a_row - w_max[:, :ch])).astype(vx.dtype)
        new_state = decay * state + jnp.dot(kw_t, vx, preferred_element_type=F32)
        intra = jnp.dot((s * w).astype(vx.dtype), vx, preferred_element_type=F32)
        g_int = jnp.exp2(m_row[:, :dh] - mm)
        num = intra[:, :dh] + g_int * inter[:, :dh]
        den = intra[:, dh:] + g_int * inter[:, dh:]
        h_store[pl.ds(rs, ch), :] = num / jnp.maximum(jnp.abs(den), jnp.exp2(-(b_bc + mm)))
        return new_state

    def step(j, carry):
        fwd, bwd = carry
        fwd = direction(j, 0, col <= row, fwd, hf_s)
        bwd = direction(n_chunks - 1 - j, 1, col >= row, bwd, hb_s)
        return fwd, bwd

    zero = jnp.zeros((dh, 2 * dh), F32)
    lax.fori_loop(0, n_chunks, step, (zero, zero), unroll=8)

    hm = hf_s[...] + hb_s[...]
    hm = hm * lax.rsqrt(jnp.mean(hm * hm, axis=-1, keepdims=True) + NORM_EPS)
    o_ref[...] = (hm * hg_ref[...] * jax.nn.sigmoid(mo_ref[...].astype(F32))).astype(o_ref.dtype)


def _mlstm(mall, gc, gr, conv_w, conv_b, head_g, *, batch, seq, n_heads):
    t = mall.shape[0]
    dh = mall.shape[1] // (4 * n_heads)
    taps = conv_w.shape[0]
    n_chunks = seq // MLSTM_CHUNK
    assert dh == MLSTM_CHUNK == LANES

    def col_block(offset):
        return pl.BlockSpec((seq, dh), lambda b, h, offset=offset: (b, offset * n_heads + h))

    in_specs = [col_block(0), col_block(1), col_block(2), col_block(3),
                pl.BlockSpec((seq, LANES), lambda b, h: (b, 0)),
                pl.BlockSpec((n_chunks, GATE_ROWS, MLSTM_CHUNK), lambda b, h: (b, 0, 0)),
                pl.BlockSpec((taps, dh), lambda b, h: (0, h)),
                pl.BlockSpec((taps, dh), lambda b, h: (0, n_heads + h)),
                pl.BlockSpec((1, dh), lambda b, h: (0, h)),
                pl.BlockSpec((1, dh), lambda b, h: (0, n_heads + h)),
                pl.BlockSpec((1, dh), lambda b, h: (0, h))]
    scratch = [pltpu.VMEM((seq, dh), BF16),
               pltpu.VMEM((dh, seq), BF16),
               pltpu.VMEM((seq, 2 * dh), BF16),
               pltpu.VMEM((seq, LANES), F32),
               pltpu.VMEM((2, 3, n_chunks, 2 * dh), F32),
               pltpu.VMEM((seq, dh), F32), pltpu.VMEM((seq, dh), F32),
               pltpu.VMEM((seq + 2 * CONV_HALO, dh), F32)]
    return pl.pallas_call(
        functools.partial(_mlstm_kernel, n_heads=n_heads),
        grid=(batch, n_heads),
        in_specs=in_specs,
        out_specs=pl.BlockSpec((seq, dh), lambda b, h: (b, h)),
        out_shape=jax.ShapeDtypeStruct((t, n_heads * dh), BF16),
        scratch_shapes=scratch,
        compiler_params=_cparams(2),
        name="mlstm",
    )(mall, mall, mall, mall, gc, gr, conv_w, conv_w, conv_b, conv_b, head_g)


def _outproj_kernel(a_ref, m_ref, w_ref, x_ref, gpost_ref, gpre_ref, x1_ref, hn_ref):
    ka = a_ref.shape[1]
    mixed = (jnp.dot(a_ref[...], w_ref[:ka, :], preferred_element_type=F32)
             + jnp.dot(m_ref[...], w_ref[ka:, :], preferred_element_type=F32))
    x1 = x_ref[...] + _rms(mixed, gpost_ref[...])
    x1_ref[...] = x1
    hn_ref[...] = _rms(x1, gpre_ref[...]).astype(hn_ref.dtype)


def _outproj(attn, ml, w_out, x, g_post, g_pre, tm=1024):
    t, d = x.shape
    ka, km = attn.shape[1], ml.shape[1]
    return pl.pallas_call(
        _outproj_kernel,
        grid=(t // tm,),
        in_specs=[pl.BlockSpec((tm, ka), lambda i: (i, 0)),
                  pl.BlockSpec((tm, km), lambda i: (i, 0)),
                  pl.BlockSpec((ka + km, d), lambda i: (0, 0)),
                  pl.BlockSpec((tm, d), lambda i: (i, 0)),
                  pl.BlockSpec((1, d), lambda i: (0, 0)),
                  pl.BlockSpec((1, d), lambda i: (0, 0))],
        out_specs=[pl.BlockSpec((tm, d), lambda i: (i, 0)),
                   pl.BlockSpec((tm, d), lambda i: (i, 0))],
        out_shape=[jax.ShapeDtypeStruct((t, d), F32), jax.ShapeDtypeStruct((t, d), BF16)],
        compiler_params=_cparams(1),
        name="outproj",
    )(attn, ml, w_out, x, g_post.reshape(1, d), g_pre.reshape(1, d))


FFN_HALO = 16
FFN_CHUNK = 256
FFN_STAGES = 4
FFN_SUBTILES = 1


def _ffn_kernel(prev_ref, main_ref, next_ref, x1_ref, wup_ref, wdn_ref, cw_ref, cb_ref,
                gpost_ref, x2_ref, *scratch, blocks_per_seq):
    ug_refs = scratch[:FFN_STAGES]
    uv_refs = scratch[FFN_STAGES:2 * FFN_STAGES]
    acc_refs = scratch[2 * FFN_STAGES:]
    i = pl.program_id(0)
    tm = main_ref.shape[0] // FFN_SUBTILES
    d_ff = wdn_ref.shape[0]
    pos = i % blocks_per_seq
    prev = jnp.where(pos == 0, jnp.zeros_like(prev_ref[...]), prev_ref[...])
    nxt = jnp.where(pos == blocks_per_seq - 1, jnp.zeros_like(next_ref[...]), next_ref[...])
    lhs = []
    for s in range(FFN_SUBTILES):
        before = prev if s == 0 else main_ref[s * tm - FFN_HALO:s * tm, :]
        after = nxt if s == FFN_SUBTILES - 1 else main_ref[(s + 1) * tm:(s + 1) * tm + FFN_HALO, :]
        lhs.append(jnp.concatenate([before, main_ref[s * tm:(s + 1) * tm, :], after], axis=0))
    taps = cw_ref.shape[0]

    row0 = pl.multiple_of(jnp.minimum(i, 0), FFN_HALO)

    def stage(u_ref, u):
        for k in range(FFN_CHUNK // LANES):
            u_ref[k] = u[:, k * LANES:(k + 1) * LANES]

    def conv(u_ref, k, c0):
        cols = slice(c0 + k * LANES, c0 + (k + 1) * LANES)
        out = cb_ref[:, cols]
        for j in range(taps):
            start = FFN_HALO + j - taps // 2
            out = out + u_ref[k, pl.ds(row0 + start, tm), :] * cw_ref[j:j + 1, cols]
        return out

    gelu_c0 = float(np.sqrt(2.0 / np.pi))
    gelu_c1 = float(np.sqrt(2.0 / np.pi) * 0.044715)
    n_chunks = d_ff // FFN_CHUNK
    n_total = FFN_SUBTILES * n_chunks

    def up_project(g):
        s, c0 = g // n_chunks, (g % n_chunks) * FFN_CHUNK
        stage(ug_refs[g % FFN_STAGES], jnp.dot(lhs[s], wup_ref[:, c0:c0 + FFN_CHUNK],
                                               preferred_element_type=F32))
        stage(uv_refs[g % FFN_STAGES], jnp.dot(lhs[s], wup_ref[:, d_ff + c0:d_ff + c0 + FFN_CHUNK],
                                               preferred_element_type=F32))

    for g in range(FFN_STAGES - 1):
        up_project(g)
    for g in range(n_total):
        s, c = g // n_chunks, g % n_chunks
        c0 = c * FFN_CHUNK
        slot = g % FFN_STAGES
        if g + FFN_STAGES - 1 < n_total:
            up_project(g + FFN_STAGES - 1)
        acts = []
        for k in range(FFN_CHUNK // LANES):
            gate = conv(ug_refs[slot], k, c0)
            val = conv(uv_refs[slot], k, d_ff + c0)
            half = (0.5 * gate) * val
            inner = gate * (gelu_c0 + gelu_c1 * (gate * gate))
            acts.append((half + half * jnp.tanh(inner)).astype(lhs[s].dtype))
        act = jnp.concatenate(acts, axis=1)
        part = jnp.dot(act, wdn_ref[c0:c0 + FFN_CHUNK, :], preferred_element_type=F32)
        if c == 0:
            acc_refs[s][...] = part
        else:
            acc_refs[s][...] += part
        if c == n_chunks - 1:
            rows = slice(s * tm, (s + 1) * tm)
            x2_ref[rows, :] = x1_ref[rows, :] + _rms(acc_refs[s][...], gpost_ref[...])


def _ffn(hn, x1, w_up, w_down, conv_w, conv_b, g_post, *, seq, tm=512):
    t, d = x1.shape
    d_ff = w_down.shape[0]
    rows = FFN_SUBTILES * tm
    blocks_per_seq = seq // rows
    halo_per_block = rows // FFN_HALO
    n_halo_blocks = t // FFN_HALO
    return pl.pallas_call(
        functools.partial(_ffn_kernel, blocks_per_seq=blocks_per_seq),
        grid=(t // rows,),
        in_specs=[pl.BlockSpec((FFN_HALO, d), lambda i: (jnp.maximum(i * halo_per_block - 1, 0), 0)),
                  pl.BlockSpec((rows, d), lambda i: (i, 0)),
                  pl.BlockSpec((FFN_HALO, d),
                               lambda i: (jnp.minimum((i + 1) * halo_per_block, n_halo_blocks - 1), 0)),
                  pl.BlockSpec((rows, d), lambda i: (i, 0)),
                  pl.BlockSpec((d, 2 * d_ff), lambda i: (0, 0), pipeline_mode=pl.Buffered(1)),
                  pl.BlockSpec((d_ff, d), lambda i: (0, 0), pipeline_mode=pl.Buffered(1)),
                  pl.BlockSpec(conv_w.shape, lambda i: (0, 0)),
                  pl.BlockSpec((1, 2 * d_ff), lambda i: (0, 0)),
                  pl.BlockSpec((1, d), lambda i: (0, 0))],
        out_specs=pl.BlockSpec((rows, d), lambda i: (i, 0)),
        out_shape=jax.ShapeDtypeStruct((t, d), F32),
        scratch_shapes=([pltpu.VMEM((FFN_CHUNK // LANES, tm + 2 * FFN_HALO, LANES), F32)
                         for _ in range(2 * FFN_STAGES)]
                        + [pltpu.VMEM((tm, d), F32) for _ in range(FFN_SUBTILES)]),
        compiler_params=_cparams(1),
        name="convffn",
    )(hn, hn, hn, x1, w_up, w_down, conv_w, conv_b.reshape(1, -1), g_post.reshape(1, d))


def _rotary_tables(seq):
    half = ATTN_HEAD_DIM // 2
    inv_freq = ROPE_THETA ** (-jnp.arange(0, ATTN_HEAD_DIM, 2, dtype=F32) / ATTN_HEAD_DIM)
    ang = jnp.arange(seq, dtype=F32)[:, None] * inv_freq[None, :]
    cos, sin = jnp.cos(ang), jnp.sin(ang)
    reps = LANES // ATTN_HEAD_DIM
    cos_t = jnp.tile(jnp.concatenate([cos, cos], axis=-1), (1, reps))
    sin_t = jnp.tile(jnp.concatenate([-sin, sin], axis=-1), (1, reps))
    return cos_t, sin_t


def kernel(x, mix_pre_g, w_in, mlstm_conv_w, mlstm_conv_b, mlstm_gate_b, mlstm_head_g, w_out,
           mix_post_g, ffn_pre_g, w_up, ffn_conv_w, ffn_conv_b, w_down, ffn_post_g):
    batch, seq, d = x.shape
    depth = w_in.shape[0]
    t = batch * seq
    n_heads = MLSTM_HEADS
    n_gates = mlstm_gate_b.shape[1]
    mix_w = mlstm_head_g.shape[1]
    attn_w = w_out.shape[1] - mix_w
    assert n_gates == 4 * n_heads and n_gates <= LANES
    assert w_in.shape[2] == 3 * attn_w + 4 * mix_w + n_gates

    cos_t, sin_t = _rotary_tables(seq)
    xf = x.reshape(t, d)
    for l in range(depth):
        gate_b = jnp.pad(mlstm_gate_b[l], (0, LANES - n_gates)).reshape(1, LANES)
        qkv, qkv16, mall, gates = _inproj(xf, mix_pre_g[l], w_in, l, gate_b, cos_t, sin_t,
                                          attn_w=attn_w, mix_w=mix_w, seq=seq)
        attn = _attention(qkv, qkv16, batch=batch, seq=seq)
        gc, gr = _gateprep(gates, n_heads=n_heads)
        ml = _mlstm(mall, gc, gr, mlstm_conv_w[l], mlstm_conv_b[l].reshape(1, -1),
                    mlstm_head_g[l].reshape(1, -1), batch=batch, seq=seq, n_heads=n_heads)
        x1, hn2 = _outproj(attn, ml, w_out[l].astype(BF16), xf, mix_post_g[l], ffn_pre_g[l])
        xf = _ffn(hn2, x1, w_up[l].astype(BF16), w_down[l].astype(BF16), ffn_conv_w[l],
                  ffn_conv_b[l], ffn_post_g[l], seq=seq)
    return xf.reshape(batch, seq, d)
```

```python
import functools

import numpy as np
import jax
import jax.numpy as jnp
from jax import lax
from jax.experimental import pallas as pl
from jax.experimental.pallas import tpu as pltpu

F32 = jnp.float32
BF16 = jnp.bfloat16

ATTN_HEAD_DIM = 64
ATTN_HALF_WINDOW = 64
DILATIONS = (1, 4, 16)
MLSTM_HEADS = 4
MLSTM_CHUNK = 128
ROPE_THETA = 10000.0
NORM_EPS = 1e-6
NEG_INF = -1e30

LANES = 128
Q_BLOCK = 128
K_WINDOW = Q_BLOCK + 2 * ATTN_HALF_WINDOW
DIL_MID, DIL_MAX = DILATIONS[1], DILATIONS[2]
DIL_RATIO = DIL_MAX // DIL_MID
LOG2_E = float(np.log2(np.e))
ATTN_Q_SCALE = float(ATTN_HEAD_DIM ** -0.5) * LOG2_E
VMEM_LIMIT_BYTES = 56 * 1024 * 1024


def _cparams(n_grid_dims):
    return pltpu.CompilerParams(
        dimension_semantics=("arbitrary",) * n_grid_dims,
        vmem_limit_bytes=VMEM_LIMIT_BYTES)


def _rms(x, g):
    return x * lax.rsqrt(jnp.mean(x * x, axis=-1, keepdims=True) + NORM_EPS) * g


def _inproj_kernel(x_ref, g_ref, w32_ref, gb_ref, cos_ref, sin_ref,
                   qkv_ref, qkv16_ref, mall_ref, gates_ref, w_ref, wg_ref, *stage_refs, attn_w, mix_w):
    n_main = w_ref.shape[1]

    @pl.when(pl.program_id(0) == 0)
    def _():
        w_ref[...] = w32_ref[:, :n_main].astype(w_ref.dtype)
        wg_ref[...] = jnp.zeros(wg_ref.shape, wg_ref.dtype)
        n_gates = w32_ref.shape[1] - n_main
        wg_ref[:, :n_gates] = w32_ref[:, n_main:].astype(wg_ref.dtype)

    hn = _rms(x_ref[...], g_ref[...]).astype(w_ref.dtype)
    cos = cos_ref[...]
    sin = sin_ref[...]
    tm = hn.shape[0]
    lane = lax.broadcasted_iota(jnp.int32, (tm, LANES), 1)
    first_half = (lane % ATTN_HEAD_DIM) < (ATTN_HEAD_DIM // 2)
    n_pairs = attn_w // LANES
    base = 3 * attn_w

    def mix_group(grp):
        res = jnp.dot(hn, w_ref[:, base + grp * mix_w: base + (grp + 1) * mix_w],
                      preferred_element_type=F32)
        mall_ref[:, grp * mix_w:(grp + 1) * mix_w] = res.astype(mall_ref.dtype)

    for grp in range(3):
        res = jnp.dot(hn, w_ref[:, grp * attn_w:(grp + 1) * attn_w],
                      preferred_element_type=F32)
        mix_group(grp)
        for hp in range(n_pairs):
            xs = res[:, hp * LANES:(hp + 1) * LANES]
            if grp < 2:
                rot = jnp.where(first_half,
                                pltpu.roll(xs, LANES - ATTN_HEAD_DIM // 2, 1),
                                pltpu.roll(xs, ATTN_HEAD_DIM // 2, 1))
                xs = xs * cos + rot * sin
            if grp == 0:
                xs = xs * ATTN_Q_SCALE
            slab = grp * n_pairs + hp
            qkv_ref[slab] = xs.astype(qkv_ref.dtype)
            stage_ref = stage_refs[slab]
            stage_ref[...] = xs
            for r in range(DIL_MAX):
                qkv16_ref[slab, :, r * LANES:(r + 1) * LANES] = (
                    stage_ref[pl.ds(r, tm // DIL_MAX, stride=DIL_MAX), :].astype(qkv16_ref.dtype))
    mix_group(3)
    gates_ref[...] = jnp.dot(hn, wg_ref[...], preferred_element_type=F32) + gb_ref[...]


def _inproj(x, g_pre, w_in, layer, gate_b, cos, sin, *, attn_w, mix_w, seq, tm=512):
    t, d = x.shape
    n_pairs = attn_w // LANES
    n_cols = w_in.shape[2]
    n_main = 3 * attn_w + 4 * mix_w
    tiles_per_seq = seq // tm
    kern = functools.partial(_inproj_kernel, attn_w=attn_w, mix_w=mix_w)
    return pl.pallas_call(
        kern,
        grid=(t // tm,),
        in_specs=[pl.BlockSpec((tm, d), lambda i: (i, 0)),
                  pl.BlockSpec((1, d), lambda i: (0, 0)),
                  pl.BlockSpec((None, d, n_cols), lambda i: (layer, 0, 0), pipeline_mode=pl.Buffered(1)),
                  pl.BlockSpec((1, LANES), lambda i: (0, 0)),
                  pl.BlockSpec((tm, LANES), lambda i: (i % tiles_per_seq, 0)),
                  pl.BlockSpec((tm, LANES), lambda i: (i % tiles_per_seq, 0))],
        out_specs=[pl.BlockSpec((3 * n_pairs, tm, LANES), lambda i: (0, i, 0)),
                   pl.BlockSpec((3 * n_pairs, tm // DIL_MAX, DIL_MAX * LANES), lambda i: (0, i, 0)),
                   pl.BlockSpec((tm, 4 * mix_w), lambda i: (i, 0)),
                   pl.BlockSpec((tm, LANES), lambda i: (i, 0))],
        out_shape=[jax.ShapeDtypeStruct((3 * n_pairs, t, LANES), BF16),
                   jax.ShapeDtypeStruct((3 * n_pairs, t // DIL_MAX, DIL_MAX * LANES), BF16),
                   jax.ShapeDtypeStruct((t, 4 * mix_w), BF16),
                   jax.ShapeDtypeStruct((t, LANES), F32)],
        scratch_shapes=([pltpu.VMEM((d, n_main), BF16), pltpu.VMEM((d, LANES), BF16)]
                        + [pltpu.VMEM((tm, LANES), F32) for _ in range(3 * n_pairs)]),
        compiler_params=_cparams(1),
        name="inproj",
    )(x, g_pre.reshape(1, d), w_in, gate_b, cos, sin)


def _head_masks(rows, dtype):
    lane = lax.broadcasted_iota(jnp.int32, (rows, LANES), 1)
    h0 = jnp.where(lane < ATTN_HEAD_DIM, 1.0, 0.0).astype(dtype)
    return h0, (1.0 - h0.astype(F32)).astype(dtype)


def _attn_scores(q, k, bias):
    q0, q1 = _head_masks(Q_BLOCK, q.dtype)
    q_st = jnp.concatenate([q * q0, q * q1], axis=0)
    s = lax.dot_general(q_st, k, (((1,), (1,)), ((), ())), preferred_element_type=F32)
    return s + jnp.concatenate([bias, bias], axis=0)


def _attn_finish(s, v):
    v0, v1 = _head_masks(v.shape[0], v.dtype)
    m = jnp.max(s, axis=-1, keepdims=True)
    p = jnp.exp2(s - m).astype(v.dtype)
    p_cat = jnp.concatenate([p[:Q_BLOCK], p[Q_BLOCK:]], axis=1)
    v_ext = jnp.concatenate([jnp.concatenate([v * v0, v0], axis=1),
                             jnp.concatenate([v * v1, v1], axis=1)], axis=0)
    o = jnp.dot(p_cat, v_ext, preferred_element_type=F32)
    lane_o = lax.broadcasted_iota(jnp.int32, (Q_BLOCK, LANES), 1)
    m_lanes = jnp.where(lane_o < ATTN_HEAD_DIM, m[:Q_BLOCK], m[Q_BLOCK:])
    return o[:, :LANES], o[:, LANES:], m_lanes


def _attn_kernel(qn, kn, vn, q16, k16, v16, bias1_ref, bias4_ref, bias16_ref, o_ref,
                 acc_ref, den_ref, max_ref, *, seq):
    n_blocks = seq // Q_BLOCK
    n_mid = (seq // DIL_MID) // Q_BLOCK
    n_max = seq // DIL_MAX
    q_rows = Q_BLOCK // DIL_RATIO
    k_rows = K_WINDOW // DIL_RATIO

    def edge_table(blk, n):
        return jnp.where(blk > 0, 1, 0) + jnp.where(blk == n - 1, 1, 0)

    def nat_window(i):
        qs = pl.multiple_of(i * Q_BLOCK, Q_BLOCK)
        ks = pl.multiple_of(jnp.clip(qs - ATTN_HALF_WINDOW, 0, seq - K_WINDOW), ATTN_HALF_WINDOW)
        return qs, ks

    def mid_window(i):
        cls, blk = i // n_mid, i % n_mid
        r0 = pl.multiple_of(blk * q_rows, q_rows)
        k0 = pl.multiple_of(jnp.clip(r0 - ATTN_HALF_WINDOW // DIL_RATIO, 0, n_max - k_rows),
                            ATTN_HALF_WINDOW // DIL_RATIO)
        lanes = [pl.ds(pl.multiple_of((cls + DIL_MID * m) * LANES, LANES), LANES) for m in range(DIL_RATIO)]
        return cls, blk, r0, k0, lanes

    def gather(ref, r0, rows, lanes):
        return jnp.concatenate([ref[pl.ds(r0, rows), ln] for ln in lanes], axis=0)

    def store(branch, start, size, stride, vals, rows=slice(None)):
        for ref, val in zip((acc_ref, den_ref, max_ref), vals):
            idx = pl.ds(start, size) if stride == 1 else pl.ds(start, size, stride=stride)
            ref[branch, idx, :] = val[rows]

    def body(i, carry):
        qs, ks = nat_window(i)
        s = _attn_scores(qn[pl.ds(qs, Q_BLOCK), :], kn[pl.ds(ks, K_WINDOW), :],
                         bias1_ref[edge_table(i, n_blocks)])
        store(0, qs, Q_BLOCK, 1, _attn_finish(s, vn[pl.ds(ks, K_WINDOW), :]))

        cls, blk, r0, k0, lanes = mid_window(i)
        s = _attn_scores(gather(q16, r0, q_rows, lanes), gather(k16, k0, k_rows, lanes),
                         bias4_ref[edge_table(blk, n_mid)])
        res = _attn_finish(s, gather(v16, k0, k_rows, lanes))
        for m in range(DIL_RATIO):
            store(1, DIL_MAX * r0 + DIL_MID * m + cls, q_rows, DIL_MAX, res,
                  rows=slice(m * q_rows, (m + 1) * q_rows))

        ln = pl.ds(pl.multiple_of(i * LANES, LANES), LANES)
        s = _attn_scores(q16[:, ln], k16[:, ln], bias16_ref[...])
        store(2, i, n_max, DIL_MAX, _attn_finish(s, v16[:, ln]))
        return carry

    lax.fori_loop(0, n_blocks, body, 0, unroll=8)

    def combine(i, carry):
        rows = pl.ds(pl.multiple_of(i * Q_BLOCK, Q_BLOCK), Q_BLOCK)
        ms = [max_ref[b, rows, :] for b in range(3)]
        mx = jnp.maximum(jnp.maximum(ms[0], ms[1]), ms[2])
        ws = [jnp.exp2(m - mx) for m in ms]
        num = ws[0] * acc_ref[0, rows, :] + ws[1] * acc_ref[1, rows, :] + ws[2] * acc_ref[2, rows, :]
        den = ws[0] * den_ref[0, rows, :] + ws[1] * den_ref[1, rows, :] + ws[2] * den_ref[2, rows, :]
        o_ref[rows, :] = (num / den).astype(o_ref.dtype)
        return carry

    lax.fori_loop(0, n_blocks, combine, 0)


def _attn_bias_tables():
    hw = ATTN_HALF_WINDOW
    row = np.arange(Q_BLOCK)[:, None]
    col = np.arange(K_WINDOW)[None, :]

    def bias(off):
        return np.where(np.abs(off) <= hw, 0.0, NEG_INF).astype(np.float32)

    nat = np.stack([bias(col - shift - row) for shift in (0, hw, 2 * hw)])
    q_rows, k_rows = Q_BLOCK // DIL_RATIO, K_WINDOW // DIL_RATIO
    qpos = DIL_RATIO * (row % q_rows) + row // q_rows
    kpos = DIL_RATIO * (col % k_rows) + col // k_rows
    mid = np.stack([bias(kpos - shift - qpos) for shift in (0, hw, 2 * hw)])
    wide = bias(np.arange(Q_BLOCK)[None, :] - row)
    return nat, mid, wide


def _attention(qkv, qkv16, *, batch, seq):
    n3, t, _ = qkv.shape
    n_pairs = n3 // 3
    n_max = seq // DIL_MAX
    assert DILATIONS[0] == 1 and n_max == Q_BLOCK and seq % (DIL_MID * Q_BLOCK) == 0
    nat, mid, wide = _attn_bias_tables()
    in_specs = ([pl.BlockSpec((None, seq, LANES), lambda b, hp, w=w: (w * n_pairs + hp, b, 0))
                 for w in range(3)]
                + [pl.BlockSpec((None, n_max, DIL_MAX * LANES), lambda b, hp, w=w: (w * n_pairs + hp, b, 0))
                   for w in range(3)]
                + [pl.BlockSpec(nat.shape, lambda b, hp: (0, 0, 0)),
                   pl.BlockSpec(mid.shape, lambda b, hp: (0, 0, 0)),
                   pl.BlockSpec(wide.shape, lambda b, hp: (0, 0))])
    scratch = [pltpu.VMEM((3, seq, LANES), F32) for _ in range(3)]
    return pl.pallas_call(
        functools.partial(_attn_kernel, seq=seq),
        grid=(batch, n_pairs),
        in_specs=in_specs,
        out_specs=pl.BlockSpec((seq, LANES), lambda b, hp: (b, hp)),
        out_shape=jax.ShapeDtypeStruct((t, n_pairs * LANES), BF16),
        scratch_shapes=scratch,
        compiler_params=_cparams(2),
        name="dilated_attn",
    )(qkv, qkv, qkv, qkv16, qkv16, qkv16, nat, mid, wide)


GATE_GROUP = 16
GATE_ROWS = 3 * GATE_GROUP


def _gateprep_kernel(g_ref, o_ref, orow_ref, *, n_heads):
    ch = MLSTM_CHUNK
    src = lax.broadcasted_iota(jnp.int32, (ch, ch), 0)
    dst = lax.broadcasted_iota(jnp.int32, (ch, ch), 1)
    tri_prefix = (src <= dst).astype(F32)
    tri_suffix = (src >= dst).astype(F32)
    n_c = g_ref.shape[0] // ch
    grow = lax.broadcasted_iota(jnp.int32, (n_c * GATE_GROUP, ch), 0)
    pos = lax.broadcasted_iota(jnp.int32, (n_c * GATE_GROUP, ch), 1)
    is_fwd = (grow % GATE_GROUP) < 2 * n_heads
    pad = jnp.zeros((LANES - GATE_ROWS, ch), F32)

    g = jnp.concatenate([g_ref[c * ch:(c + 1) * ch, :].T[:GATE_GROUP, :] for c in range(n_c)], axis=0)
    log_f = jnp.minimum(g, 0.0) - jnp.log(1.0 + jnp.exp(-jnp.abs(g)))
    pre = jnp.dot(log_f, tri_prefix, preferred_element_type=F32, precision=lax.Precision.HIGHEST)
    suf = jnp.dot(log_f, tri_suffix, preferred_element_type=F32, precision=lax.Precision.HIGHEST)
    b = jnp.where(is_fwd, pre, suf)
    a = pltpu.roll(g, n_heads, 0) - b
    cm = a
    step = 1
    while step < ch:
        ahead = jnp.where(pos >= step, pltpu.roll(cm, step, 1), -jnp.inf)
        behind = jnp.where(pos + step < ch, pltpu.roll(cm, ch - step, 1), -jnp.inf)
        cm = jnp.maximum(cm, jnp.where(is_fwd, ahead, behind))
        step *= 2
    for c in range(n_c):
        part = slice(c * GATE_GROUP, (c + 1) * GATE_GROUP)
        rows = jnp.concatenate([b[part], a[part], cm[part]], axis=0) * LOG2_E
        orow_ref[c] = rows
        o_ref[c * ch:(c + 1) * ch, :] = jnp.concatenate([rows, pad], axis=0).T


def _gateprep(gates, *, n_heads, tm=2048):
    t, w = gates.shape
    assert 4 * n_heads == GATE_GROUP and GATE_ROWS <= w
    cpt = tm // MLSTM_CHUNK
    return pl.pallas_call(
        functools.partial(_gateprep_kernel, n_heads=n_heads),
        grid=(t // tm,),
        in_specs=[pl.BlockSpec((tm, w), lambda i: (i, 0))],
        out_specs=[pl.BlockSpec((tm, w), lambda i: (i, 0)),
                   pl.BlockSpec((cpt, GATE_ROWS, MLSTM_CHUNK), lambda i: (i, 0, 0))],
        out_shape=[jax.ShapeDtypeStruct((t, w), F32),
                   jax.ShapeDtypeStruct((t // MLSTM_CHUNK, GATE_ROWS, MLSTM_CHUNK), F32)],
        compiler_params=_cparams(1),
        name="gateprep",
    )(gates)


CONV_HALO = 16


def _conv_silu_chunks(x_ref, w_ref, b_ref, stage_ref, emit):
    n = x_ref.shape[0]
    taps = w_ref.shape[0]
    pad = taps // 2
    zeros = jnp.zeros((CONV_HALO, x_ref.shape[1]), F32)
    stage_ref[0:CONV_HALO, :] = zeros
    stage_ref[CONV_HALO + n:, :] = zeros
    stage_ref[CONV_HALO:CONV_HALO + n, :] = x_ref[...].astype(F32)

    def body(c, carry):
        base = pl.multiple_of(c * MLSTM_CHUNK, MLSTM_CHUNK)
        acc = b_ref[...]
        for j in range(taps):
            acc = acc + stage_ref[pl.ds(base + (CONV_HALO + j - pad), MLSTM_CHUNK), :] * w_ref[j:j + 1, :]
        emit(c, acc * jax.nn.sigmoid(acc))
        return carry

    lax.fori_loop(0, n // MLSTM_CHUNK, body, 0, unroll=2)


def _mlstm_kernel(mq_ref, mk_ref, mv_ref, mo_ref, gc_ref, gr_ref, cwq_ref, cwk_ref, cbq_ref, cbk_ref,
                  hg_ref, o_ref, q_s, kt_s, vx_s, gc_s, sc_s, hf_s, hb_s, stage_s, *, n_heads):
    head = pl.program_id(1)
    seq, dh = q_s.shape
    ch = MLSTM_CHUNK
    n_chunks = seq // ch
    def emit_q(c, y):
        q_s[pl.ds(pl.multiple_of(c * ch, ch), ch), :] = y.astype(q_s.dtype)

    def emit_k(c, y):
        kt_s[:, pl.ds(pl.multiple_of(c * ch, ch), ch)] = (y * (dh ** -0.5)).T.astype(kt_s.dtype)

    _conv_silu_chunks(mq_ref, cwq_ref, cbq_ref, stage_s, emit_q)
    _conv_silu_chunks(mk_ref, cwk_ref, cbk_ref, stage_s, emit_k)
    vx_s[:, :dh] = mv_ref[...]
    vx_s[:, dh:] = jnp.ones((seq, dh), vx_s.dtype)
    gc_s[...] = pltpu.roll(gc_ref[...], (LANES - head) % LANES, 1)

    def gate_lane(group, d):
        return group * GATE_GROUP + (2 * d + 1) * n_heads

    for d in range(2):
        lane_b, lane_cm = gate_lane(0, d), gate_lane(2, d)
        m_run = jnp.zeros((1, 2 * dh), F32)
        for step in range(n_chunks):
            c = step if d == 0 else n_chunks - 1 - step
            last = c * ch + (ch - 1 if d == 0 else 0)
            b_last = jnp.broadcast_to(gc_s[last:last + 1, lane_b:lane_b + 1], (1, 2 * dh))
            a_max = jnp.broadcast_to(gc_s[last:last + 1, lane_cm:lane_cm + 1], (1, 2 * dh))
            w_max = jnp.maximum(m_run, a_max)
            sc_s[d, 0, c:c + 1, :] = m_run
            sc_s[d, 1, c:c + 1, :] = w_max
            sc_s[d, 2, c:c + 1, :] = jnp.exp2(m_run - w_max)
            m_run = b_last + w_max

    row = lax.broadcasted_iota(jnp.int32, (ch, ch), 0)
    col = lax.broadcasted_iota(jnp.int32, (ch, ch), 1)

    def direction(c, d, mask, state, h_store):
        lane_b, lane_cm = gate_lane(0, d), gate_lane(2, d)
        rs = pl.multiple_of(c * ch, ch)
        qc = q_s[pl.ds(rs, ch), :]
        ktc = kt_s[:, pl.ds(rs, ch)]
        vx = vx_s[pl.ds(rs, ch), :]
        a_row = gr_ref[c, pl.ds(gate_lane(1, d) + head, 1), :]
        m_row = sc_s[d, 0, pl.ds(c, 1), :]
        w_max = sc_s[d, 1, pl.ds(c, 1), :]
        decay = sc_s[d, 2, pl.ds(c, 1), :]
        g = gc_s[pl.ds(rs, ch), :]
        b_bc = jnp.broadcast_to(g[:, lane_b:lane_b + 1], (ch, dh))
        cm_bc = jnp.broadcast_to(g[:, lane_cm:lane_cm + 1], (ch, dh))
        mm = jnp.maximum(m_row[:, :dh], cm_bc)
        w = jnp.where(mask, jnp.exp2(a_row - mm), 0.0)
        s = jnp.dot(qc, ktc, preferred_element_type=F32)
        inter = jnp.dot(qc, state.astype(qc.dtype), preferred_element_type=F32)
        kw_t = (ktc.astype(F32) * jnp.exp2(a_row - w_max[:, :ch])).astype(vx.dtype)
        new_state = decay * state + jnp.dot(kw_t, vx, preferred_element_type=F32)
        intra = jnp.dot((s * w).astype(vx.dtype), vx, preferred_element_type=F32)
        g_int = jnp.exp2(m_row[:, :dh] - mm)
        num = intra[:, :dh] + g_int * inter[:, :dh]
        den = intra[:, dh:] + g_int * inter[:, dh:]
        h_store[pl.ds(rs, ch), :] = num / jnp.maximum(jnp.abs(den), jnp.exp2(-(b_bc + mm)))
        return new_state

    def step(j, carry):
        fwd, bwd = carry
        fwd = direction(j, 0, col <= row, fwd, hf_s)
        bwd = direction(n_chunks - 1 - j, 1, col >= row, bwd, hb_s)
        return fwd, bwd

    zero = jnp.zeros((dh, 2 * dh), F32)
    lax.fori_loop(0, n_chunks, step, (zero, zero), unroll=8)

    hm = hf_s[...] + hb_s[...]
    hm = hm * lax.rsqrt(jnp.mean(hm * hm, axis=-1, keepdims=True) + NORM_EPS)
    o_ref[...] = (hm * hg_ref[...] * jax.nn.sigmoid(mo_ref[...].astype(F32))).astype(o_ref.dtype)


def _mlstm(mall, gc, gr, conv_w, conv_b, head_g, *, batch, seq, n_heads):
    t = mall.shape[0]
    dh = mall.shape[1] // (4 * n_heads)
    taps = conv_w.shape[0]
    n_chunks = seq // MLSTM_CHUNK
    assert dh == MLSTM_CHUNK == LANES

    def col_block(offset):
        return pl.BlockSpec((seq, dh), lambda b, h, offset=offset: (b, offset * n_heads + h))

    in_specs = [col_block(0), col_block(1), col_block(2), col_block(3),
                pl.BlockSpec((seq, LANES), lambda b, h: (b, 0)),
                pl.BlockSpec((n_chunks, GATE_ROWS, MLSTM_CHUNK), lambda b, h: (b, 0, 0)),
                pl.BlockSpec((taps, dh), lambda b, h: (0, h)),
                pl.BlockSpec((taps, dh), lambda b, h: (0, n_heads + h)),
                pl.BlockSpec((1, dh), lambda b, h: (0, h)),
                pl.BlockSpec((1, dh), lambda b, h: (0, n_heads + h)),
                pl.BlockSpec((1, dh), lambda b, h: (0, h))]
    scratch = [pltpu.VMEM((seq, dh), BF16),
               pltpu.VMEM((dh, seq), BF16),
               pltpu.VMEM((seq, 2 * dh), BF16),
               pltpu.VMEM((seq, LANES), F32),
               pltpu.VMEM((2, 3, n_chunks, 2 * dh), F32),
               pltpu.VMEM((seq, dh), F32), pltpu.VMEM((seq, dh), F32),
               pltpu.VMEM((seq + 2 * CONV_HALO, dh), F32)]
    return pl.pallas_call(
        functools.partial(_mlstm_kernel, n_heads=n_heads),
        grid=(batch, n_heads),
        in_specs=in_specs,
        out_specs=pl.BlockSpec((seq, dh), lambda b, h: (b, h)),
        out_shape=jax.ShapeDtypeStruct((t, n_heads * dh), BF16),
        scratch_shapes=scratch,
        compiler_params=_cparams(2),
        name="mlstm",
    )(mall, mall, mall, mall, gc, gr, conv_w, conv_w, conv_b, conv_b, head_g)


def _outproj_kernel(a_ref, m_ref, w_ref, x_ref, gpost_ref, gpre_ref, x1_ref, hn_ref):
    ka = a_ref.shape[1]
    mixed = (jnp.dot(a_ref[...], w_ref[:ka, :], preferred_element_type=F32)
             + jnp.dot(m_ref[...], w_ref[ka:, :], preferred_element_type=F32))
    x1 = x_ref[...] + _rms(mixed, gpost_ref[...])
    x1_ref[...] = x1
    hn_ref[...] = _rms(x1, gpre_ref[...]).astype(hn_ref.dtype)


def _outproj(attn, ml, w_out, layer, x, g_post, g_pre, tm=1024):
    t, d = x.shape
    ka, km = attn.shape[1], ml.shape[1]
    return pl.pallas_call(
        _outproj_kernel,
        grid=(t // tm,),
        in_specs=[pl.BlockSpec((tm, ka), lambda i: (i, 0)),
                  pl.BlockSpec((tm, km), lambda i: (i, 0)),
                  pl.BlockSpec((None, ka + km, d), lambda i: (layer, 0, 0)),
                  pl.BlockSpec((tm, d), lambda i: (i, 0)),
                  pl.BlockSpec((1, d), lambda i: (0, 0)),
                  pl.BlockSpec((1, d), lambda i: (0, 0))],
        out_specs=[pl.BlockSpec((tm, d), lambda i: (i, 0)),
                   pl.BlockSpec((tm, d), lambda i: (i, 0))],
        out_shape=[jax.ShapeDtypeStruct((t, d), F32), jax.ShapeDtypeStruct((t, d), BF16)],
        compiler_params=_cparams(1),
        name="outproj",
    )(attn, ml, w_out, x, g_post.reshape(1, d), g_pre.reshape(1, d))


FFN_HALO = 16
FFN_CHUNK = 256
FFN_STAGES = 4
FFN_SUBTILES = 1


def _ffn_kernel(prev_ref, main_ref, next_ref, x1_ref, wup_ref, wdn_ref, cw_ref, cb_ref,
                gpost_ref, x2_ref, *scratch, blocks_per_seq):
    ug_refs = scratch[:FFN_STAGES]
    uv_refs = scratch[FFN_STAGES:2 * FFN_STAGES]
    acc_refs = scratch[2 * FFN_STAGES:]
    i = pl.program_id(0)
    tm = main_ref.shape[0] // FFN_SUBTILES
    d_ff = wdn_ref.shape[0]
    pos = i % blocks_per_seq
    prev = jnp.where(pos == 0, jnp.zeros_like(prev_ref[...]), prev_ref[...])
    nxt = jnp.where(pos == blocks_per_seq - 1, jnp.zeros_like(next_ref[...]), next_ref[...])
    lhs = []
    for s in range(FFN_SUBTILES):
        before = prev if s == 0 else main_ref[s * tm - FFN_HALO:s * tm, :]
        after = nxt if s == FFN_SUBTILES - 1 else main_ref[(s + 1) * tm:(s + 1) * tm + FFN_HALO, :]
        lhs.append(jnp.concatenate([before, main_ref[s * tm:(s + 1) * tm, :], after], axis=0))
    taps = cw_ref.shape[0]

    row0 = pl.multiple_of(jnp.minimum(i, 0), FFN_HALO)

    def stage(u_ref, u):
        for k in range(FFN_CHUNK // LANES):
            u_ref[k] = u[:, k * LANES:(k + 1) * LANES]

    def conv(u_ref, k, c0):
        cols = slice(c0 + k * LANES, c0 + (k + 1) * LANES)
        out = cb_ref[:, cols]
        for j in range(taps):
            start = FFN_HALO + j - taps // 2
            out = out + u_ref[k, pl.ds(row0 + start, tm), :] * cw_ref[j:j + 1, cols]
        return out

    gelu_c0 = float(np.sqrt(2.0 / np.pi))
    gelu_c1 = float(np.sqrt(2.0 / np.pi) * 0.044715)
    n_chunks = d_ff // FFN_CHUNK
    n_total = FFN_SUBTILES * n_chunks

    def up_project(g):
        s, c0 = g // n_chunks, (g % n_chunks) * FFN_CHUNK
        stage(ug_refs[g % FFN_STAGES], jnp.dot(lhs[s], wup_ref[:, c0:c0 + FFN_CHUNK],
                                               preferred_element_type=F32))
        stage(uv_refs[g % FFN_STAGES], jnp.dot(lhs[s], wup_ref[:, d_ff + c0:d_ff + c0 + FFN_CHUNK],
                                               preferred_element_type=F32))

    for g in range(FFN_STAGES - 1):
        up_project(g)
    for g in range(n_total):
        s, c = g // n_chunks, g % n_chunks
        c0 = c * FFN_CHUNK
        slot = g % FFN_STAGES
        if g + FFN_STAGES - 1 < n_total:
            up_project(g + FFN_STAGES - 1)
        acts = []
        for k in range(FFN_CHUNK // LANES):
            gate = conv(ug_refs[slot], k, c0)
            val = conv(uv_refs[slot], k, d_ff + c0)
            half = (0.5 * gate) * val
            inner = gate * (gelu_c0 + gelu_c1 * (gate * gate))
            acts.append((half + half * jnp.tanh(inner)).astype(lhs[s].dtype))
        act = jnp.concatenate(acts, axis=1)
        part = jnp.dot(act, wdn_ref[c0:c0 + FFN_CHUNK, :], preferred_element_type=F32)
        if c == 0:
            acc_refs[s][...] = part
        else:
            acc_refs[s][...] += part
        if c == n_chunks - 1:
            rows = slice(s * tm, (s + 1) * tm)
            x2_ref[rows, :] = x1_ref[rows, :] + _rms(acc_refs[s][...], gpost_ref[...])


def _ffn(hn, x1, w_up, w_down, layer, conv_w, conv_b, g_post, *, seq, tm=512):
    t, d = x1.shape
    d_ff = w_down.shape[1]
    rows = FFN_SUBTILES * tm
    blocks_per_seq = seq // rows
    halo_per_block = rows // FFN_HALO
    n_halo_blocks = t // FFN_HALO
    return pl.pallas_call(
        functools.partial(_ffn_kernel, blocks_per_seq=blocks_per_seq),
        grid=(t // rows,),
        in_specs=[pl.BlockSpec((FFN_HALO, d), lambda i: (jnp.maximum(i * halo_per_block - 1, 0), 0)),
                  pl.BlockSpec((rows, d), lambda i: (i, 0)),
                  pl.BlockSpec((FFN_HALO, d),
                               lambda i: (jnp.minimum((i + 1) * halo_per_block, n_halo_blocks - 1), 0)),
                  pl.BlockSpec((rows, d), lambda i: (i, 0)),
                  pl.BlockSpec((None, d, 2 * d_ff), lambda i: (layer, 0, 0), pipeline_mode=pl.Buffered(1)),
                  pl.BlockSpec((None, d_ff, d), lambda i: (layer, 0, 0), pipeline_mode=pl.Buffered(1)),
                  pl.BlockSpec(conv_w.shape, lambda i: (0, 0)),
                  pl.BlockSpec((1, 2 * d_ff), lambda i: (0, 0)),
                  pl.BlockSpec((1, d), lambda i: (0, 0))],
        out_specs=pl.BlockSpec((rows, d), lambda i: (i, 0)),
        out_shape=jax.ShapeDtypeStruct((t, d), F32),
        scratch_shapes=([pltpu.VMEM((FFN_CHUNK // LANES, tm + 2 * FFN_HALO, LANES), F32)
                         for _ in range(2 * FFN_STAGES)]
                        + [pltpu.VMEM((tm, d), F32) for _ in range(FFN_SUBTILES)]),
        compiler_params=_cparams(1),
        name="convffn",
    )(hn, hn, hn, x1, w_up, w_down, conv_w, conv_b.reshape(1, -1), g_post.reshape(1, d))


def _rotary_tables(seq):
    half = ATTN_HEAD_DIM // 2
    inv_freq = ROPE_THETA ** (-jnp.arange(0, ATTN_HEAD_DIM, 2, dtype=F32) / ATTN_HEAD_DIM)
    ang = jnp.arange(seq, dtype=F32)[:, None] * inv_freq[None, :]
    cos, sin = jnp.cos(ang), jnp.sin(ang)
    reps = LANES // ATTN_HEAD_DIM
    cos_t = jnp.tile(jnp.concatenate([cos, cos], axis=-1), (1, reps))
    sin_t = jnp.tile(jnp.concatenate([-sin, sin], axis=-1), (1, reps))
    return cos_t, sin_t


def kernel(x, mix_pre_g, w_in, mlstm_conv_w, mlstm_conv_b, mlstm_gate_b, mlstm_head_g, w_out,
           mix_post_g, ffn_pre_g, w_up, ffn_conv_w, ffn_conv_b, w_down, ffn_post_g):
    batch, seq, d = x.shape
    depth = w_in.shape[0]
    t = batch * seq
    n_heads = MLSTM_HEADS
    n_gates = mlstm_gate_b.shape[1]
    mix_w = mlstm_head_g.shape[1]
    attn_w = w_out.shape[1] - mix_w
    assert n_gates == 4 * n_heads and n_gates <= LANES
    assert w_in.shape[2] == 3 * attn_w + 4 * mix_w + n_gates

    cos_t, sin_t = _rotary_tables(seq)
    xf = x.reshape(t, d)
    w_out_b, w_up_b, w_down_b = w_out.astype(BF16), w_up.astype(BF16), w_down.astype(BF16)
    for l in range(depth):
        gate_b = jnp.pad(mlstm_gate_b[l], (0, LANES - n_gates)).reshape(1, LANES)
        qkv, qkv16, mall, gates = _inproj(xf, mix_pre_g[l], w_in, l, gate_b, cos_t, sin_t,
                                          attn_w=attn_w, mix_w=mix_w, seq=seq)
        attn = _attention(qkv, qkv16, batch=batch, seq=seq)
        gc, gr = _gateprep(gates, n_heads=n_heads)
        ml = _mlstm(mall, gc, gr, mlstm_conv_w[l], mlstm_conv_b[l].reshape(1, -1),
                    mlstm_head_g[l].reshape(1, -1), batch=batch, seq=seq, n_heads=n_heads)
        x1, hn2 = _outproj(attn, ml, w_out_b, l, xf, mix_post_g[l], ffn_pre_g[l])
        xf = _ffn(hn2, x1, w_up_b, w_down_b, l, ffn_conv_w[l],
                  ffn_conv_b[l], ffn_post_g[l], seq=seq)
    return xf.reshape(batch, seq, d)
```

```python
import functools

import numpy as np
import jax
import jax.numpy as jnp
from jax import lax
from jax.experimental import pallas as pl
from jax.experimental.pallas import tpu as pltpu

F32 = jnp.float32
BF16 = jnp.bfloat16

ATTN_HEAD_DIM = 64
ATTN_HALF_WINDOW = 64
DILATIONS = (1, 4, 16)
MLSTM_HEADS = 4
MLSTM_CHUNK = 128
ROPE_THETA = 10000.0
NORM_EPS = 1e-6
NEG_INF = -1e30

LANES = 128
Q_BLOCK = 128
K_WINDOW = Q_BLOCK + 2 * ATTN_HALF_WINDOW
DIL_MID, DIL_MAX = DILATIONS[1], DILATIONS[2]
DIL_RATIO = DIL_MAX // DIL_MID
LOG2_E = float(np.log2(np.e))
ATTN_Q_SCALE = float(ATTN_HEAD_DIM ** -0.5) * LOG2_E
VMEM_LIMIT_BYTES = 56 * 1024 * 1024


def _cparams(n_grid_dims):
    return pltpu.CompilerParams(
        dimension_semantics=("arbitrary",) * n_grid_dims,
        vmem_limit_bytes=VMEM_LIMIT_BYTES)


def _rms(x, g):
    return x * lax.rsqrt(jnp.mean(x * x, axis=-1, keepdims=True) + NORM_EPS) * g


def _inproj_kernel(x_ref, g_ref, w32_ref, gb_ref, cos_ref, sin_ref,
                   qkv_ref, qkv16_ref, mall_ref, gates_ref, w_ref, wg_ref, *stage_refs, attn_w, mix_w):
    n_main = w_ref.shape[1]

    @pl.when(pl.program_id(0) == 0)
    def _():
        w_ref[...] = w32_ref[:, :n_main].astype(w_ref.dtype)
        wg_ref[...] = jnp.zeros(wg_ref.shape, wg_ref.dtype)
        n_gates = w32_ref.shape[1] - n_main
        wg_ref[:, :n_gates] = w32_ref[:, n_main:].astype(wg_ref.dtype)

    hn = _rms(x_ref[...], g_ref[...]).astype(w_ref.dtype)
    cos = cos_ref[...]
    sin = sin_ref[...]
    tm = hn.shape[0]
    lane = lax.broadcasted_iota(jnp.int32, (tm, LANES), 1)
    first_half = (lane % ATTN_HEAD_DIM) < (ATTN_HEAD_DIM // 2)
    n_pairs = attn_w // LANES
    base = 3 * attn_w

    def mix_group(grp):
        res = jnp.dot(hn, w_ref[:, base + grp * mix_w: base + (grp + 1) * mix_w],
                      preferred_element_type=F32)
        mall_ref[:, grp * mix_w:(grp + 1) * mix_w] = res.astype(mall_ref.dtype)

    for grp in range(3):
        res = jnp.dot(hn, w_ref[:, grp * attn_w:(grp + 1) * attn_w],
                      preferred_element_type=F32)
        mix_group(grp)
        for hp in range(n_pairs):
            xs = res[:, hp * LANES:(hp + 1) * LANES]
            if grp < 2:
                rot = jnp.where(first_half,
                                pltpu.roll(xs, LANES - ATTN_HEAD_DIM // 2, 1),
                                pltpu.roll(xs, ATTN_HEAD_DIM // 2, 1))
                xs = xs * cos + rot * sin
            if grp == 0:
                xs = xs * ATTN_Q_SCALE
            slab = grp * n_pairs + hp
            qkv_ref[slab] = xs.astype(qkv_ref.dtype)
            stage_ref = stage_refs[slab]
            stage_ref[...] = xs
            for r in range(DIL_MAX):
                qkv16_ref[slab, :, r * LANES:(r + 1) * LANES] = (
                    stage_ref[pl.ds(r, tm // DIL_MAX, stride=DIL_MAX), :].astype(qkv16_ref.dtype))
    mix_group(3)
    gates_ref[...] = jnp.dot(hn, wg_ref[...], preferred_element_type=F32) + gb_ref[...]


def _inproj(x, g_pre, w_in, layer, gate_b, cos, sin, *, attn_w, mix_w, seq, tm=512):
    t, d = x.shape
    n_pairs = attn_w // LANES
    n_cols = w_in.shape[2]
    n_main = 3 * attn_w + 4 * mix_w
    tiles_per_seq = seq // tm
    kern = functools.partial(_inproj_kernel, attn_w=attn_w, mix_w=mix_w)
    return pl.pallas_call(
        kern,
        grid=(t // tm,),
        in_specs=[pl.BlockSpec((tm, d), lambda i: (i, 0)),
                  pl.BlockSpec((1, d), lambda i: (0, 0)),
                  pl.BlockSpec((None, d, n_cols), lambda i: (layer, 0, 0), pipeline_mode=pl.Buffered(1)),
                  pl.BlockSpec((1, LANES), lambda i: (0, 0)),
                  pl.BlockSpec((tm, LANES), lambda i: (i % tiles_per_seq, 0)),
                  pl.BlockSpec((tm, LANES), lambda i: (i % tiles_per_seq, 0))],
        out_specs=[pl.BlockSpec((3 * n_pairs, tm, LANES), lambda i: (0, i, 0)),
                   pl.BlockSpec((3 * n_pairs, tm // DIL_MAX, DIL_MAX * LANES), lambda i: (0, i, 0)),
                   pl.BlockSpec((tm, 4 * mix_w), lambda i: (i, 0)),
                   pl.BlockSpec((tm, LANES), lambda i: (i, 0))],
        out_shape=[jax.ShapeDtypeStruct((3 * n_pairs, t, LANES), BF16),
                   jax.ShapeDtypeStruct((3 * n_pairs, t // DIL_MAX, DIL_MAX * LANES), BF16),
                   jax.ShapeDtypeStruct((t, 4 * mix_w), BF16),
                   jax.ShapeDtypeStruct((t, LANES), F32)],
        scratch_shapes=([pltpu.VMEM((d, n_main), BF16), pltpu.VMEM((d, LANES), BF16)]
                        + [pltpu.VMEM((tm, LANES), F32) for _ in range(3 * n_pairs)]),
        compiler_params=_cparams(1),
        name="inproj",
    )(x, g_pre.reshape(1, d), w_in, gate_b, cos, sin)


def _head_masks(rows, dtype):
    lane = lax.broadcasted_iota(jnp.int32, (rows, LANES), 1)
    h0 = jnp.where(lane < ATTN_HEAD_DIM, 1.0, 0.0).astype(dtype)
    return h0, (1.0 - h0.astype(F32)).astype(dtype)


def _attn_scores(q, k, bias):
    q0, q1 = _head_masks(Q_BLOCK, q.dtype)
    q_st = jnp.concatenate([q * q0, q * q1], axis=0)
    s = lax.dot_general(q_st, k, (((1,), (1,)), ((), ())), preferred_element_type=F32)
    return s + jnp.concatenate([bias, bias], axis=0)


def _attn_finish(s, v):
    v0, v1 = _head_masks(v.shape[0], v.dtype)
    m = jnp.max(s, axis=-1, keepdims=True)
    p = jnp.exp2(s - m).astype(v.dtype)
    p_cat = jnp.concatenate([p[:Q_BLOCK], p[Q_BLOCK:]], axis=1)
    v_ext = jnp.concatenate([jnp.concatenate([v * v0, v0], axis=1),
                             jnp.concatenate([v * v1, v1], axis=1)], axis=0)
    o = jnp.dot(p_cat, v_ext, preferred_element_type=F32)
    lane_o = lax.broadcasted_iota(jnp.int32, (Q_BLOCK, LANES), 1)
    m_lanes = jnp.where(lane_o < ATTN_HEAD_DIM, m[:Q_BLOCK], m[Q_BLOCK:])
    return o[:, :LANES], o[:, LANES:], m_lanes


def _attn_kernel(qn, kn, vn, q16, k16, v16, bias1_ref, bias4_ref, bias16_ref, o_ref,
                 acc_ref, den_ref, max_ref, *, seq):
    n_blocks = seq // Q_BLOCK
    n_mid = (seq // DIL_MID) // Q_BLOCK
    n_max = seq // DIL_MAX
    q_rows = Q_BLOCK // DIL_RATIO
    k_rows = K_WINDOW // DIL_RATIO

    def edge_table(blk, n):
        return jnp.where(blk > 0, 1, 0) + jnp.where(blk == n - 1, 1, 0)

    def nat_window(i):
        qs = pl.multiple_of(i * Q_BLOCK, Q_BLOCK)
        ks = pl.multiple_of(jnp.clip(qs - ATTN_HALF_WINDOW, 0, seq - K_WINDOW), ATTN_HALF_WINDOW)
        return qs, ks

    def mid_window(i):
        cls, blk = i // n_mid, i % n_mid
        r0 = pl.multiple_of(blk * q_rows, q_rows)
        k0 = pl.multiple_of(jnp.clip(r0 - ATTN_HALF_WINDOW // DIL_RATIO, 0, n_max - k_rows),
                            ATTN_HALF_WINDOW // DIL_RATIO)
        lanes = [pl.ds(pl.multiple_of((cls + DIL_MID * m) * LANES, LANES), LANES) for m in range(DIL_RATIO)]
        return cls, blk, r0, k0, lanes

    def gather(ref, r0, rows, lanes):
        return jnp.concatenate([ref[pl.ds(r0, rows), ln] for ln in lanes], axis=0)

    def store(branch, start, size, stride, vals, rows=slice(None)):
        for ref, val in zip((acc_ref, den_ref, max_ref), vals):
            idx = pl.ds(start, size) if stride == 1 else pl.ds(start, size, stride=stride)
            ref[branch, idx, :] = val[rows]

    def body(i, carry):
        qs, ks = nat_window(i)
        s = _attn_scores(qn[pl.ds(qs, Q_BLOCK), :], kn[pl.ds(ks, K_WINDOW), :],
                         bias1_ref[edge_table(i, n_blocks)])
        store(0, qs, Q_BLOCK, 1, _attn_finish(s, vn[pl.ds(ks, K_WINDOW), :]))

        cls, blk, r0, k0, lanes = mid_window(i)
        s = _attn_scores(gather(q16, r0, q_rows, lanes), gather(k16, k0, k_rows, lanes),
                         bias4_ref[edge_table(blk, n_mid)])
        res = _attn_finish(s, gather(v16, k0, k_rows, lanes))
        for m in range(DIL_RATIO):
            store(1, DIL_MAX * r0 + DIL_MID * m + cls, q_rows, DIL_MAX, res,
                  rows=slice(m * q_rows, (m + 1) * q_rows))

        ln = pl.ds(pl.multiple_of(i * LANES, LANES), LANES)
        s = _attn_scores(q16[:, ln], k16[:, ln], bias16_ref[...])
        store(2, i, n_max, DIL_MAX, _attn_finish(s, v16[:, ln]))
        return carry

    lax.fori_loop(0, n_blocks, body, 0, unroll=8)

    def combine(i, carry):
        rows = pl.ds(pl.multiple_of(i * Q_BLOCK, Q_BLOCK), Q_BLOCK)
        ms = [max_ref[b, rows, :] for b in range(3)]
        mx = jnp.maximum(jnp.maximum(ms[0], ms[1]), ms[2])
        ws = [jnp.exp2(m - mx) for m in ms]
        num = ws[0] * acc_ref[0, rows, :] + ws[1] * acc_ref[1, rows, :] + ws[2] * acc_ref[2, rows, :]
        den = ws[0] * den_ref[0, rows, :] + ws[1] * den_ref[1, rows, :] + ws[2] * den_ref[2, rows, :]
        o_ref[rows, :] = (num / den).astype(o_ref.dtype)
        return carry

    lax.fori_loop(0, n_blocks, combine, 0)


def _attn_bias_tables():
    hw = ATTN_HALF_WINDOW
    row = np.arange(Q_BLOCK)[:, None]
    col = np.arange(K_WINDOW)[None, :]

    def bias(off):
        return np.where(np.abs(off) <= hw, 0.0, NEG_INF).astype(np.float32)

    nat = np.stack([bias(col - shift - row) for shift in (0, hw, 2 * hw)])
    q_rows, k_rows = Q_BLOCK // DIL_RATIO, K_WINDOW // DIL_RATIO
    qpos = DIL_RATIO * (row % q_rows) + row // q_rows
    kpos = DIL_RATIO * (col % k_rows) + col // k_rows
    mid = np.stack([bias(kpos - shift - qpos) for shift in (0, hw, 2 * hw)])
    wide = bias(np.arange(Q_BLOCK)[None, :] - row)
    return nat, mid, wide


def _attention(qkv, qkv16, *, batch, seq):
    n3, t, _ = qkv.shape
    n_pairs = n3 // 3
    n_max = seq // DIL_MAX
    assert DILATIONS[0] == 1 and n_max == Q_BLOCK and seq % (DIL_MID * Q_BLOCK) == 0
    nat, mid, wide = _attn_bias_tables()
    in_specs = ([pl.BlockSpec((None, seq, LANES), lambda b, hp, w=w: (w * n_pairs + hp, b, 0))
                 for w in range(3)]
                + [pl.BlockSpec((None, n_max, DIL_MAX * LANES), lambda b, hp, w=w: (w * n_pairs + hp, b, 0))
                   for w in range(3)]
                + [pl.BlockSpec(nat.shape, lambda b, hp: (0, 0, 0)),
                   pl.BlockSpec(mid.shape, lambda b, hp: (0, 0, 0)),
                   pl.BlockSpec(wide.shape, lambda b, hp: (0, 0))])
    scratch = [pltpu.VMEM((3, seq, LANES), F32) for _ in range(3)]
    return pl.pallas_call(
        functools.partial(_attn_kernel, seq=seq),
        grid=(batch, n_pairs),
        in_specs=in_specs,
        out_specs=pl.BlockSpec((seq, LANES), lambda b, hp: (b, hp)),
        out_shape=jax.ShapeDtypeStruct((t, n_pairs * LANES), BF16),
        scratch_shapes=scratch,
        compiler_params=_cparams(2),
        name="dilated_attn",
    )(qkv, qkv, qkv, qkv16, qkv16, qkv16, nat, mid, wide)


GATE_GROUP = 16
GATE_ROWS = 3 * GATE_GROUP


def _gateprep_kernel(g_ref, o_ref, orow_ref, *, n_heads):
    ch = MLSTM_CHUNK
    src = lax.broadcasted_iota(jnp.int32, (ch, ch), 0)
    dst = lax.broadcasted_iota(jnp.int32, (ch, ch), 1)
    tri_prefix = (src <= dst).astype(F32)
    tri_suffix = (src >= dst).astype(F32)
    n_c = g_ref.shape[0] // ch
    grow = lax.broadcasted_iota(jnp.int32, (n_c * GATE_GROUP, ch), 0)
    pos = lax.broadcasted_iota(jnp.int32, (n_c * GATE_GROUP, ch), 1)
    is_fwd = (grow % GATE_GROUP) < 2 * n_heads
    pad = jnp.zeros((LANES - GATE_ROWS, ch), F32)

    g = jnp.concatenate([g_ref[c * ch:(c + 1) * ch, :].T[:GATE_GROUP, :] for c in range(n_c)], axis=0)
    log_f = jnp.minimum(g, 0.0) - jnp.log(1.0 + jnp.exp(-jnp.abs(g)))
    pre = jnp.dot(log_f, tri_prefix, preferred_element_type=F32, precision=lax.Precision.HIGHEST)
    suf = jnp.dot(log_f, tri_suffix, preferred_element_type=F32, precision=lax.Precision.HIGHEST)
    b = jnp.where(is_fwd, pre, suf)
    a = pltpu.roll(g, n_heads, 0) - b
    cm = a
    step = 1
    while step < ch:
        ahead = jnp.where(pos >= step, pltpu.roll(cm, step, 1), -jnp.inf)
        behind = jnp.where(pos + step < ch, pltpu.roll(cm, ch - step, 1), -jnp.inf)
        cm = jnp.maximum(cm, jnp.where(is_fwd, ahead, behind))
        step *= 2
    for c in range(n_c):
        part = slice(c * GATE_GROUP, (c + 1) * GATE_GROUP)
        rows = jnp.concatenate([b[part], a[part], cm[part]], axis=0) * LOG2_E
        orow_ref[c] = rows
        o_ref[c * ch:(c + 1) * ch, :] = jnp.concatenate([rows, pad], axis=0).T


def _gateprep(gates, *, n_heads, tm=2048):
    t, w = gates.shape
    assert 4 * n_heads == GATE_GROUP and GATE_ROWS <= w
    cpt = tm // MLSTM_CHUNK
    return pl.pallas_call(
        functools.partial(_gateprep_kernel, n_heads=n_heads),
        grid=(t // tm,),
        in_specs=[pl.BlockSpec((tm, w), lambda i: (i, 0))],
        out_specs=[pl.BlockSpec((tm, w), lambda i: (i, 0)),
                   pl.BlockSpec((cpt, GATE_ROWS, MLSTM_CHUNK), lambda i: (i, 0, 0))],
        out_shape=[jax.ShapeDtypeStruct((t, w), F32),
                   jax.ShapeDtypeStruct((t // MLSTM_CHUNK, GATE_ROWS, MLSTM_CHUNK), F32)],
        compiler_params=_cparams(1),
        name="gateprep",
    )(gates)


CONV_HALO = 16


def _conv_silu_chunks(x_ref, w_ref, b_ref, stage_ref, emit):
    n = x_ref.shape[0]
    taps = w_ref.shape[0]
    pad = taps // 2
    zeros = jnp.zeros((CONV_HALO, x_ref.shape[1]), F32)
    stage_ref[0:CONV_HALO, :] = zeros
    stage_ref[CONV_HALO + n:, :] = zeros
    stage_ref[CONV_HALO:CONV_HALO + n, :] = x_ref[...].astype(F32)

    def body(c, carry):
        base = pl.multiple_of(c * MLSTM_CHUNK, MLSTM_CHUNK)
        acc = b_ref[...]
        for j in range(taps):
            acc = acc + stage_ref[pl.ds(base + (CONV_HALO + j - pad), MLSTM_CHUNK), :] * w_ref[j:j + 1, :]
        half = 0.5 * acc
        emit(c, half + half * jnp.tanh(half))
        return carry

    lax.fori_loop(0, n // MLSTM_CHUNK, body, 0, unroll=2)


def _mlstm_kernel(mq_ref, mk_ref, mv_ref, mo_ref, gc_ref, gr_ref, cwq_ref, cwk_ref, cbq_ref, cbk_ref,
                  hg_ref, o_ref, q_s, kt_s, vx_s, gc_s, sc_s, hf_s, hb_s, stage_s, *, n_heads):
    head = pl.program_id(1)
    seq, dh = q_s.shape
    ch = MLSTM_CHUNK
    n_chunks = seq // ch
    def emit_q(c, y):
        q_s[pl.ds(pl.multiple_of(c * ch, ch), ch), :] = y.astype(q_s.dtype)

    def emit_k(c, y):
        kt_s[:, pl.ds(pl.multiple_of(c * ch, ch), ch)] = (y * (dh ** -0.5)).astype(kt_s.dtype).T

    _conv_silu_chunks(mq_ref, cwq_ref, cbq_ref, stage_s, emit_q)
    _conv_silu_chunks(mk_ref, cwk_ref, cbk_ref, stage_s, emit_k)
    vx_s[:, :dh] = mv_ref[...]
    vx_s[:, dh:] = jnp.ones((seq, dh), vx_s.dtype)
    gc_s[...] = pltpu.roll(gc_ref[...], (LANES - head) % LANES, 1)

    def gate_lane(group, d):
        return group * GATE_GROUP + (2 * d + 1) * n_heads

    for d in range(2):
        lane_b, lane_cm = gate_lane(0, d), gate_lane(2, d)
        m_run = jnp.zeros((1, 2 * dh), F32)
        for step in range(n_chunks):
            c = step if d == 0 else n_chunks - 1 - step
            last = c * ch + (ch - 1 if d == 0 else 0)
            b_last = jnp.broadcast_to(gc_s[last:last + 1, lane_b:lane_b + 1], (1, 2 * dh))
            a_max = jnp.broadcast_to(gc_s[last:last + 1, lane_cm:lane_cm + 1], (1, 2 * dh))
            w_max = jnp.maximum(m_run, a_max)
            sc_s[d, 0, c:c + 1, :] = m_run
            sc_s[d, 1, c:c + 1, :] = w_max
            sc_s[d, 2, c:c + 1, :] = jnp.exp2(m_run - w_max)
            m_run = b_last + w_max

    row = lax.broadcasted_iota(jnp.int32, (ch, ch), 0)
    col = lax.broadcasted_iota(jnp.int32, (ch, ch), 1)

    def direction(c, d, mask, state, h_store):
        lane_b, lane_cm = gate_lane(0, d), gate_lane(2, d)
        rs = pl.multiple_of(c * ch, ch)
        qc = q_s[pl.ds(rs, ch), :]
        ktc = kt_s[:, pl.ds(rs, ch)]
        vx = vx_s[pl.ds(rs, ch), :]
        a_row = gr_ref[c, pl.ds(gate_lane(1, d) + head, 1), :]
        m_row = sc_s[d, 0, pl.ds(c, 1), :]
        w_max = sc_s[d, 1, pl.ds(c, 1), :]
        decay = sc_s[d, 2, pl.ds(c, 1), :]
        g = gc_s[pl.ds(rs, ch), :]
        b_bc = jnp.broadcast_to(g[:, lane_b:lane_b + 1], (ch, dh))
        cm_bc = jnp.broadcast_to(g[:, lane_cm:lane_cm + 1], (ch, dh))
        mm = jnp.maximum(m_row[:, :dh], cm_bc)
        w = jnp.where(mask, jnp.exp2(a_row - mm), 0.0)
        s = jnp.dot(qc, ktc, preferred_element_type=F32)
        inter = jnp.dot(qc, state.astype(qc.dtype), preferred_element_type=F32)
        kw_t = (ktc.astype(F32) * jnp.exp2(a_row - w_max[:, :ch])).astype(vx.dtype)
        new_state = decay * state + jnp.dot(kw_t, vx, preferred_element_type=F32)
        intra = jnp.dot((s * w).astype(vx.dtype), vx, preferred_element_type=F32)
        g_int = jnp.exp2(m_row[:, :dh] - mm)
        num = intra[:, :dh] + g_int * inter[:, :dh]
        den = intra[:, dh:] + g_int * inter[:, dh:]
        h_store[pl.ds(rs, ch), :] = num / jnp.maximum(jnp.abs(den), jnp.exp2(-(b_bc + mm)))
        return new_state

    def step(j, carry):
        fwd, bwd = carry
        fwd = direction(j, 0, col <= row, fwd, hf_s)
        bwd = direction(n_chunks - 1 - j, 1, col >= row, bwd, hb_s)
        return fwd, bwd

    zero = jnp.zeros((dh, 2 * dh), F32)
    lax.fori_loop(0, n_chunks, step, (zero, zero), unroll=8)

    hm = hf_s[...] + hb_s[...]
    hm = hm * lax.rsqrt(jnp.mean(hm * hm, axis=-1, keepdims=True) + NORM_EPS)
    o_ref[...] = (hm * hg_ref[...] * jax.nn.sigmoid(mo_ref[...].astype(F32))).astype(o_ref.dtype)


def _mlstm(mall, gc, gr, conv_w, conv_b, head_g, *, batch, seq, n_heads):
    t = mall.shape[0]
    dh = mall.shape[1] // (4 * n_heads)
    taps = conv_w.shape[0]
    n_chunks = seq // MLSTM_CHUNK
    assert dh == MLSTM_CHUNK == LANES

    def col_block(offset):
        return pl.BlockSpec((seq, dh), lambda b, h, offset=offset: (b, offset * n_heads + h))

    in_specs = [col_block(0), col_block(1), col_block(2), col_block(3),
                pl.BlockSpec((seq, LANES), lambda b, h: (b, 0)),
                pl.BlockSpec((n_chunks, GATE_ROWS, MLSTM_CHUNK), lambda b, h: (b, 0, 0)),
                pl.BlockSpec((taps, dh), lambda b, h: (0, h)),
                pl.BlockSpec((taps, dh), lambda b, h: (0, n_heads + h)),
                pl.BlockSpec((1, dh), lambda b, h: (0, h)),
                pl.BlockSpec((1, dh), lambda b, h: (0, n_heads + h)),
                pl.BlockSpec((1, dh), lambda b, h: (0, h))]
    scratch = [pltpu.VMEM((seq, dh), BF16),
               pltpu.VMEM((dh, seq), BF16),
               pltpu.VMEM((seq, 2 * dh), BF16),
               pltpu.VMEM((seq, LANES), F32),
               pltpu.VMEM((2, 3, n_chunks, 2 * dh), F32),
               pltpu.VMEM((seq, dh), F32), pltpu.VMEM((seq, dh), F32),
               pltpu.VMEM((seq + 2 * CONV_HALO, dh), F32)]
    return pl.pallas_call(
        functools.partial(_mlstm_kernel, n_heads=n_heads),
        grid=(batch, n_heads),
        in_specs=in_specs,
        out_specs=pl.BlockSpec((seq, dh), lambda b, h: (b, h)),
        out_shape=jax.ShapeDtypeStruct((t, n_heads * dh), BF16),
        scratch_shapes=scratch,
        compiler_params=_cparams(2),
        name="mlstm",
    )(mall, mall, mall, mall, gc, gr, conv_w, conv_w, conv_b, conv_b, head_g)


def _outproj_kernel(a_ref, m_ref, w_ref, x_ref, gpost_ref, gpre_ref, x1_ref, hn_ref):
    ka = a_ref.shape[1]
    mixed = (jnp.dot(a_ref[...], w_ref[:ka, :], preferred_element_type=F32)
             + jnp.dot(m_ref[...], w_ref[ka:, :], preferred_element_type=F32))
    x1 = x_ref[...] + _rms(mixed, gpost_ref[...])
    x1_ref[...] = x1
    hn_ref[...] = _rms(x1, gpre_ref[...]).astype(hn_ref.dtype)


def _outproj(attn, ml, w_out, layer, x, g_post, g_pre, tm=1024):
    t, d = x.shape
    ka, km = attn.shape[1], ml.shape[1]
    return pl.pallas_call(
        _outproj_kernel,
        grid=(t // tm,),
        in_specs=[pl.BlockSpec((tm, ka), lambda i: (i, 0)),
                  pl.BlockSpec((tm, km), lambda i: (i, 0)),
                  pl.BlockSpec((None, ka + km, d), lambda i: (layer, 0, 0)),
                  pl.BlockSpec((tm, d), lambda i: (i, 0)),
                  pl.BlockSpec((1, d), lambda i: (0, 0)),
                  pl.BlockSpec((1, d), lambda i: (0, 0))],
        out_specs=[pl.BlockSpec((tm, d), lambda i: (i, 0)),
                   pl.BlockSpec((tm, d), lambda i: (i, 0))],
        out_shape=[jax.ShapeDtypeStruct((t, d), F32), jax.ShapeDtypeStruct((t, d), BF16)],
        compiler_params=_cparams(1),
        name="outproj",
    )(attn, ml, w_out, x, g_post.reshape(1, d), g_pre.reshape(1, d))


FFN_HALO = 16
FFN_CHUNK = 256
FFN_STAGES = 4
FFN_SUBTILES = 1


def _ffn_kernel(prev_ref, main_ref, next_ref, x1_ref, wup_ref, wdn_ref, cw_ref, cb_ref,
                gpost_ref, x2_ref, *scratch, blocks_per_seq):
    ug_refs = scratch[:FFN_STAGES]
    uv_refs = scratch[FFN_STAGES:2 * FFN_STAGES]
    acc_refs = scratch[2 * FFN_STAGES:]
    i = pl.program_id(0)
    tm = main_ref.shape[0] // FFN_SUBTILES
    d_ff = wdn_ref.shape[0]
    pos = i % blocks_per_seq
    prev = jnp.where(pos == 0, jnp.zeros_like(prev_ref[...]), prev_ref[...])
    nxt = jnp.where(pos == blocks_per_seq - 1, jnp.zeros_like(next_ref[...]), next_ref[...])
    lhs = []
    for s in range(FFN_SUBTILES):
        before = prev if s == 0 else main_ref[s * tm - FFN_HALO:s * tm, :]
        after = nxt if s == FFN_SUBTILES - 1 else main_ref[(s + 1) * tm:(s + 1) * tm + FFN_HALO, :]
        lhs.append(jnp.concatenate([before, main_ref[s * tm:(s + 1) * tm, :], after], axis=0))
    taps = cw_ref.shape[0]

    row0 = pl.multiple_of(jnp.minimum(i, 0), FFN_HALO)

    def stage(u_ref, u):
        for k in range(FFN_CHUNK // LANES):
            u_ref[k] = u[:, k * LANES:(k + 1) * LANES]

    def conv(u_ref, k, c0):
        cols = slice(c0 + k * LANES, c0 + (k + 1) * LANES)
        out = cb_ref[:, cols]
        for j in range(taps):
            start = FFN_HALO + j - taps // 2
            out = out + u_ref[k, pl.ds(row0 + start, tm), :] * cw_ref[j:j + 1, cols]
        return out

    gelu_c0 = float(np.sqrt(2.0 / np.pi))
    gelu_c1 = float(np.sqrt(2.0 / np.pi) * 0.044715)
    n_chunks = d_ff // FFN_CHUNK
    n_total = FFN_SUBTILES * n_chunks

    def up_project(g):
        s, c0 = g // n_chunks, (g % n_chunks) * FFN_CHUNK
        stage(ug_refs[g % FFN_STAGES], jnp.dot(lhs[s], wup_ref[:, c0:c0 + FFN_CHUNK],
                                               preferred_element_type=F32))
        stage(uv_refs[g % FFN_STAGES], jnp.dot(lhs[s], wup_ref[:, d_ff + c0:d_ff + c0 + FFN_CHUNK],
                                               preferred_element_type=F32))

    for g in range(FFN_STAGES - 1):
        up_project(g)
    for g in range(n_total):
        s, c = g // n_chunks, g % n_chunks
        c0 = c * FFN_CHUNK
        slot = g % FFN_STAGES
        if g + FFN_STAGES - 1 < n_total:
            up_project(g + FFN_STAGES - 1)
        acts = []
        for k in range(FFN_CHUNK // LANES):
            gate = conv(ug_refs[slot], k, c0)
            val = conv(uv_refs[slot], k, d_ff + c0)
            half = (0.5 * gate) * val
            inner = gate * (gelu_c0 + gelu_c1 * (gate * gate))
            acts.append((half + half * jnp.tanh(inner)).astype(lhs[s].dtype))
        act = jnp.concatenate(acts, axis=1)
        part = jnp.dot(act, wdn_ref[c0:c0 + FFN_CHUNK, :], preferred_element_type=F32)
        if c == 0:
            acc_refs[s][...] = part
        else:
            acc_refs[s][...] += part
        if c == n_chunks - 1:
            rows = slice(s * tm, (s + 1) * tm)
            x2_ref[rows, :] = x1_ref[rows, :] + _rms(acc_refs[s][...], gpost_ref[...])


def _ffn(hn, x1, w_up, w_down, layer, conv_w, conv_b, g_post, *, seq, tm=512):
    t, d = x1.shape
    d_ff = w_down.shape[1]
    rows = FFN_SUBTILES * tm
    blocks_per_seq = seq // rows
    halo_per_block = rows // FFN_HALO
    n_halo_blocks = t // FFN_HALO
    return pl.pallas_call(
        functools.partial(_ffn_kernel, blocks_per_seq=blocks_per_seq),
        grid=(t // rows,),
        in_specs=[pl.BlockSpec((FFN_HALO, d), lambda i: (jnp.maximum(i * halo_per_block - 1, 0), 0)),
                  pl.BlockSpec((rows, d), lambda i: (i, 0)),
                  pl.BlockSpec((FFN_HALO, d),
                               lambda i: (jnp.minimum((i + 1) * halo_per_block, n_halo_blocks - 1), 0)),
                  pl.BlockSpec((rows, d), lambda i: (i, 0)),
                  pl.BlockSpec((None, d, 2 * d_ff), lambda i: (layer, 0, 0), pipeline_mode=pl.Buffered(1)),
                  pl.BlockSpec((None, d_ff, d), lambda i: (layer, 0, 0), pipeline_mode=pl.Buffered(1)),
                  pl.BlockSpec(conv_w.shape, lambda i: (0, 0)),
                  pl.BlockSpec((1, 2 * d_ff), lambda i: (0, 0)),
                  pl.BlockSpec((1, d), lambda i: (0, 0))],
        out_specs=pl.BlockSpec((rows, d), lambda i: (i, 0)),
        out_shape=jax.ShapeDtypeStruct((t, d), F32),
        scratch_shapes=([pltpu.VMEM((FFN_CHUNK // LANES, tm + 2 * FFN_HALO, LANES), F32)
                         for _ in range(2 * FFN_STAGES)]
                        + [pltpu.VMEM((tm, d), F32) for _ in range(FFN_SUBTILES)]),
        compiler_params=_cparams(1),
        name="convffn",
    )(hn, hn, hn, x1, w_up, w_down, conv_w, conv_b.reshape(1, -1), g_post.reshape(1, d))


def _rotary_tables(seq):
    half = ATTN_HEAD_DIM // 2
    inv_freq = ROPE_THETA ** (-jnp.arange(0, ATTN_HEAD_DIM, 2, dtype=F32) / ATTN_HEAD_DIM)
    ang = jnp.arange(seq, dtype=F32)[:, None] * inv_freq[None, :]
    cos, sin = jnp.cos(ang), jnp.sin(ang)
    reps = LANES // ATTN_HEAD_DIM
    cos_t = jnp.tile(jnp.concatenate([cos, cos], axis=-1), (1, reps))
    sin_t = jnp.tile(jnp.concatenate([-sin, sin], axis=-1), (1, reps))
    return cos_t, sin_t


def kernel(x, mix_pre_g, w_in, mlstm_conv_w, mlstm_conv_b, mlstm_gate_b, mlstm_head_g, w_out,
           mix_post_g, ffn_pre_g, w_up, ffn_conv_w, ffn_conv_b, w_down, ffn_post_g):
    batch, seq, d = x.shape
    depth = w_in.shape[0]
    t = batch * seq
    n_heads = MLSTM_HEADS
    n_gates = mlstm_gate_b.shape[1]
    mix_w = mlstm_head_g.shape[1]
    attn_w = w_out.shape[1] - mix_w
    assert n_gates == 4 * n_heads and n_gates <= LANES
    assert w_in.shape[2] == 3 * attn_w + 4 * mix_w + n_gates

    cos_t, sin_t = _rotary_tables(seq)
    xf = x.reshape(t, d)
    w_out_b, w_up_b, w_down_b = w_out.astype(BF16), w_up.astype(BF16), w_down.astype(BF16)
    for l in range(depth):
        gate_b = jnp.pad(mlstm_gate_b[l], (0, LANES - n_gates)).reshape(1, LANES)
        qkv, qkv16, mall, gates = _inproj(xf, mix_pre_g[l], w_in, l, gate_b, cos_t, sin_t,
                                          attn_w=attn_w, mix_w=mix_w, seq=seq)
        attn = _attention(qkv, qkv16, batch=batch, seq=seq)
        gc, gr = _gateprep(gates, n_heads=n_heads)
        ml = _mlstm(mall, gc, gr, mlstm_conv_w[l], mlstm_conv_b[l].reshape(1, -1),
                    mlstm_head_g[l].reshape(1, -1), batch=batch, seq=seq, n_heads=n_heads)
        x1, hn2 = _outproj(attn, ml, w_out_b, l, xf, mix_post_g[l], ffn_pre_g[l])
        xf = _ffn(hn2, x1, w_up_b, w_down_b, l, ffn_conv_w[l],
                  ffn_conv_b[l], ffn_post_g[l], seq=seq)
    return xf.reshape(batch, seq, d)
```

```python
import functools

import numpy as np
import jax
import jax.numpy as jnp
from jax import lax
from jax.experimental import pallas as pl
from jax.experimental.pallas import tpu as pltpu

F32 = jnp.float32
BF16 = jnp.bfloat16

ATTN_HEAD_DIM = 64
ATTN_HALF_WINDOW = 64
DILATIONS = (1, 4, 16)
MLSTM_HEADS = 4
MLSTM_CHUNK = 128
ROPE_THETA = 10000.0
NORM_EPS = 1e-6
NEG_INF = -1e30

LANES = 128
Q_BLOCK = 128
K_WINDOW = Q_BLOCK + 2 * ATTN_HALF_WINDOW
DIL_MID, DIL_MAX = DILATIONS[1], DILATIONS[2]
DIL_RATIO = DIL_MAX // DIL_MID
LOG2_E = float(np.log2(np.e))
ATTN_Q_SCALE = float(ATTN_HEAD_DIM ** -0.5) * LOG2_E
VMEM_LIMIT_BYTES = 56 * 1024 * 1024


def _cparams(n_grid_dims):
    return pltpu.CompilerParams(
        dimension_semantics=("arbitrary",) * n_grid_dims,
        vmem_limit_bytes=VMEM_LIMIT_BYTES)


def _rms(x, g):
    return x * lax.rsqrt(jnp.mean(x * x, axis=-1, keepdims=True) + NORM_EPS) * g


def _inproj_kernel(x_ref, g_ref, w32_ref, gb_ref, cos_ref, sin_ref,
                   qkv_ref, qkv16_ref, mall_ref, gates_ref, w_ref, wg_ref, *stage_refs, attn_w, mix_w):
    n_main = w_ref.shape[1]

    @pl.when(pl.program_id(0) == 0)
    def _():
        w_ref[...] = w32_ref[:, :n_main].astype(w_ref.dtype)
        wg_ref[...] = jnp.zeros(wg_ref.shape, wg_ref.dtype)
        n_gates = w32_ref.shape[1] - n_main
        wg_ref[:, :n_gates] = w32_ref[:, n_main:].astype(wg_ref.dtype)

    hn = _rms(x_ref[...], g_ref[...]).astype(w_ref.dtype)
    cos = cos_ref[...]
    sin = sin_ref[...]
    tm = hn.shape[0]
    lane = lax.broadcasted_iota(jnp.int32, (tm, LANES), 1)
    first_half = (lane % ATTN_HEAD_DIM) < (ATTN_HEAD_DIM // 2)
    n_pairs = attn_w // LANES
    base = 3 * attn_w

    def mix_group(grp):
        res = jnp.dot(hn, w_ref[:, base + grp * mix_w: base + (grp + 1) * mix_w],
                      preferred_element_type=F32)
        mall_ref[:, grp * mix_w:(grp + 1) * mix_w] = res.astype(mall_ref.dtype)

    for grp in range(3):
        res = jnp.dot(hn, w_ref[:, grp * attn_w:(grp + 1) * attn_w],
                      preferred_element_type=F32)
        mix_group(grp)
        for hp in range(n_pairs):
            xs = res[:, hp * LANES:(hp + 1) * LANES]
            if grp < 2:
                rot = jnp.where(first_half,
                                pltpu.roll(xs, LANES - ATTN_HEAD_DIM // 2, 1),
                                pltpu.roll(xs, ATTN_HEAD_DIM // 2, 1))
                xs = xs * cos + rot * sin
            if grp == 0:
                xs = xs * ATTN_Q_SCALE
            slab = grp * n_pairs + hp
            qkv_ref[slab] = xs.astype(qkv_ref.dtype)
            stage_ref = stage_refs[slab]
            stage_ref[...] = xs
            for r in range(DIL_MAX):
                qkv16_ref[slab, :, r * LANES:(r + 1) * LANES] = (
                    stage_ref[pl.ds(r, tm // DIL_MAX, stride=DIL_MAX), :].astype(qkv16_ref.dtype))
    mix_group(3)
    gates_ref[...] = jnp.dot(hn, wg_ref[...], preferred_element_type=F32) + gb_ref[...]


def _inproj(x, g_pre, w_in, layer, gate_b, cos, sin, *, attn_w, mix_w, seq, tm=512):
    t, d = x.shape
    n_pairs = attn_w // LANES
    n_cols = w_in.shape[2]
    n_main = 3 * attn_w + 4 * mix_w
    tiles_per_seq = seq // tm
    kern = functools.partial(_inproj_kernel, attn_w=attn_w, mix_w=mix_w)
    return pl.pallas_call(
        kern,
        grid=(t // tm,),
        in_specs=[pl.BlockSpec((tm, d), lambda i: (i, 0)),
                  pl.BlockSpec((1, d), lambda i: (0, 0)),
                  pl.BlockSpec((None, d, n_cols), lambda i: (layer, 0, 0), pipeline_mode=pl.Buffered(1)),
                  pl.BlockSpec((1, LANES), lambda i: (0, 0)),
                  pl.BlockSpec((tm, LANES), lambda i: (i % tiles_per_seq, 0)),
                  pl.BlockSpec((tm, LANES), lambda i: (i % tiles_per_seq, 0))],
        out_specs=[pl.BlockSpec((3 * n_pairs, tm, LANES), lambda i: (0, i, 0)),
                   pl.BlockSpec((3 * n_pairs, tm // DIL_MAX, DIL_MAX * LANES), lambda i: (0, i, 0)),
                   pl.BlockSpec((tm, 4 * mix_w), lambda i: (i, 0)),
                   pl.BlockSpec((tm, LANES), lambda i: (i, 0))],
        out_shape=[jax.ShapeDtypeStruct((3 * n_pairs, t, LANES), BF16),
                   jax.ShapeDtypeStruct((3 * n_pairs, t // DIL_MAX, DIL_MAX * LANES), BF16),
                   jax.ShapeDtypeStruct((t, 4 * mix_w), BF16),
                   jax.ShapeDtypeStruct((t, LANES), F32)],
        scratch_shapes=([pltpu.VMEM((d, n_main), BF16), pltpu.VMEM((d, LANES), BF16)]
                        + [pltpu.VMEM((tm, LANES), F32) for _ in range(3 * n_pairs)]),
        compiler_params=_cparams(1),
        name="inproj",
    )(x, g_pre.reshape(1, d), w_in, gate_b, cos, sin)


def _head_masks(rows, dtype):
    lane = lax.broadcasted_iota(jnp.int32, (rows, LANES), 1)
    h0 = jnp.where(lane < ATTN_HEAD_DIM, 1.0, 0.0).astype(dtype)
    return h0, (1.0 - h0.astype(F32)).astype(dtype)


def _attn_scores(q, k, bias):
    q0, q1 = _head_masks(Q_BLOCK, q.dtype)
    q_st = jnp.concatenate([q * q0, q * q1], axis=0)
    s = lax.dot_general(q_st, k, (((1,), (1,)), ((), ())), preferred_element_type=F32)
    return s + jnp.concatenate([bias, bias], axis=0)


def _attn_finish(s, v):
    v0, v1 = _head_masks(v.shape[0], v.dtype)
    m = jnp.max(s, axis=-1, keepdims=True)
    p = jnp.exp2(s - m).astype(v.dtype)
    p_cat = jnp.concatenate([p[:Q_BLOCK], p[Q_BLOCK:]], axis=1)
    v_ext = jnp.concatenate([jnp.concatenate([v * v0, v0], axis=1),
                             jnp.concatenate([v * v1, v1], axis=1)], axis=0)
    o = jnp.dot(p_cat, v_ext, preferred_element_type=F32)
    lane_o = lax.broadcasted_iota(jnp.int32, (Q_BLOCK, LANES), 1)
    m_lanes = jnp.where(lane_o < ATTN_HEAD_DIM, m[:Q_BLOCK], m[Q_BLOCK:])
    return o[:, :LANES], o[:, LANES:], m_lanes


def _attn_kernel(qn, kn, vn, q16, k16, v16, bias1_ref, bias4_ref, bias16_ref, o_ref,
                 acc_ref, den_ref, max_ref, *, seq):
    n_blocks = seq // Q_BLOCK
    n_mid = (seq // DIL_MID) // Q_BLOCK
    n_max = seq // DIL_MAX
    q_rows = Q_BLOCK // DIL_RATIO
    k_rows = K_WINDOW // DIL_RATIO

    def edge_table(blk, n):
        return jnp.where(blk > 0, 1, 0) + jnp.where(blk == n - 1, 1, 0)

    def nat_window(i):
        qs = pl.multiple_of(i * Q_BLOCK, Q_BLOCK)
        ks = pl.multiple_of(jnp.clip(qs - ATTN_HALF_WINDOW, 0, seq - K_WINDOW), ATTN_HALF_WINDOW)
        return qs, ks

    def mid_window(i):
        cls, blk = i // n_mid, i % n_mid
        r0 = pl.multiple_of(blk * q_rows, q_rows)
        k0 = pl.multiple_of(jnp.clip(r0 - ATTN_HALF_WINDOW // DIL_RATIO, 0, n_max - k_rows),
                            ATTN_HALF_WINDOW // DIL_RATIO)
        lanes = [pl.ds(pl.multiple_of((cls + DIL_MID * m) * LANES, LANES), LANES) for m in range(DIL_RATIO)]
        return cls, blk, r0, k0, lanes

    def gather(ref, r0, rows, lanes):
        return jnp.concatenate([ref[pl.ds(r0, rows), ln] for ln in lanes], axis=0)

    def store(branch, start, size, stride, vals, rows=slice(None)):
        for ref, val in zip((acc_ref, den_ref, max_ref), vals):
            idx = pl.ds(start, size) if stride == 1 else pl.ds(start, size, stride=stride)
            ref[branch, idx, :] = val[rows]

    def body(i, carry):
        qs, ks = nat_window(i)
        s = _attn_scores(qn[pl.ds(qs, Q_BLOCK), :], kn[pl.ds(ks, K_WINDOW), :],
                         bias1_ref[edge_table(i, n_blocks)])
        store(0, qs, Q_BLOCK, 1, _attn_finish(s, vn[pl.ds(ks, K_WINDOW), :]))

        cls, blk, r0, k0, lanes = mid_window(i)
        s = _attn_scores(gather(q16, r0, q_rows, lanes), gather(k16, k0, k_rows, lanes),
                         bias4_ref[edge_table(blk, n_mid)])
        res = _attn_finish(s, gather(v16, k0, k_rows, lanes))
        for m in range(DIL_RATIO):
            store(1, DIL_MAX * r0 + DIL_MID * m + cls, q_rows, DIL_MAX, res,
                  rows=slice(m * q_rows, (m + 1) * q_rows))

        ln = pl.ds(pl.multiple_of(i * LANES, LANES), LANES)
        s = _attn_scores(q16[:, ln], k16[:, ln], bias16_ref[...])
        store(2, i, n_max, DIL_MAX, _attn_finish(s, v16[:, ln]))
        return carry

    lax.fori_loop(0, n_blocks, body, 0, unroll=16)

    def combine(i, carry):
        rows = pl.ds(pl.multiple_of(i * Q_BLOCK, Q_BLOCK), Q_BLOCK)
        ms = [max_ref[b, rows, :] for b in range(3)]
        mx = jnp.maximum(jnp.maximum(ms[0], ms[1]), ms[2])
        ws = [jnp.exp2(m - mx) for m in ms]
        num = ws[0] * acc_ref[0, rows, :] + ws[1] * acc_ref[1, rows, :] + ws[2] * acc_ref[2, rows, :]
        den = ws[0] * den_ref[0, rows, :] + ws[1] * den_ref[1, rows, :] + ws[2] * den_ref[2, rows, :]
        o_ref[rows, :] = (num / den).astype(o_ref.dtype)
        return carry

    lax.fori_loop(0, n_blocks, combine, 0)


def _attn_bias_tables():
    hw = ATTN_HALF_WINDOW
    row = np.arange(Q_BLOCK)[:, None]
    col = np.arange(K_WINDOW)[None, :]

    def bias(off):
        return np.where(np.abs(off) <= hw, 0.0, NEG_INF).astype(np.float32)

    nat = np.stack([bias(col - shift - row) for shift in (0, hw, 2 * hw)])
    q_rows, k_rows = Q_BLOCK // DIL_RATIO, K_WINDOW // DIL_RATIO
    qpos = DIL_RATIO * (row % q_rows) + row // q_rows
    kpos = DIL_RATIO * (col % k_rows) + col // k_rows
    mid = np.stack([bias(kpos - shift - qpos) for shift in (0, hw, 2 * hw)])
    wide = bias(np.arange(Q_BLOCK)[None, :] - row)
    return nat, mid, wide


def _attention(qkv, qkv16, *, batch, seq):
    n3, t, _ = qkv.shape
    n_pairs = n3 // 3
    n_max = seq // DIL_MAX
    assert DILATIONS[0] == 1 and n_max == Q_BLOCK and seq % (DIL_MID * Q_BLOCK) == 0
    nat, mid, wide = _attn_bias_tables()
    in_specs = ([pl.BlockSpec((None, seq, LANES), lambda b, hp, w=w: (w * n_pairs + hp, b, 0))
                 for w in range(3)]
                + [pl.BlockSpec((None, n_max, DIL_MAX * LANES), lambda b, hp, w=w: (w * n_pairs + hp, b, 0))
                   for w in range(3)]
                + [pl.BlockSpec(nat.shape, lambda b, hp: (0, 0, 0)),
                   pl.BlockSpec(mid.shape, lambda b, hp: (0, 0, 0)),
                   pl.BlockSpec(wide.shape, lambda b, hp: (0, 0))])
    scratch = [pltpu.VMEM((3, seq, LANES), F32) for _ in range(3)]
    return pl.pallas_call(
        functools.partial(_attn_kernel, seq=seq),
        grid=(batch, n_pairs),
        in_specs=in_specs,
        out_specs=pl.BlockSpec((seq, LANES), lambda b, hp: (b, hp)),
        out_shape=jax.ShapeDtypeStruct((t, n_pairs * LANES), BF16),
        scratch_shapes=scratch,
        compiler_params=_cparams(2),
        name="dilated_attn",
    )(qkv, qkv, qkv, qkv16, qkv16, qkv16, nat, mid, wide)


GATE_GROUP = 16
GATE_ROWS = 3 * GATE_GROUP


def _gateprep_kernel(g_ref, o_ref, orow_ref, *, n_heads):
    ch = MLSTM_CHUNK
    src = lax.broadcasted_iota(jnp.int32, (ch, ch), 0)
    dst = lax.broadcasted_iota(jnp.int32, (ch, ch), 1)
    tri_prefix = (src <= dst).astype(F32)
    tri_suffix = (src >= dst).astype(F32)
    n_c = g_ref.shape[0] // ch
    grow = lax.broadcasted_iota(jnp.int32, (n_c * GATE_GROUP, ch), 0)
    pos = lax.broadcasted_iota(jnp.int32, (n_c * GATE_GROUP, ch), 1)
    is_fwd = (grow % GATE_GROUP) < 2 * n_heads
    pad = jnp.zeros((LANES - GATE_ROWS, ch), F32)

    g = jnp.concatenate([g_ref[c * ch:(c + 1) * ch, :].T[:GATE_GROUP, :] for c in range(n_c)], axis=0)
    log_f = jnp.minimum(g, 0.0) - jnp.log(1.0 + jnp.exp(-jnp.abs(g)))
    pre = jnp.dot(log_f, tri_prefix, preferred_element_type=F32, precision=lax.Precision.HIGHEST)
    suf = jnp.dot(log_f, tri_suffix, preferred_element_type=F32, precision=lax.Precision.HIGHEST)
    b = jnp.where(is_fwd, pre, suf)
    a = pltpu.roll(g, n_heads, 0) - b
    cm = a
    step = 1
    while step < ch:
        ahead = jnp.where(pos >= step, pltpu.roll(cm, step, 1), -jnp.inf)
        behind = jnp.where(pos + step < ch, pltpu.roll(cm, ch - step, 1), -jnp.inf)
        cm = jnp.maximum(cm, jnp.where(is_fwd, ahead, behind))
        step *= 2
    for c in range(n_c):
        part = slice(c * GATE_GROUP, (c + 1) * GATE_GROUP)
        rows = jnp.concatenate([b[part], a[part], cm[part]], axis=0) * LOG2_E
        orow_ref[c] = rows
        o_ref[c * ch:(c + 1) * ch, :] = jnp.concatenate([rows, pad], axis=0).T


def _gateprep(gates, *, n_heads, tm=2048):
    t, w = gates.shape
    assert 4 * n_heads == GATE_GROUP and GATE_ROWS <= w
    cpt = tm // MLSTM_CHUNK
    return pl.pallas_call(
        functools.partial(_gateprep_kernel, n_heads=n_heads),
        grid=(t // tm,),
        in_specs=[pl.BlockSpec((tm, w), lambda i: (i, 0))],
        out_specs=[pl.BlockSpec((tm, w), lambda i: (i, 0)),
                   pl.BlockSpec((cpt, GATE_ROWS, MLSTM_CHUNK), lambda i: (i, 0, 0))],
        out_shape=[jax.ShapeDtypeStruct((t, w), F32),
                   jax.ShapeDtypeStruct((t // MLSTM_CHUNK, GATE_ROWS, MLSTM_CHUNK), F32)],
        compiler_params=_cparams(1),
        name="gateprep",
    )(gates)


CONV_HALO = 16


def _conv_silu_chunks(x_ref, w_ref, b_ref, stage_ref, emit):
    n = x_ref.shape[0]
    taps = w_ref.shape[0]
    pad = taps // 2
    zeros = jnp.zeros((CONV_HALO, x_ref.shape[1]), F32)
    stage_ref[0:CONV_HALO, :] = zeros
    stage_ref[CONV_HALO + n:, :] = zeros
    stage_ref[CONV_HALO:CONV_HALO + n, :] = x_ref[...].astype(F32)

    def body(c, carry):
        base = pl.multiple_of(c * MLSTM_CHUNK, MLSTM_CHUNK)
        acc = b_ref[...]
        for j in range(taps):
            acc = acc + stage_ref[pl.ds(base + (CONV_HALO + j - pad), MLSTM_CHUNK), :] * w_ref[j:j + 1, :]
        half = 0.5 * acc
        emit(c, half + half * jnp.tanh(half))
        return carry

    lax.fori_loop(0, n // MLSTM_CHUNK, body, 0, unroll=2)


def _mlstm_kernel(mq_ref, mk_ref, mv_ref, mo_ref, gc_ref, gr_ref, cwq_ref, cwk_ref, cbq_ref, cbk_ref,
                  hg_ref, o_ref, q_s, kt_s, vx_s, gc_s, sc_s, hf_s, hb_s, stage_s, *, n_heads):
    head = pl.program_id(1)
    seq, dh = q_s.shape
    ch = MLSTM_CHUNK
    n_chunks = seq // ch
    def emit_q(c, y):
        q_s[pl.ds(pl.multiple_of(c * ch, ch), ch), :] = y.astype(q_s.dtype)

    def emit_k(c, y):
        kt_s[:, pl.ds(pl.multiple_of(c * ch, ch), ch)] = (y * (dh ** -0.5)).astype(kt_s.dtype).T

    _conv_silu_chunks(mq_ref, cwq_ref, cbq_ref, stage_s, emit_q)
    _conv_silu_chunks(mk_ref, cwk_ref, cbk_ref, stage_s, emit_k)
    vx_s[:, :dh] = mv_ref[...]
    vx_s[:, dh:] = jnp.ones((seq, dh), vx_s.dtype)
    gc_s[...] = pltpu.roll(gc_ref[...], (LANES - head) % LANES, 1)

    def gate_lane(group, d):
        return group * GATE_GROUP + (2 * d + 1) * n_heads

    for d in range(2):
        lane_b, lane_cm = gate_lane(0, d), gate_lane(2, d)
        m_run = jnp.zeros((1, 2 * dh), F32)
        for step in range(n_chunks):
            c = step if d == 0 else n_chunks - 1 - step
            last = c * ch + (ch - 1 if d == 0 else 0)
            b_last = jnp.broadcast_to(gc_s[last:last + 1, lane_b:lane_b + 1], (1, 2 * dh))
            a_max = jnp.broadcast_to(gc_s[last:last + 1, lane_cm:lane_cm + 1], (1, 2 * dh))
            w_max = jnp.maximum(m_run, a_max)
            sc_s[d, 0, c:c + 1, :] = m_run
            sc_s[d, 1, c:c + 1, :] = w_max
            sc_s[d, 2, c:c + 1, :] = jnp.exp2(m_run - w_max)
            m_run = b_last + w_max

    row = lax.broadcasted_iota(jnp.int32, (ch, ch), 0)
    col = lax.broadcasted_iota(jnp.int32, (ch, ch), 1)

    def direction(c, d, mask, state, h_store):
        lane_b, lane_cm = gate_lane(0, d), gate_lane(2, d)
        rs = pl.multiple_of(c * ch, ch)
        qc = q_s[pl.ds(rs, ch), :]
        ktc = kt_s[:, pl.ds(rs, ch)]
        vx = vx_s[pl.ds(rs, ch), :]
        a_row = gr_ref[c, pl.ds(gate_lane(1, d) + head, 1), :]
        m_row = sc_s[d, 0, pl.ds(c, 1), :]
        w_max = sc_s[d, 1, pl.ds(c, 1), :]
        decay = sc_s[d, 2, pl.ds(c, 1), :]
        g = gc_s[pl.ds(rs, ch), :]
        b_bc = jnp.broadcast_to(g[:, lane_b:lane_b + 1], (ch, dh))
        cm_bc = jnp.broadcast_to(g[:, lane_cm:lane_cm + 1], (ch, dh))
        mm = jnp.maximum(m_row[:, :dh], cm_bc)
        w = jnp.where(mask, jnp.exp2(a_row - mm), 0.0)
        s = jnp.dot(qc, ktc, preferred_element_type=F32)
        inter = jnp.dot(qc, state.astype(qc.dtype), preferred_element_type=F32)
        kw_t = (ktc.astype(F32) * jnp.exp2(a_row - w_max[:, :ch])).astype(vx.dtype)
        new_state = decay * state + jnp.dot(kw_t, vx, preferred_element_type=F32)
        intra = jnp.dot((s * w).astype(vx.dtype), vx, preferred_element_type=F32)
        g_int = jnp.exp2(m_row[:, :dh] - mm)
        num = intra[:, :dh] + g_int * inter[:, :dh]
        den = intra[:, dh:] + g_int * inter[:, dh:]
        h_store[pl.ds(rs, ch), :] = num / jnp.maximum(jnp.abs(den), jnp.exp2(-(b_bc + mm)))
        return new_state

    def step(j, carry):
        fwd, bwd = carry
        fwd = direction(j, 0, col <= row, fwd, hf_s)
        bwd = direction(n_chunks - 1 - j, 1, col >= row, bwd, hb_s)
        return fwd, bwd

    zero = jnp.zeros((dh, 2 * dh), F32)
    lax.fori_loop(0, n_chunks, step, (zero, zero), unroll=16)

    hm = hf_s[...] + hb_s[...]
    hm = hm * lax.rsqrt(jnp.mean(hm * hm, axis=-1, keepdims=True) + NORM_EPS)
    o_ref[...] = (hm * hg_ref[...] * jax.nn.sigmoid(mo_ref[...].astype(F32))).astype(o_ref.dtype)


def _mlstm(mall, gc, gr, conv_w, conv_b, head_g, *, batch, seq, n_heads):
    t = mall.shape[0]
    dh = mall.shape[1] // (4 * n_heads)
    taps = conv_w.shape[0]
    n_chunks = seq // MLSTM_CHUNK
    assert dh == MLSTM_CHUNK == LANES

    def col_block(offset):
        return pl.BlockSpec((seq, dh), lambda b, h, offset=offset: (b, offset * n_heads + h))

    in_specs = [col_block(0), col_block(1), col_block(2), col_block(3),
                pl.BlockSpec((seq, LANES), lambda b, h: (b, 0)),
                pl.BlockSpec((n_chunks, GATE_ROWS, MLSTM_CHUNK), lambda b, h: (b, 0, 0)),
                pl.BlockSpec((taps, dh), lambda b, h: (0, h)),
                pl.BlockSpec((taps, dh), lambda b, h: (0, n_heads + h)),
                pl.BlockSpec((1, dh), lambda b, h: (0, h)),
                pl.BlockSpec((1, dh), lambda b, h: (0, n_heads + h)),
                pl.BlockSpec((1, dh), lambda b, h: (0, h))]
    scratch = [pltpu.VMEM((seq, dh), BF16),
               pltpu.VMEM((dh, seq), BF16),
               pltpu.VMEM((seq, 2 * dh), BF16),
               pltpu.VMEM((seq, LANES), F32),
               pltpu.VMEM((2, 3, n_chunks, 2 * dh), F32),
               pltpu.VMEM((seq, dh), F32), pltpu.VMEM((seq, dh), F32),
               pltpu.VMEM((seq + 2 * CONV_HALO, dh), F32)]
    return pl.pallas_call(
        functools.partial(_mlstm_kernel, n_heads=n_heads),
        grid=(batch, n_heads),
        in_specs=in_specs,
        out_specs=pl.BlockSpec((seq, dh), lambda b, h: (b, h)),
        out_shape=jax.ShapeDtypeStruct((t, n_heads * dh), BF16),
        scratch_shapes=scratch,
        compiler_params=_cparams(2),
        name="mlstm",
    )(mall, mall, mall, mall, gc, gr, conv_w, conv_w, conv_b, conv_b, head_g)


def _outproj_kernel(a_ref, m_ref, w_ref, x_ref, gpost_ref, gpre_ref, x1_ref, hn_ref):
    ka = a_ref.shape[1]
    mixed = (jnp.dot(a_ref[...], w_ref[:ka, :], preferred_element_type=F32)
             + jnp.dot(m_ref[...], w_ref[ka:, :], preferred_element_type=F32))
    x1 = x_ref[...] + _rms(mixed, gpost_ref[...])
    x1_ref[...] = x1
    hn_ref[...] = _rms(x1, gpre_ref[...]).astype(hn_ref.dtype)


def _outproj(attn, ml, w_out, layer, x, g_post, g_pre, tm=1024):
    t, d = x.shape
    ka, km = attn.shape[1], ml.shape[1]
    return pl.pallas_call(
        _outproj_kernel,
        grid=(t // tm,),
        in_specs=[pl.BlockSpec((tm, ka), lambda i: (i, 0)),
                  pl.BlockSpec((tm, km), lambda i: (i, 0)),
                  pl.BlockSpec((None, ka + km, d), lambda i: (layer, 0, 0)),
                  pl.BlockSpec((tm, d), lambda i: (i, 0)),
                  pl.BlockSpec((1, d), lambda i: (0, 0)),
                  pl.BlockSpec((1, d), lambda i: (0, 0))],
        out_specs=[pl.BlockSpec((tm, d), lambda i: (i, 0)),
                   pl.BlockSpec((tm, d), lambda i: (i, 0))],
        out_shape=[jax.ShapeDtypeStruct((t, d), F32), jax.ShapeDtypeStruct((t, d), BF16)],
        compiler_params=_cparams(1),
        name="outproj",
    )(attn, ml, w_out, x, g_post.reshape(1, d), g_pre.reshape(1, d))


FFN_HALO = 16
FFN_CHUNK = 256
FFN_STAGES = 4


def _ffn_kernel(prev_ref, main_ref, next_ref, x1_ref, wup_ref, wdn_ref, cw_ref, cb_ref,
                gpost_ref, x2_ref, *scratch, tiles_per_seq):
    ug_refs = scratch[:FFN_STAGES]
    uv_refs = scratch[FFN_STAGES:2 * FFN_STAGES]
    acc_ref = scratch[-1]
    i = pl.program_id(0)
    tm = main_ref.shape[0]
    d_ff = wdn_ref.shape[0]
    pos = i % tiles_per_seq
    prev = jnp.where(pos == 0, jnp.zeros_like(prev_ref[...]), prev_ref[...])
    nxt = jnp.where(pos == tiles_per_seq - 1, jnp.zeros_like(next_ref[...]), next_ref[...])
    lhs = jnp.concatenate([prev, main_ref[...], nxt], axis=0)
    taps = cw_ref.shape[0]

    row0 = pl.multiple_of(jnp.minimum(i, 0), FFN_HALO)

    def stage(u_ref, u):
        for k in range(FFN_CHUNK // LANES):
            u_ref[k] = u[:, k * LANES:(k + 1) * LANES]

    def conv(u_ref, k, c0):
        cols = slice(c0 + k * LANES, c0 + (k + 1) * LANES)
        out = cb_ref[:, cols]
        for j in range(taps):
            start = FFN_HALO + j - taps // 2
            out = out + u_ref[k, pl.ds(row0 + start, tm), :] * cw_ref[j:j + 1, cols]
        return out

    gelu_c0 = float(np.sqrt(2.0 / np.pi))
    gelu_c1 = float(np.sqrt(2.0 / np.pi) * 0.044715)
    n_chunks = d_ff // FFN_CHUNK

    def up_project(c):
        c0 = c * FFN_CHUNK
        stage(ug_refs[c % FFN_STAGES], jnp.dot(lhs, wup_ref[:, c0:c0 + FFN_CHUNK],
                                               preferred_element_type=F32))
        stage(uv_refs[c % FFN_STAGES], jnp.dot(lhs, wup_ref[:, d_ff + c0:d_ff + c0 + FFN_CHUNK],
                                               preferred_element_type=F32))

    for c in range(FFN_STAGES - 1):
        up_project(c)
    for c in range(n_chunks):
        c0 = c * FFN_CHUNK
        slot = c % FFN_STAGES
        if c + FFN_STAGES - 1 < n_chunks:
            up_project(c + FFN_STAGES - 1)
        acts = []
        for k in range(FFN_CHUNK // LANES):
            gate = conv(ug_refs[slot], k, c0)
            val = conv(uv_refs[slot], k, d_ff + c0)
            half = (0.5 * gate) * val
            inner = gate * (gelu_c0 + gelu_c1 * (gate * gate))
            acts.append((half + half * jnp.tanh(inner)).astype(lhs.dtype))
        act = jnp.concatenate(acts, axis=1)
        part = jnp.dot(act, wdn_ref[c0:c0 + FFN_CHUNK, :], preferred_element_type=F32)
        if c == 0:
            acc_ref[...] = part
        else:
            acc_ref[...] += part

    x2_ref[...] = x1_ref[...] + _rms(acc_ref[...], gpost_ref[...])


def _ffn(hn, x1, w_up, w_down, layer, conv_w, conv_b, g_post, *, seq, tm=512):
    t, d = x1.shape
    d_ff = w_down.shape[1]
    tiles_per_seq = seq // tm
    halo_per_tile = tm // FFN_HALO
    n_halo_blocks = t // FFN_HALO
    return pl.pallas_call(
        functools.partial(_ffn_kernel, tiles_per_seq=tiles_per_seq),
        grid=(t // tm,),
        in_specs=[pl.BlockSpec((FFN_HALO, d), lambda i: (jnp.maximum(i * halo_per_tile - 1, 0), 0)),
                  pl.BlockSpec((tm, d), lambda i: (i, 0)),
                  pl.BlockSpec((FFN_HALO, d),
                               lambda i: (jnp.minimum((i + 1) * halo_per_tile, n_halo_blocks - 1), 0)),
                  pl.BlockSpec((tm, d), lambda i: (i, 0)),
                  pl.BlockSpec((None, d, 2 * d_ff), lambda i: (layer, 0, 0), pipeline_mode=pl.Buffered(1)),
                  pl.BlockSpec((None, d_ff, d), lambda i: (layer, 0, 0), pipeline_mode=pl.Buffered(1)),
                  pl.BlockSpec(conv_w.shape, lambda i: (0, 0)),
                  pl.BlockSpec((1, 2 * d_ff), lambda i: (0, 0)),
                  pl.BlockSpec((1, d), lambda i: (0, 0))],
        out_specs=pl.BlockSpec((tm, d), lambda i: (i, 0)),
        out_shape=jax.ShapeDtypeStruct((t, d), F32),
        scratch_shapes=([pltpu.VMEM((FFN_CHUNK // LANES, tm + 2 * FFN_HALO, LANES), F32)
                         for _ in range(2 * FFN_STAGES)]
                        + [pltpu.VMEM((tm, d), F32)]),
        compiler_params=_cparams(1),
        name="convffn",
    )(hn, hn, hn, x1, w_up, w_down, conv_w, conv_b.reshape(1, -1), g_post.reshape(1, d))


def _rotary_tables(seq):
    half = ATTN_HEAD_DIM // 2
    inv_freq = ROPE_THETA ** (-jnp.arange(0, ATTN_HEAD_DIM, 2, dtype=F32) / ATTN_HEAD_DIM)
    ang = jnp.arange(seq, dtype=F32)[:, None] * inv_freq[None, :]
    cos, sin = jnp.cos(ang), jnp.sin(ang)
    reps = LANES // ATTN_HEAD_DIM
    cos_t = jnp.tile(jnp.concatenate([cos, cos], axis=-1), (1, reps))
    sin_t = jnp.tile(jnp.concatenate([-sin, sin], axis=-1), (1, reps))
    return cos_t, sin_t


def kernel(x, mix_pre_g, w_in, mlstm_conv_w, mlstm_conv_b, mlstm_gate_b, mlstm_head_g, w_out,
           mix_post_g, ffn_pre_g, w_up, ffn_conv_w, ffn_conv_b, w_down, ffn_post_g):
    batch, seq, d = x.shape
    depth = w_in.shape[0]
    t = batch * seq
    n_heads = MLSTM_HEADS
    n_gates = mlstm_gate_b.shape[1]
    mix_w = mlstm_head_g.shape[1]
    attn_w = w_out.shape[1] - mix_w
    assert n_gates == 4 * n_heads and n_gates <= LANES
    assert w_in.shape[2] == 3 * attn_w + 4 * mix_w + n_gates

    cos_t, sin_t = _rotary_tables(seq)
    xf = x.reshape(t, d)
    w_out_b, w_up_b, w_down_b = w_out.astype(BF16), w_up.astype(BF16), w_down.astype(BF16)
    for l in range(depth):
        gate_b = jnp.pad(mlstm_gate_b[l], (0, LANES - n_gates)).reshape(1, LANES)
        qkv, qkv16, mall, gates = _inproj(xf, mix_pre_g[l], w_in, l, gate_b, cos_t, sin_t,
                                          attn_w=attn_w, mix_w=mix_w, seq=seq)
        attn = _attention(qkv, qkv16, batch=batch, seq=seq)
        gc, gr = _gateprep(gates, n_heads=n_heads)
        ml = _mlstm(mall, gc, gr, mlstm_conv_w[l], mlstm_conv_b[l].reshape(1, -1),
                    mlstm_head_g[l].reshape(1, -1), batch=batch, seq=seq, n_heads=n_heads)
        x1, hn2 = _outproj(attn, ml, w_out_b, l, xf, mix_post_g[l], ffn_pre_g[l])
        xf = _ffn(hn2, x1, w_up_b, w_down_b, l, ffn_conv_w[l],
                  ffn_conv_b[l], ffn_post_g[l], seq=seq)
    return xf.reshape(batch, seq, d)
```

```python
import functools

import numpy as np
import jax
import jax.numpy as jnp
from jax import lax
from jax.experimental import pallas as pl
from jax.experimental.pallas import tpu as pltpu

F32 = jnp.float32
BF16 = jnp.bfloat16

ATTN_HEAD_DIM = 64
ATTN_HALF_WINDOW = 64
DILATIONS = (1, 4, 16)
MLSTM_HEADS = 4
MLSTM_CHUNK = 128
ROPE_THETA = 10000.0
NORM_EPS = 1e-6
NEG_INF = -1e30

LANES = 128
Q_BLOCK = 128
K_WINDOW = Q_BLOCK + 2 * ATTN_HALF_WINDOW
DIL_MID, DIL_MAX = DILATIONS[1], DILATIONS[2]
DIL_RATIO = DIL_MAX // DIL_MID
LOG2_E = float(np.log2(np.e))
ATTN_Q_SCALE = float(ATTN_HEAD_DIM ** -0.5) * LOG2_E
VMEM_LIMIT_BYTES = 56 * 1024 * 1024


def _cparams(n_grid_dims):
    return pltpu.CompilerParams(
        dimension_semantics=("arbitrary",) * n_grid_dims,
        vmem_limit_bytes=VMEM_LIMIT_BYTES)


def _rms(x, g):
    return x * lax.rsqrt(jnp.mean(x * x, axis=-1, keepdims=True) + NORM_EPS) * g


def _inproj_kernel(x_ref, g_ref, w32_ref, gb_ref, cos_ref, sin_ref,
                   qkv_ref, qkv16_ref, mall_ref, gates_ref, w_ref, wg_ref, *stage_refs, attn_w, mix_w):
    n_main = w_ref.shape[1]

    @pl.when(pl.program_id(0) == 0)
    def _():
        w_ref[...] = w32_ref[:, :n_main].astype(w_ref.dtype)
        wg_ref[...] = jnp.zeros(wg_ref.shape, wg_ref.dtype)
        n_gates = w32_ref.shape[1] - n_main
        wg_ref[:, :n_gates] = w32_ref[:, n_main:].astype(wg_ref.dtype)

    hn = _rms(x_ref[...], g_ref[...]).astype(w_ref.dtype)
    cos = cos_ref[...]
    sin = sin_ref[...]
    tm = hn.shape[0]
    lane = lax.broadcasted_iota(jnp.int32, (tm, LANES), 1)
    first_half = (lane % ATTN_HEAD_DIM) < (ATTN_HEAD_DIM // 2)
    n_pairs = attn_w // LANES
    base = 3 * attn_w

    def mix_group(grp):
        res = jnp.dot(hn, w_ref[:, base + grp * mix_w: base + (grp + 1) * mix_w],
                      preferred_element_type=F32)
        mall_ref[:, grp * mix_w:(grp + 1) * mix_w] = res.astype(mall_ref.dtype)

    for grp in range(3):
        res = jnp.dot(hn, w_ref[:, grp * attn_w:(grp + 1) * attn_w],
                      preferred_element_type=F32)
        mix_group(grp)
        for hp in range(n_pairs):
            xs = res[:, hp * LANES:(hp + 1) * LANES]
            if grp < 2:
                rot = jnp.where(first_half,
                                pltpu.roll(xs, LANES - ATTN_HEAD_DIM // 2, 1),
                                pltpu.roll(xs, ATTN_HEAD_DIM // 2, 1))
                xs = xs * cos + rot * sin
            if grp == 0:
                xs = xs * ATTN_Q_SCALE
            slab = grp * n_pairs + hp
            qkv_ref[slab] = xs.astype(qkv_ref.dtype)
            stage_ref = stage_refs[slab]
            stage_ref[...] = xs
            for r in range(DIL_MAX):
                qkv16_ref[slab, :, r * LANES:(r + 1) * LANES] = (
                    stage_ref[pl.ds(r, tm // DIL_MAX, stride=DIL_MAX), :].astype(qkv16_ref.dtype))
    mix_group(3)
    gates_ref[...] = jnp.dot(hn, wg_ref[...], preferred_element_type=F32) + gb_ref[...]


def _inproj(x, g_pre, w_in, layer, gate_b, cos, sin, *, attn_w, mix_w, seq, tm=512):
    t, d = x.shape
    n_pairs = attn_w // LANES
    n_cols = w_in.shape[2]
    n_main = 3 * attn_w + 4 * mix_w
    tiles_per_seq = seq // tm
    kern = functools.partial(_inproj_kernel, attn_w=attn_w, mix_w=mix_w)
    return pl.pallas_call(
        kern,
        grid=(t // tm,),
        in_specs=[pl.BlockSpec((tm, d), lambda i: (i, 0)),
                  pl.BlockSpec((1, d), lambda i: (0, 0)),
                  pl.BlockSpec((None, d, n_cols), lambda i: (layer, 0, 0), pipeline_mode=pl.Buffered(1)),
                  pl.BlockSpec((1, LANES), lambda i: (0, 0)),
                  pl.BlockSpec((tm, LANES), lambda i: (i % tiles_per_seq, 0)),
                  pl.BlockSpec((tm, LANES), lambda i: (i % tiles_per_seq, 0))],
        out_specs=[pl.BlockSpec((3 * n_pairs, tm, LANES), lambda i: (0, i, 0)),
                   pl.BlockSpec((3 * n_pairs, tm // DIL_MAX, DIL_MAX * LANES), lambda i: (0, i, 0)),
                   pl.BlockSpec((tm, 4 * mix_w), lambda i: (i, 0)),
                   pl.BlockSpec((tm, LANES), lambda i: (i, 0))],
        out_shape=[jax.ShapeDtypeStruct((3 * n_pairs, t, LANES), BF16),
                   jax.ShapeDtypeStruct((3 * n_pairs, t // DIL_MAX, DIL_MAX * LANES), BF16),
                   jax.ShapeDtypeStruct((t, 4 * mix_w), BF16),
                   jax.ShapeDtypeStruct((t, LANES), F32)],
        scratch_shapes=([pltpu.VMEM((d, n_main), BF16), pltpu.VMEM((d, LANES), BF16)]
                        + [pltpu.VMEM((tm, LANES), F32) for _ in range(3 * n_pairs)]),
        compiler_params=_cparams(1),
        name="inproj",
    )(x, g_pre.reshape(1, d), w_in, gate_b, cos, sin)


def _head_masks(rows, dtype):
    lane = lax.broadcasted_iota(jnp.int32, (rows, LANES), 1)
    h0 = jnp.where(lane < ATTN_HEAD_DIM, 1.0, 0.0).astype(dtype)
    return h0, (1.0 - h0.astype(F32)).astype(dtype)


def _attn_scores(q, k, bias):
    q0, q1 = _head_masks(Q_BLOCK, q.dtype)
    q_st = jnp.concatenate([q * q0, q * q1], axis=0)
    s = lax.dot_general(q_st, k, (((1,), (1,)), ((), ())), preferred_element_type=F32)
    return s + jnp.concatenate([bias, bias], axis=0)


def _attn_finish(s, v):
    v0, v1 = _head_masks(v.shape[0], v.dtype)
    m = jnp.max(s, axis=-1, keepdims=True)
    p = jnp.exp2(s - m).astype(v.dtype)
    p_cat = jnp.concatenate([p[:Q_BLOCK], p[Q_BLOCK:]], axis=1)
    v_ext = jnp.concatenate([jnp.concatenate([v * v0, v0], axis=1),
                             jnp.concatenate([v * v1, v1], axis=1)], axis=0)
    o = jnp.dot(p_cat, v_ext, preferred_element_type=F32)
    lane_o = lax.broadcasted_iota(jnp.int32, (Q_BLOCK, LANES), 1)
    m_lanes = jnp.where(lane_o < ATTN_HEAD_DIM, m[:Q_BLOCK], m[Q_BLOCK:])
    return o[:, :LANES], o[:, LANES:], m_lanes


def _attn_kernel(qn, kn, vn, q16, k16, v16, bias1_ref, bias4_ref, bias16_ref, o_ref,
                 acc_ref, den_ref, max_ref, *, seq):
    n_blocks = seq // Q_BLOCK
    n_mid = (seq // DIL_MID) // Q_BLOCK
    n_max = seq // DIL_MAX
    q_rows = Q_BLOCK // DIL_RATIO
    k_rows = K_WINDOW // DIL_RATIO

    def edge_table(blk, n):
        return jnp.where(blk > 0, 1, 0) + jnp.where(blk == n - 1, 1, 0)

    def nat_window(i):
        qs = pl.multiple_of(i * Q_BLOCK, Q_BLOCK)
        ks = pl.multiple_of(jnp.clip(qs - ATTN_HALF_WINDOW, 0, seq - K_WINDOW), ATTN_HALF_WINDOW)
        return qs, ks

    def mid_window(i):
        cls, blk = i // n_mid, i % n_mid
        r0 = pl.multiple_of(blk * q_rows, q_rows)
        k0 = pl.multiple_of(jnp.clip(r0 - ATTN_HALF_WINDOW // DIL_RATIO, 0, n_max - k_rows),
                            ATTN_HALF_WINDOW // DIL_RATIO)
        lanes = [pl.ds(pl.multiple_of((cls + DIL_MID * m) * LANES, LANES), LANES) for m in range(DIL_RATIO)]
        return cls, blk, r0, k0, lanes

    def gather(ref, r0, rows, lanes):
        return jnp.concatenate([ref[pl.ds(r0, rows), ln] for ln in lanes], axis=0)

    def store(branch, start, size, stride, vals, rows=slice(None)):
        for ref, val in zip((acc_ref, den_ref, max_ref), vals):
            idx = pl.ds(start, size) if stride == 1 else pl.ds(start, size, stride=stride)
            ref[branch, idx, :] = val[rows]

    def body(i, carry):
        qs, ks = nat_window(i)
        s = _attn_scores(qn[pl.ds(qs, Q_BLOCK), :], kn[pl.ds(ks, K_WINDOW), :],
                         bias1_ref[edge_table(i, n_blocks)])
        store(0, qs, Q_BLOCK, 1, _attn_finish(s, vn[pl.ds(ks, K_WINDOW), :]))

        cls, blk, r0, k0, lanes = mid_window(i)
        s = _attn_scores(gather(q16, r0, q_rows, lanes), gather(k16, k0, k_rows, lanes),
                         bias4_ref[edge_table(blk, n_mid)])
        res = _attn_finish(s, gather(v16, k0, k_rows, lanes))
        for m in range(DIL_RATIO):
            store(1, DIL_MAX * r0 + DIL_MID * m + cls, q_rows, DIL_MAX, res,
                  rows=slice(m * q_rows, (m + 1) * q_rows))

        ln = pl.ds(pl.multiple_of(i * LANES, LANES), LANES)
        s = _attn_scores(q16[:, ln], k16[:, ln], bias16_ref[...])
        store(2, i, n_max, DIL_MAX, _attn_finish(s, v16[:, ln]))
        return carry

    lax.fori_loop(0, n_blocks, body, 0, unroll=16)

    def combine(i, carry):
        rows = pl.ds(pl.multiple_of(i * Q_BLOCK, Q_BLOCK), Q_BLOCK)
        ms = [max_ref[b, rows, :] for b in range(3)]
        mx = jnp.maximum(jnp.maximum(ms[0], ms[1]), ms[2])
        ws = [jnp.exp2(m - mx) for m in ms]
        num = ws[0] * acc_ref[0, rows, :] + ws[1] * acc_ref[1, rows, :] + ws[2] * acc_ref[2, rows, :]
        den = ws[0] * den_ref[0, rows, :] + ws[1] * den_ref[1, rows, :] + ws[2] * den_ref[2, rows, :]
        o_ref[rows, :] = (num / den).astype(o_ref.dtype)
        return carry

    lax.fori_loop(0, n_blocks, combine, 0, unroll=4)


def _attn_bias_tables():
    hw = ATTN_HALF_WINDOW
    row = np.arange(Q_BLOCK)[:, None]
    col = np.arange(K_WINDOW)[None, :]

    def bias(off):
        return np.where(np.abs(off) <= hw, 0.0, NEG_INF).astype(np.float32)

    nat = np.stack([bias(col - shift - row) for shift in (0, hw, 2 * hw)])
    q_rows, k_rows = Q_BLOCK // DIL_RATIO, K_WINDOW // DIL_RATIO
    qpos = DIL_RATIO * (row % q_rows) + row // q_rows
    kpos = DIL_RATIO * (col % k_rows) + col // k_rows
    mid = np.stack([bias(kpos - shift - qpos) for shift in (0, hw, 2 * hw)])
    wide = bias(np.arange(Q_BLOCK)[None, :] - row)
    return nat, mid, wide


def _attention(qkv, qkv16, *, batch, seq):
    n3, t, _ = qkv.shape
    n_pairs = n3 // 3
    n_max = seq // DIL_MAX
    assert DILATIONS[0] == 1 and n_max == Q_BLOCK and seq % (DIL_MID * Q_BLOCK) == 0
    nat, mid, wide = _attn_bias_tables()
    in_specs = ([pl.BlockSpec((None, seq, LANES), lambda b, hp, w=w: (w * n_pairs + hp, b, 0))
                 for w in range(3)]
                + [pl.BlockSpec((None, n_max, DIL_MAX * LANES), lambda b, hp, w=w: (w * n_pairs + hp, b, 0))
                   for w in range(3)]
                + [pl.BlockSpec(nat.shape, lambda b, hp: (0, 0, 0)),
                   pl.BlockSpec(mid.shape, lambda b, hp: (0, 0, 0)),
                   pl.BlockSpec(wide.shape, lambda b, hp: (0, 0))])
    scratch = [pltpu.VMEM((3, seq, LANES), F32) for _ in range(3)]
    return pl.pallas_call(
        functools.partial(_attn_kernel, seq=seq),
        grid=(batch, n_pairs),
        in_specs=in_specs,
        out_specs=pl.BlockSpec((seq, LANES), lambda b, hp: (b, hp)),
        out_shape=jax.ShapeDtypeStruct((t, n_pairs * LANES), BF16),
        scratch_shapes=scratch,
        compiler_params=_cparams(2),
        name="dilated_attn",
    )(qkv, qkv, qkv, qkv16, qkv16, qkv16, nat, mid, wide)


GATE_GROUP = 16
GATE_ROWS = 3 * GATE_GROUP


def _gateprep_kernel(g_ref, o_ref, orow_ref, *, n_heads):
    ch = MLSTM_CHUNK
    src = lax.broadcasted_iota(jnp.int32, (ch, ch), 0)
    dst = lax.broadcasted_iota(jnp.int32, (ch, ch), 1)
    tri_prefix = (src <= dst).astype(F32)
    tri_suffix = (src >= dst).astype(F32)
    n_c = g_ref.shape[0] // ch
    grow = lax.broadcasted_iota(jnp.int32, (n_c * GATE_GROUP, ch), 0)
    pos = lax.broadcasted_iota(jnp.int32, (n_c * GATE_GROUP, ch), 1)
    is_fwd = (grow % GATE_GROUP) < 2 * n_heads
    pad = jnp.zeros((LANES - GATE_ROWS, ch), F32)

    g = jnp.concatenate([g_ref[c * ch:(c + 1) * ch, :].T[:GATE_GROUP, :] for c in range(n_c)], axis=0)
    log_f = jnp.minimum(g, 0.0) - jnp.log(1.0 + jnp.exp(-jnp.abs(g)))
    pre = jnp.dot(log_f, tri_prefix, preferred_element_type=F32, precision=lax.Precision.HIGHEST)
    suf = jnp.dot(log_f, tri_suffix, preferred_element_type=F32, precision=lax.Precision.HIGHEST)
    b = jnp.where(is_fwd, pre, suf)
    a = pltpu.roll(g, n_heads, 0) - b
    cm = a
    step = 1
    while step < ch:
        ahead = jnp.where(pos >= step, pltpu.roll(cm, step, 1), -jnp.inf)
        behind = jnp.where(pos + step < ch, pltpu.roll(cm, ch - step, 1), -jnp.inf)
        cm = jnp.maximum(cm, jnp.where(is_fwd, ahead, behind))
        step *= 2
    for c in range(n_c):
        part = slice(c * GATE_GROUP, (c + 1) * GATE_GROUP)
        rows = jnp.concatenate([b[part], a[part], cm[part]], axis=0) * LOG2_E
        orow_ref[c] = rows
        o_ref[c * ch:(c + 1) * ch, :] = jnp.concatenate([rows, pad], axis=0).T


def _gateprep(gates, *, n_heads, tm=2048):
    t, w = gates.shape
    assert 4 * n_heads == GATE_GROUP and GATE_ROWS <= w
    cpt = tm // MLSTM_CHUNK
    return pl.pallas_call(
        functools.partial(_gateprep_kernel, n_heads=n_heads),
        grid=(t // tm,),
        in_specs=[pl.BlockSpec((tm, w), lambda i: (i, 0))],
        out_specs=[pl.BlockSpec((tm, w), lambda i: (i, 0)),
                   pl.BlockSpec((cpt, GATE_ROWS, MLSTM_CHUNK), lambda i: (i, 0, 0))],
        out_shape=[jax.ShapeDtypeStruct((t, w), F32),
                   jax.ShapeDtypeStruct((t // MLSTM_CHUNK, GATE_ROWS, MLSTM_CHUNK), F32)],
        compiler_params=_cparams(1),
        name="gateprep",
    )(gates)


CONV_HALO = 16


def _conv_silu_chunks(x_ref, w_ref, b_ref, stage_ref, emit):
    n = x_ref.shape[0]
    taps = w_ref.shape[0]
    pad = taps // 2
    zeros = jnp.zeros((CONV_HALO, x_ref.shape[1]), F32)
    stage_ref[0:CONV_HALO, :] = zeros
    stage_ref[CONV_HALO + n:, :] = zeros
    stage_ref[CONV_HALO:CONV_HALO + n, :] = x_ref[...].astype(F32)

    def body(c, carry):
        base = pl.multiple_of(c * MLSTM_CHUNK, MLSTM_CHUNK)
        acc = b_ref[...]
        for j in range(taps):
            acc = acc + stage_ref[pl.ds(base + (CONV_HALO + j - pad), MLSTM_CHUNK), :] * w_ref[j:j + 1, :]
        half = 0.5 * acc
        emit(c, half + half * jnp.tanh(half))
        return carry

    lax.fori_loop(0, n // MLSTM_CHUNK, body, 0, unroll=8)


def _mlstm_kernel(mq_ref, mk_ref, mv_ref, mo_ref, gc_ref, gr_ref, cwq_ref, cwk_ref, cbq_ref, cbk_ref,
                  hg_ref, o_ref, q_s, kt_s, vx_s, gc_s, sc_s, hf_s, hb_s, stage_s, *, n_heads):
    head = pl.program_id(1)
    seq, dh = q_s.shape
    ch = MLSTM_CHUNK
    n_chunks = seq // ch
    def emit_q(c, y):
        q_s[pl.ds(pl.multiple_of(c * ch, ch), ch), :] = y.astype(q_s.dtype)

    def emit_k(c, y):
        kt_s[:, pl.ds(pl.multiple_of(c * ch, ch), ch)] = (y * (dh ** -0.5)).astype(kt_s.dtype).T

    _conv_silu_chunks(mq_ref, cwq_ref, cbq_ref, stage_s, emit_q)
    _conv_silu_chunks(mk_ref, cwk_ref, cbk_ref, stage_s, emit_k)
    vx_s[:, :dh] = mv_ref[...]
    vx_s[:, dh:] = jnp.ones((seq, dh), vx_s.dtype)
    gc_s[...] = pltpu.roll(gc_ref[...], (LANES - head) % LANES, 1)

    def gate_lane(group, d):
        return group * GATE_GROUP + (2 * d + 1) * n_heads

    for d in range(2):
        lane_b, lane_cm = gate_lane(0, d), gate_lane(2, d)
        m_run = jnp.zeros((1, 2 * dh), F32)
        for step in range(n_chunks):
            c = step if d == 0 else n_chunks - 1 - step
            last = c * ch + (ch - 1 if d == 0 else 0)
            b_last = jnp.broadcast_to(gc_s[last:last + 1, lane_b:lane_b + 1], (1, 2 * dh))
            a_max = jnp.broadcast_to(gc_s[last:last + 1, lane_cm:lane_cm + 1], (1, 2 * dh))
            w_max = jnp.maximum(m_run, a_max)
            sc_s[d, 0, c:c + 1, :] = m_run
            sc_s[d, 1, c:c + 1, :] = w_max
            sc_s[d, 2, c:c + 1, :] = jnp.exp2(m_run - w_max)
            m_run = b_last + w_max

    row = lax.broadcasted_iota(jnp.int32, (ch, ch), 0)
    col = lax.broadcasted_iota(jnp.int32, (ch, ch), 1)

    def direction(c, d, mask, state, h_store):
        lane_b, lane_cm = gate_lane(0, d), gate_lane(2, d)
        rs = pl.multiple_of(c * ch, ch)
        qc = q_s[pl.ds(rs, ch), :]
        ktc = kt_s[:, pl.ds(rs, ch)]
        vx = vx_s[pl.ds(rs, ch), :]
        a_row = gr_ref[c, pl.ds(gate_lane(1, d) + head, 1), :]
        m_row = sc_s[d, 0, pl.ds(c, 1), :]
        w_max = sc_s[d, 1, pl.ds(c, 1), :]
        decay = sc_s[d, 2, pl.ds(c, 1), :]
        g = gc_s[pl.ds(rs, ch), :]
        b_bc = jnp.broadcast_to(g[:, lane_b:lane_b + 1], (ch, dh))
        cm_bc = jnp.broadcast_to(g[:, lane_cm:lane_cm + 1], (ch, dh))
        mm = jnp.maximum(m_row[:, :dh], cm_bc)
        w = jnp.where(mask, jnp.exp2(a_row - mm), 0.0)
        s = jnp.dot(qc, ktc, preferred_element_type=F32)
        inter = jnp.dot(qc, state.astype(qc.dtype), preferred_element_type=F32)
        kw_t = (ktc.astype(F32) * jnp.exp2(a_row - w_max[:, :ch])).astype(vx.dtype)
        new_state = decay * state + jnp.dot(kw_t, vx, preferred_element_type=F32)
        intra = jnp.dot((s * w).astype(vx.dtype), vx, preferred_element_type=F32)
        g_int = jnp.exp2(m_row[:, :dh] - mm)
        num = intra[:, :dh] + g_int * inter[:, :dh]
        den = intra[:, dh:] + g_int * inter[:, dh:]
        h_store[pl.ds(rs, ch), :] = num / jnp.maximum(jnp.abs(den), jnp.exp2(-(b_bc + mm)))
        return new_state

    def step(j, carry):
        fwd, bwd = carry
        fwd = direction(j, 0, col <= row, fwd, hf_s)
        bwd = direction(n_chunks - 1 - j, 1, col >= row, bwd, hb_s)
        return fwd, bwd

    zero = jnp.zeros((dh, 2 * dh), F32)
    lax.fori_loop(0, n_chunks, step, (zero, zero), unroll=16)

    hm = hf_s[...] + hb_s[...]
    hm = hm * lax.rsqrt(jnp.mean(hm * hm, axis=-1, keepdims=True) + NORM_EPS)
    o_ref[...] = (hm * hg_ref[...] * jax.nn.sigmoid(mo_ref[...].astype(F32))).astype(o_ref.dtype)


def _mlstm(mall, gc, gr, conv_w, conv_b, head_g, *, batch, seq, n_heads):
    t = mall.shape[0]
    dh = mall.shape[1] // (4 * n_heads)
    taps = conv_w.shape[0]
    n_chunks = seq // MLSTM_CHUNK
    assert dh == MLSTM_CHUNK == LANES

    def col_block(offset):
        return pl.BlockSpec((seq, dh), lambda b, h, offset=offset: (b, offset * n_heads + h))

    in_specs = [col_block(0), col_block(1), col_block(2), col_block(3),
                pl.BlockSpec((seq, LANES), lambda b, h: (b, 0)),
                pl.BlockSpec((n_chunks, GATE_ROWS, MLSTM_CHUNK), lambda b, h: (b, 0, 0)),
                pl.BlockSpec((taps, dh), lambda b, h: (0, h)),
                pl.BlockSpec((taps, dh), lambda b, h: (0, n_heads + h)),
                pl.BlockSpec((1, dh), lambda b, h: (0, h)),
                pl.BlockSpec((1, dh), lambda b, h: (0, n_heads + h)),
                pl.BlockSpec((1, dh), lambda b, h: (0, h))]
    scratch = [pltpu.VMEM((seq, dh), BF16),
               pltpu.VMEM((dh, seq), BF16),
               pltpu.VMEM((seq, 2 * dh), BF16),
               pltpu.VMEM((seq, LANES), F32),
               pltpu.VMEM((2, 3, n_chunks, 2 * dh), F32),
               pltpu.VMEM((seq, dh), F32), pltpu.VMEM((seq, dh), F32),
               pltpu.VMEM((seq + 2 * CONV_HALO, dh), F32)]
    return pl.pallas_call(
        functools.partial(_mlstm_kernel, n_heads=n_heads),
        grid=(batch, n_heads),
        in_specs=in_specs,
        out_specs=pl.BlockSpec((seq, dh), lambda b, h: (b, h)),
        out_shape=jax.ShapeDtypeStruct((t, n_heads * dh), BF16),
        scratch_shapes=scratch,
        compiler_params=_cparams(2),
        name="mlstm",
    )(mall, mall, mall, mall, gc, gr, conv_w, conv_w, conv_b, conv_b, head_g)


def _outproj_kernel(a_ref, m_ref, w_ref, x_ref, gpost_ref, gpre_ref, x1_ref, hn_ref):
    ka = a_ref.shape[1]
    mixed = (jnp.dot(a_ref[...], w_ref[:ka, :], preferred_element_type=F32)
             + jnp.dot(m_ref[...], w_ref[ka:, :], preferred_element_type=F32))
    x1 = x_ref[...] + _rms(mixed, gpost_ref[...])
    x1_ref[...] = x1
    hn_ref[...] = _rms(x1, gpre_ref[...]).astype(hn_ref.dtype)


def _outproj(attn, ml, w_out, layer, x, g_post, g_pre, tm=1024):
    t, d = x.shape
    ka, km = attn.shape[1], ml.shape[1]
    return pl.pallas_call(
        _outproj_kernel,
        grid=(t // tm,),
        in_specs=[pl.BlockSpec((tm, ka), lambda i: (i, 0)),
                  pl.BlockSpec((tm, km), lambda i: (i, 0)),
                  pl.BlockSpec((None, ka + km, d), lambda i: (layer, 0, 0)),
                  pl.BlockSpec((tm, d), lambda i: (i, 0)),
                  pl.BlockSpec((1, d), lambda i: (0, 0)),
                  pl.BlockSpec((1, d), lambda i: (0, 0))],
        out_specs=[pl.BlockSpec((tm, d), lambda i: (i, 0)),
                   pl.BlockSpec((tm, d), lambda i: (i, 0))],
        out_shape=[jax.ShapeDtypeStruct((t, d), F32), jax.ShapeDtypeStruct((t, d), BF16)],
        compiler_params=_cparams(1),
        name="outproj",
    )(attn, ml, w_out, x, g_post.reshape(1, d), g_pre.reshape(1, d))


FFN_HALO = 16
FFN_CHUNK = 256
FFN_STAGES = 4


def _ffn_kernel(prev_ref, main_ref, next_ref, x1_ref, wup_ref, wdn_ref, cw_ref, cb_ref,
                gpost_ref, x2_ref, *scratch, tiles_per_seq):
    ug_refs = scratch[:FFN_STAGES]
    uv_refs = scratch[FFN_STAGES:2 * FFN_STAGES]
    acc_ref = scratch[-1]
    i = pl.program_id(0)
    tm = main_ref.shape[0]
    d_ff = wdn_ref.shape[0]
    pos = i % tiles_per_seq
    prev = jnp.where(pos == 0, jnp.zeros_like(prev_ref[...]), prev_ref[...])
    nxt = jnp.where(pos == tiles_per_seq - 1, jnp.zeros_like(next_ref[...]), next_ref[...])
    lhs = jnp.concatenate([prev, main_ref[...], nxt], axis=0)
    taps = cw_ref.shape[0]

    row0 = pl.multiple_of(jnp.minimum(i, 0), FFN_HALO)

    def stage(u_ref, u):
        for k in range(FFN_CHUNK // LANES):
            u_ref[k] = u[:, k * LANES:(k + 1) * LANES]

    def conv(u_ref, k, c0):
        cols = slice(c0 + k * LANES, c0 + (k + 1) * LANES)
        out = cb_ref[:, cols]
        for j in range(taps):
            start = FFN_HALO + j - taps // 2
            out = out + u_ref[k, pl.ds(row0 + start, tm), :] * cw_ref[j:j + 1, cols]
        return out

    gelu_c0 = float(np.sqrt(2.0 / np.pi))
    gelu_c1 = float(np.sqrt(2.0 / np.pi) * 0.044715)
    n_chunks = d_ff // FFN_CHUNK

    def up_project(c):
        c0 = c * FFN_CHUNK
        stage(ug_refs[c % FFN_STAGES], jnp.dot(lhs, wup_ref[:, c0:c0 + FFN_CHUNK],
                                               preferred_element_type=F32))
        stage(uv_refs[c % FFN_STAGES], jnp.dot(lhs, wup_ref[:, d_ff + c0:d_ff + c0 + FFN_CHUNK],
                                               preferred_element_type=F32))

    for c in range(FFN_STAGES - 1):
        up_project(c)
    for c in range(n_chunks):
        c0 = c * FFN_CHUNK
        slot = c % FFN_STAGES
        if c + FFN_STAGES - 1 < n_chunks:
            up_project(c + FFN_STAGES - 1)
        acts = []
        for k in range(FFN_CHUNK // LANES):
            gate = conv(ug_refs[slot], k, c0)
            val = conv(uv_refs[slot], k, d_ff + c0)
            half = (0.5 * gate) * val
            inner = gate * (gelu_c0 + gelu_c1 * (gate * gate))
            acts.append((half + half * jnp.tanh(inner)).astype(lhs.dtype))
        act = jnp.concatenate(acts, axis=1)
        part = jnp.dot(act, wdn_ref[c0:c0 + FFN_CHUNK, :], preferred_element_type=F32)
        if c == 0:
            acc_ref[...] = part
        else:
            acc_ref[...] += part

    x2_ref[...] = x1_ref[...] + _rms(acc_ref[...], gpost_ref[...])


def _ffn(hn, x1, w_up, w_down, layer, conv_w, conv_b, g_post, *, seq, tm=512):
    t, d = x1.shape
    d_ff = w_down.shape[1]
    tiles_per_seq = seq // tm
    halo_per_tile = tm // FFN_HALO
    n_halo_blocks = t // FFN_HALO
    return pl.pallas_call(
        functools.partial(_ffn_kernel, tiles_per_seq=tiles_per_seq),
        grid=(t // tm,),
        in_specs=[pl.BlockSpec((FFN_HALO, d), lambda i: (jnp.maximum(i * halo_per_tile - 1, 0), 0)),
                  pl.BlockSpec((tm, d), lambda i: (i, 0)),
                  pl.BlockSpec((FFN_HALO, d),
                               lambda i: (jnp.minimum((i + 1) * halo_per_tile, n_halo_blocks - 1), 0)),
                  pl.BlockSpec((tm, d), lambda i: (i, 0)),
                  pl.BlockSpec((None, d, 2 * d_ff), lambda i: (layer, 0, 0), pipeline_mode=pl.Buffered(1)),
                  pl.BlockSpec((None, d_ff, d), lambda i: (layer, 0, 0), pipeline_mode=pl.Buffered(1)),
                  pl.BlockSpec(conv_w.shape, lambda i: (0, 0)),
                  pl.BlockSpec((1, 2 * d_ff), lambda i: (0, 0)),
                  pl.BlockSpec((1, d), lambda i: (0, 0))],
        out_specs=pl.BlockSpec((tm, d), lambda i: (i, 0)),
        out_shape=jax.ShapeDtypeStruct((t, d), F32),
        scratch_shapes=([pltpu.VMEM((FFN_CHUNK // LANES, tm + 2 * FFN_HALO, LANES), F32)
                         for _ in range(2 * FFN_STAGES)]
                        + [pltpu.VMEM((tm, d), F32)]),
        compiler_params=_cparams(1),
        name="convffn",
    )(hn, hn, hn, x1, w_up, w_down, conv_w, conv_b.reshape(1, -1), g_post.reshape(1, d))


def _rotary_tables(seq):
    half = ATTN_HEAD_DIM // 2
    inv_freq = ROPE_THETA ** (-jnp.arange(0, ATTN_HEAD_DIM, 2, dtype=F32) / ATTN_HEAD_DIM)
    ang = jnp.arange(seq, dtype=F32)[:, None] * inv_freq[None, :]
    cos, sin = jnp.cos(ang), jnp.sin(ang)
    reps = LANES // ATTN_HEAD_DIM
    cos_t = jnp.tile(jnp.concatenate([cos, cos], axis=-1), (1, reps))
    sin_t = jnp.tile(jnp.concatenate([-sin, sin], axis=-1), (1, reps))
    return cos_t, sin_t


def kernel(x, mix_pre_g, w_in, mlstm_conv_w, mlstm_conv_b, mlstm_gate_b, mlstm_head_g, w_out,
           mix_post_g, ffn_pre_g, w_up, ffn_conv_w, ffn_conv_b, w_down, ffn_post_g):
    batch, seq, d = x.shape
    depth = w_in.shape[0]
    t = batch * seq
    n_heads = MLSTM_HEADS
    n_gates = mlstm_gate_b.shape[1]
    mix_w = mlstm_head_g.shape[1]
    attn_w = w_out.shape[1] - mix_w
    assert n_gates == 4 * n_heads and n_gates <= LANES
    assert w_in.shape[2] == 3 * attn_w + 4 * mix_w + n_gates

    cos_t, sin_t = _rotary_tables(seq)
    xf = x.reshape(t, d)
    w_out_b, w_up_b, w_down_b = w_out.astype(BF16), w_up.astype(BF16), w_down.astype(BF16)
    for l in range(depth):
        gate_b = jnp.pad(mlstm_gate_b[l], (0, LANES - n_gates)).reshape(1, LANES)
        qkv, qkv16, mall, gates = _inproj(xf, mix_pre_g[l], w_in, l, gate_b, cos_t, sin_t,
                                          attn_w=attn_w, mix_w=mix_w, seq=seq)
        attn = _attention(qkv, qkv16, batch=batch, seq=seq)
        gc, gr = _gateprep(gates, n_heads=n_heads)
        ml = _mlstm(mall, gc, gr, mlstm_conv_w[l], mlstm_conv_b[l].reshape(1, -1),
                    mlstm_head_g[l].reshape(1, -1), batch=batch, seq=seq, n_heads=n_heads)
        x1, hn2 = _outproj(attn, ml, w_out_b, l, xf, mix_post_g[l], ffn_pre_g[l])
        xf = _ffn(hn2, x1, w_up_b, w_down_b, l, ffn_conv_w[l],
                  ffn_conv_b[l], ffn_post_g[l], seq=seq)
    return xf.reshape(batch, seq, d)
```

```python
import functools

import numpy as np
import jax
import jax.numpy as jnp
from jax import lax
from jax.experimental import pallas as pl
from jax.experimental.pallas import tpu as pltpu

F32 = jnp.float32
BF16 = jnp.bfloat16

ATTN_HEAD_DIM = 64
ATTN_HALF_WINDOW = 64
DILATIONS = (1, 4, 16)
MLSTM_HEADS = 4
MLSTM_CHUNK = 128
ROPE_THETA = 10000.0
NORM_EPS = 1e-6
NEG_INF = -1e30

LANES = 128
Q_BLOCK = 128
K_WINDOW = Q_BLOCK + 2 * ATTN_HALF_WINDOW
DIL_MID, DIL_MAX = DILATIONS[1], DILATIONS[2]
DIL_RATIO = DIL_MAX // DIL_MID
LOG2_E = float(np.log2(np.e))
ATTN_Q_SCALE = float(ATTN_HEAD_DIM ** -0.5) * LOG2_E
VMEM_LIMIT_BYTES = 56 * 1024 * 1024


def _cparams(n_grid_dims):
    return pltpu.CompilerParams(
        dimension_semantics=("arbitrary",) * n_grid_dims,
        vmem_limit_bytes=VMEM_LIMIT_BYTES)


def _rms(x, g):
    return x * lax.rsqrt(jnp.mean(x * x, axis=-1, keepdims=True) + NORM_EPS) * g


def _inproj_kernel(x_ref, g_ref, w32_ref, gb_ref, cos_ref, sin_ref,
                   qkv_ref, qkv16_ref, mall_ref, gates_ref, w_ref, wg_ref, *stage_refs, attn_w, mix_w):
    n_main = w_ref.shape[1]

    @pl.when(pl.program_id(0) == 0)
    def _():
        w_ref[...] = w32_ref[:, :n_main].astype(w_ref.dtype)
        wg_ref[...] = jnp.zeros(wg_ref.shape, wg_ref.dtype)
        n_gates = w32_ref.shape[1] - n_main
        wg_ref[:, :n_gates] = w32_ref[:, n_main:].astype(wg_ref.dtype)

    hn = _rms(x_ref[...], g_ref[...]).astype(w_ref.dtype)
    cos = cos_ref[...]
    sin = sin_ref[...]
    tm = hn.shape[0]
    lane = lax.broadcasted_iota(jnp.int32, (tm, LANES), 1)
    first_half = (lane % ATTN_HEAD_DIM) < (ATTN_HEAD_DIM // 2)
    n_pairs = attn_w // LANES
    base = 3 * attn_w

    def mix_group(grp):
        res = jnp.dot(hn, w_ref[:, base + grp * mix_w: base + (grp + 1) * mix_w],
                      preferred_element_type=F32)
        mall_ref[:, grp * mix_w:(grp + 1) * mix_w] = res.astype(mall_ref.dtype)

    for grp in range(3):
        res = jnp.dot(hn, w_ref[:, grp * attn_w:(grp + 1) * attn_w],
                      preferred_element_type=F32)
        mix_group(grp)
        for hp in range(n_pairs):
            xs = res[:, hp * LANES:(hp + 1) * LANES]
            if grp < 2:
                rot = jnp.where(first_half,
                                pltpu.roll(xs, LANES - ATTN_HEAD_DIM // 2, 1),
                                pltpu.roll(xs, ATTN_HEAD_DIM // 2, 1))
                xs = xs * cos + rot * sin
            if grp == 0:
                xs = xs * ATTN_Q_SCALE
            slab = grp * n_pairs + hp
            qkv_ref[slab] = xs.astype(qkv_ref.dtype)
            stage_ref = stage_refs[slab]
            stage_ref[...] = xs
            for r in range(DIL_MAX):
                qkv16_ref[slab, :, r * LANES:(r + 1) * LANES] = (
                    stage_ref[pl.ds(r, tm // DIL_MAX, stride=DIL_MAX), :].astype(qkv16_ref.dtype))
    mix_group(3)
    gates_ref[...] = jnp.dot(hn, wg_ref[...], preferred_element_type=F32) + gb_ref[...]


def _inproj(x, g_pre, w_in, layer, gate_b, cos, sin, *, attn_w, mix_w, seq, tm=512):
    t, d = x.shape
    n_pairs = attn_w // LANES
    n_cols = w_in.shape[2]
    n_main = 3 * attn_w + 4 * mix_w
    tiles_per_seq = seq // tm
    kern = functools.partial(_inproj_kernel, attn_w=attn_w, mix_w=mix_w)
    return pl.pallas_call(
        kern,
        grid=(t // tm,),
        in_specs=[pl.BlockSpec((tm, d), lambda i: (i, 0)),
                  pl.BlockSpec((1, d), lambda i: (0, 0)),
                  pl.BlockSpec((None, d, n_cols), lambda i: (layer, 0, 0), pipeline_mode=pl.Buffered(1)),
                  pl.BlockSpec((1, LANES), lambda i: (0, 0)),
                  pl.BlockSpec((tm, LANES), lambda i: (i % tiles_per_seq, 0)),
                  pl.BlockSpec((tm, LANES), lambda i: (i % tiles_per_seq, 0))],
        out_specs=[pl.BlockSpec((3 * n_pairs, tm, LANES), lambda i: (0, i, 0)),
                   pl.BlockSpec((3 * n_pairs, tm // DIL_MAX, DIL_MAX * LANES), lambda i: (0, i, 0)),
                   pl.BlockSpec((tm, 4 * mix_w), lambda i: (i, 0)),
                   pl.BlockSpec((tm, LANES), lambda i: (i, 0))],
        out_shape=[jax.ShapeDtypeStruct((3 * n_pairs, t, LANES), BF16),
                   jax.ShapeDtypeStruct((3 * n_pairs, t // DIL_MAX, DIL_MAX * LANES), BF16),
                   jax.ShapeDtypeStruct((t, 4 * mix_w), BF16),
                   jax.ShapeDtypeStruct((t, LANES), F32)],
        scratch_shapes=([pltpu.VMEM((d, n_main), BF16), pltpu.VMEM((d, LANES), BF16)]
                        + [pltpu.VMEM((tm, LANES), F32) for _ in range(3 * n_pairs)]),
        compiler_params=_cparams(1),
        name="inproj",
    )(x, g_pre.reshape(1, d), w_in, gate_b, cos, sin)


def _head_masks(rows, dtype):
    lane = lax.broadcasted_iota(jnp.int32, (rows, LANES), 1)
    h0 = jnp.where(lane < ATTN_HEAD_DIM, 1.0, 0.0).astype(dtype)
    return h0, (1.0 - h0.astype(F32)).astype(dtype)


def _attn_scores(q, k, bias):
    q0, q1 = _head_masks(Q_BLOCK, q.dtype)
    q_st = jnp.concatenate([q * q0, q * q1], axis=0)
    s = lax.dot_general(q_st, k, (((1,), (1,)), ((), ())), preferred_element_type=F32)
    return s + jnp.concatenate([bias, bias], axis=0)


def _attn_finish(s, v):
    v0, v1 = _head_masks(v.shape[0], v.dtype)
    m = jnp.max(s, axis=-1, keepdims=True)
    p = jnp.exp2(s - m).astype(v.dtype)
    p_cat = jnp.concatenate([p[:Q_BLOCK], p[Q_BLOCK:]], axis=1)
    v_ext = jnp.concatenate([jnp.concatenate([v * v0, v0], axis=1),
                             jnp.concatenate([v * v1, v1], axis=1)], axis=0)
    o = jnp.dot(p_cat, v_ext, preferred_element_type=F32)
    lane_o = lax.broadcasted_iota(jnp.int32, (Q_BLOCK, LANES), 1)
    m_lanes = jnp.where(lane_o < ATTN_HEAD_DIM, m[:Q_BLOCK], m[Q_BLOCK:])
    return o[:, :LANES], o[:, LANES:], m_lanes


def _attn_kernel(qn, kn, vn, q16, k16, v16, bias1_ref, bias4_ref, bias16_ref, o_ref,
                 acc_ref, den_ref, max_ref, *, seq):
    n_blocks = seq // Q_BLOCK
    n_mid = (seq // DIL_MID) // Q_BLOCK
    n_max = seq // DIL_MAX
    q_rows = Q_BLOCK // DIL_RATIO
    k_rows = K_WINDOW // DIL_RATIO

    def edge_table(blk, n):
        return jnp.where(blk > 0, 1, 0) + jnp.where(blk == n - 1, 1, 0)

    def nat_window(i):
        qs = pl.multiple_of(i * Q_BLOCK, Q_BLOCK)
        ks = pl.multiple_of(jnp.clip(qs - ATTN_HALF_WINDOW, 0, seq - K_WINDOW), ATTN_HALF_WINDOW)
        return qs, ks

    def mid_window(i):
        cls, blk = i // n_mid, i % n_mid
        r0 = pl.multiple_of(blk * q_rows, q_rows)
        k0 = pl.multiple_of(jnp.clip(r0 - ATTN_HALF_WINDOW // DIL_RATIO, 0, n_max - k_rows),
                            ATTN_HALF_WINDOW // DIL_RATIO)
        lanes = [pl.ds(pl.multiple_of((cls + DIL_MID * m) * LANES, LANES), LANES) for m in range(DIL_RATIO)]
        return cls, blk, r0, k0, lanes

    def gather(ref, r0, rows, lanes):
        return jnp.concatenate([ref[pl.ds(r0, rows), ln] for ln in lanes], axis=0)

    def store(branch, start, size, stride, vals, rows=slice(None)):
        for ref, val in zip((acc_ref, den_ref, max_ref), vals):
            idx = pl.ds(start, size) if stride == 1 else pl.ds(start, size, stride=stride)
            ref[branch, idx, :] = val[rows]

    def body(i, carry):
        qs, ks = nat_window(i)
        s = _attn_scores(qn[pl.ds(qs, Q_BLOCK), :], kn[pl.ds(ks, K_WINDOW), :],
                         bias1_ref[edge_table(i, n_blocks)])
        store(0, qs, Q_BLOCK, 1, _attn_finish(s, vn[pl.ds(ks, K_WINDOW), :]))

        cls, blk, r0, k0, lanes = mid_window(i)
        s = _attn_scores(gather(q16, r0, q_rows, lanes), gather(k16, k0, k_rows, lanes),
                         bias4_ref[edge_table(blk, n_mid)])
        res = _attn_finish(s, gather(v16, k0, k_rows, lanes))
        for m in range(DIL_RATIO):
            store(1, DIL_MAX * r0 + DIL_MID * m + cls, q_rows, DIL_MAX, res,
                  rows=slice(m * q_rows, (m + 1) * q_rows))

        ln = pl.ds(pl.multiple_of(i * LANES, LANES), LANES)
        s = _attn_scores(q16[:, ln], k16[:, ln], bias16_ref[...])
        store(2, i, n_max, DIL_MAX, _attn_finish(s, v16[:, ln]))
        return carry

    lax.fori_loop(0, n_blocks, body, 0, unroll=16)

    def combine(i, carry):
        rows = pl.ds(pl.multiple_of(i * Q_BLOCK, Q_BLOCK), Q_BLOCK)
        ms = [max_ref[b, rows, :] for b in range(3)]
        mx = jnp.maximum(jnp.maximum(ms[0], ms[1]), ms[2])
        ws = [jnp.exp2(m - mx) for m in ms]
        num = ws[0] * acc_ref[0, rows, :] + ws[1] * acc_ref[1, rows, :] + ws[2] * acc_ref[2, rows, :]
        den = ws[0] * den_ref[0, rows, :] + ws[1] * den_ref[1, rows, :] + ws[2] * den_ref[2, rows, :]
        o_ref[rows, :] = (num / den).astype(o_ref.dtype)
        return carry

    lax.fori_loop(0, n_blocks, combine, 0, unroll=4)


def _attn_bias_tables():
    hw = ATTN_HALF_WINDOW
    row = np.arange(Q_BLOCK)[:, None]
    col = np.arange(K_WINDOW)[None, :]

    def bias(off):
        return np.where(np.abs(off) <= hw, 0.0, NEG_INF).astype(np.float32)

    nat = np.stack([bias(col - shift - row) for shift in (0, hw, 2 * hw)])
    q_rows, k_rows = Q_BLOCK // DIL_RATIO, K_WINDOW // DIL_RATIO
    qpos = DIL_RATIO * (row % q_rows) + row // q_rows
    kpos = DIL_RATIO * (col % k_rows) + col // k_rows
    mid = np.stack([bias(kpos - shift - qpos) for shift in (0, hw, 2 * hw)])
    wide = bias(np.arange(Q_BLOCK)[None, :] - row)
    return nat, mid, wide


def _attention(qkv, qkv16, *, batch, seq):
    n3, t, _ = qkv.shape
    n_pairs = n3 // 3
    n_max = seq // DIL_MAX
    assert DILATIONS[0] == 1 and n_max == Q_BLOCK and seq % (DIL_MID * Q_BLOCK) == 0
    nat, mid, wide = _attn_bias_tables()
    in_specs = ([pl.BlockSpec((None, seq, LANES), lambda b, hp, w=w: (w * n_pairs + hp, b, 0))
                 for w in range(3)]
                + [pl.BlockSpec((None, n_max, DIL_MAX * LANES), lambda b, hp, w=w: (w * n_pairs + hp, b, 0))
                   for w in range(3)]
                + [pl.BlockSpec(nat.shape, lambda b, hp: (0, 0, 0)),
                   pl.BlockSpec(mid.shape, lambda b, hp: (0, 0, 0)),
                   pl.BlockSpec(wide.shape, lambda b, hp: (0, 0))])
    scratch = [pltpu.VMEM((3, seq, LANES), F32) for _ in range(3)]
    return pl.pallas_call(
        functools.partial(_attn_kernel, seq=seq),
        grid=(batch, n_pairs),
        in_specs=in_specs,
        out_specs=pl.BlockSpec((seq, LANES), lambda b, hp: (b, hp)),
        out_shape=jax.ShapeDtypeStruct((t, n_pairs * LANES), BF16),
        scratch_shapes=scratch,
        compiler_params=_cparams(2),
        name="dilated_attn",
    )(qkv, qkv, qkv, qkv16, qkv16, qkv16, nat, mid, wide)


GATE_GROUP = 16
GATE_ROWS = 3 * GATE_GROUP


def _gateprep_kernel(g_ref, o_ref, orow_ref, *, n_heads):
    ch = MLSTM_CHUNK
    src = lax.broadcasted_iota(jnp.int32, (ch, ch), 0)
    dst = lax.broadcasted_iota(jnp.int32, (ch, ch), 1)
    tri_prefix = (src <= dst).astype(F32)
    tri_suffix = (src >= dst).astype(F32)
    n_c = g_ref.shape[0] // ch
    grow = lax.broadcasted_iota(jnp.int32, (n_c * GATE_GROUP, ch), 0)
    pos = lax.broadcasted_iota(jnp.int32, (n_c * GATE_GROUP, ch), 1)
    is_fwd = (grow % GATE_GROUP) < 2 * n_heads
    pad = jnp.zeros((LANES - GATE_ROWS, ch), F32)

    g = jnp.concatenate([g_ref[c * ch:(c + 1) * ch, :].T[:GATE_GROUP, :] for c in range(n_c)], axis=0)
    log_f = jnp.minimum(g, 0.0) - jnp.log(1.0 + jnp.exp(-jnp.abs(g)))
    pre = jnp.dot(log_f, tri_prefix, preferred_element_type=F32, precision=lax.Precision.HIGHEST)
    suf = jnp.dot(log_f, tri_suffix, preferred_element_type=F32, precision=lax.Precision.HIGHEST)
    b = jnp.where(is_fwd, pre, suf)
    a = pltpu.roll(g, n_heads, 0) - b
    cm = a
    step = 1
    while step < ch:
        ahead = jnp.where(pos >= step, pltpu.roll(cm, step, 1), -jnp.inf)
        behind = jnp.where(pos + step < ch, pltpu.roll(cm, ch - step, 1), -jnp.inf)
        cm = jnp.maximum(cm, jnp.where(is_fwd, ahead, behind))
        step *= 2
    for c in range(n_c):
        part = slice(c * GATE_GROUP, (c + 1) * GATE_GROUP)
        rows = jnp.concatenate([b[part], a[part], cm[part]], axis=0) * LOG2_E
        orow_ref[c] = rows
        o_ref[c * ch:(c + 1) * ch, :] = jnp.concatenate([rows, pad], axis=0).T


def _gateprep(gates, *, n_heads, tm=2048):
    t, w = gates.shape
    assert 4 * n_heads == GATE_GROUP and GATE_ROWS <= w
    cpt = tm // MLSTM_CHUNK
    return pl.pallas_call(
        functools.partial(_gateprep_kernel, n_heads=n_heads),
        grid=(t // tm,),
        in_specs=[pl.BlockSpec((tm, w), lambda i: (i, 0))],
        out_specs=[pl.BlockSpec((tm, w), lambda i: (i, 0)),
                   pl.BlockSpec((cpt, GATE_ROWS, MLSTM_CHUNK), lambda i: (i, 0, 0))],
        out_shape=[jax.ShapeDtypeStruct((t, w), F32),
                   jax.ShapeDtypeStruct((t // MLSTM_CHUNK, GATE_ROWS, MLSTM_CHUNK), F32)],
        compiler_params=_cparams(1),
        name="gateprep",
    )(gates)


CONV_HALO = 16


def _conv_silu_chunks(x_ref, w_ref, b_ref, stage_ref, emit):
    n = x_ref.shape[0]
    taps = w_ref.shape[0]
    pad = taps // 2
    zeros = jnp.zeros((CONV_HALO, x_ref.shape[1]), F32)
    stage_ref[0:CONV_HALO, :] = zeros
    stage_ref[CONV_HALO + n:, :] = zeros
    stage_ref[CONV_HALO:CONV_HALO + n, :] = x_ref[...].astype(F32)

    def body(c, carry):
        base = pl.multiple_of(c * MLSTM_CHUNK, MLSTM_CHUNK)
        acc = b_ref[...]
        for j in range(taps):
            acc = acc + stage_ref[pl.ds(base + (CONV_HALO + j - pad), MLSTM_CHUNK), :] * w_ref[j:j + 1, :]
        half = 0.5 * acc
        emit(c, half + half * jnp.tanh(half))
        return carry

    lax.fori_loop(0, n // MLSTM_CHUNK, body, 0, unroll=8)


def _mlstm_kernel(mq_ref, mk_ref, mv_ref, mo_ref, gc_ref, gr_ref, cwq_ref, cwk_ref, cbq_ref, cbk_ref,
                  hg_ref, o_ref, q_s, kt_s, vx_s, gc_s, sc_s, hf_s, hb_s, stage_s, *, n_heads):
    head = pl.program_id(1)
    seq, dh = q_s.shape
    ch = MLSTM_CHUNK
    n_chunks = seq // ch
    def emit_q(c, y):
        q_s[pl.ds(pl.multiple_of(c * ch, ch), ch), :] = y.astype(q_s.dtype)

    def emit_k(c, y):
        kt_s[:, pl.ds(pl.multiple_of(c * ch, ch), ch)] = (y * (dh ** -0.5)).astype(kt_s.dtype).T

    _conv_silu_chunks(mq_ref, cwq_ref, cbq_ref, stage_s, emit_q)
    _conv_silu_chunks(mk_ref, cwk_ref, cbk_ref, stage_s, emit_k)
    vx_s[:, :dh] = mv_ref[...]
    vx_s[:, dh:] = jnp.ones((seq, dh), vx_s.dtype)
    gc_s[...] = pltpu.roll(gc_ref[...], (LANES - head) % LANES, 1)

    def gate_lane(group, d):
        return group * GATE_GROUP + (2 * d + 1) * n_heads

    for d in range(2):
        lane_b, lane_cm = gate_lane(0, d), gate_lane(2, d)
        m_run = jnp.zeros((1, 2 * dh), F32)
        for step in range(n_chunks):
            c = step if d == 0 else n_chunks - 1 - step
            last = c * ch + (ch - 1 if d == 0 else 0)
            b_last = jnp.broadcast_to(gc_s[last:last + 1, lane_b:lane_b + 1], (1, 2 * dh))
            a_max = jnp.broadcast_to(gc_s[last:last + 1, lane_cm:lane_cm + 1], (1, 2 * dh))
            w_max = jnp.maximum(m_run, a_max)
            sc_s[d, 0, c:c + 1, :] = m_run
            sc_s[d, 1, c:c + 1, :] = w_max
            sc_s[d, 2, c:c + 1, :] = jnp.exp2(m_run - w_max)
            m_run = b_last + w_max

    row = lax.broadcasted_iota(jnp.int32, (ch, ch), 0)
    col = lax.broadcasted_iota(jnp.int32, (ch, ch), 1)

    def direction(c, d, mask, state, h_store):
        lane_b, lane_cm = gate_lane(0, d), gate_lane(2, d)
        rs = pl.multiple_of(c * ch, ch)
        qc = q_s[pl.ds(rs, ch), :]
        ktc = kt_s[:, pl.ds(rs, ch)]
        vx = vx_s[pl.ds(rs, ch), :]
        a_row = gr_ref[c, pl.ds(gate_lane(1, d) + head, 1), :]
        m_row = sc_s[d, 0, pl.ds(c, 1), :]
        w_max = sc_s[d, 1, pl.ds(c, 1), :]
        decay = sc_s[d, 2, pl.ds(c, 1), :]
        g = gc_s[pl.ds(rs, ch), :]
        b_bc = jnp.broadcast_to(g[:, lane_b:lane_b + 1], (ch, dh))
        cm_bc = jnp.broadcast_to(g[:, lane_cm:lane_cm + 1], (ch, dh))
        mm = jnp.maximum(m_row[:, :dh], cm_bc)
        w = jnp.where(mask, jnp.exp2(a_row - mm), 0.0)
        s = jnp.dot(qc, ktc, preferred_element_type=F32)
        inter = jnp.dot(qc, state.astype(qc.dtype), preferred_element_type=F32)
        kw_t = (ktc.astype(F32) * jnp.exp2(a_row - w_max[:, :ch])).astype(vx.dtype)
        new_state = decay * state + jnp.dot(kw_t, vx, preferred_element_type=F32)
        intra = jnp.dot((s * w).astype(vx.dtype), vx, preferred_element_type=F32)
        g_int = jnp.exp2(m_row[:, :dh] - mm)
        num = intra[:, :dh] + g_int * inter[:, :dh]
        den = intra[:, dh:] + g_int * inter[:, dh:]
        h_store[pl.ds(rs, ch), :] = num / jnp.maximum(jnp.abs(den), jnp.exp2(-(b_bc + mm)))
        return new_state

    def step(j, carry):
        fwd, bwd = carry
        fwd = direction(j, 0, col <= row, fwd, hf_s)
        bwd = direction(n_chunks - 1 - j, 1, col >= row, bwd, hb_s)
        return fwd, bwd

    zero = jnp.zeros((dh, 2 * dh), F32)
    lax.fori_loop(0, n_chunks, step, (zero, zero), unroll=16)

    hm = hf_s[...] + hb_s[...]
    hm = hm * lax.rsqrt(jnp.mean(hm * hm, axis=-1, keepdims=True) + NORM_EPS)
    o_ref[...] = (hm * hg_ref[...] * jax.nn.sigmoid(mo_ref[...].astype(F32))).astype(o_ref.dtype)


def _mlstm(mall, gc, gr, conv_w, conv_b, head_g, *, batch, seq, n_heads):
    t = mall.shape[0]
    dh = mall.shape[1] // (4 * n_heads)
    taps = conv_w.shape[0]
    n_chunks = seq // MLSTM_CHUNK
    assert dh == MLSTM_CHUNK == LANES

    def col_block(offset):
        return pl.BlockSpec((seq, dh), lambda b, h, offset=offset: (b, offset * n_heads + h))

    in_specs = [col_block(0), col_block(1), col_block(2), col_block(3),
                pl.BlockSpec((seq, LANES), lambda b, h: (b, 0)),
                pl.BlockSpec((n_chunks, GATE_ROWS, MLSTM_CHUNK), lambda b, h: (b, 0, 0)),
                pl.BlockSpec((taps, dh), lambda b, h: (0, h)),
                pl.BlockSpec((taps, dh), lambda b, h: (0, n_heads + h)),
                pl.BlockSpec((1, dh), lambda b, h: (0, h)),
                pl.BlockSpec((1, dh), lambda b, h: (0, n_heads + h)),
                pl.BlockSpec((1, dh), lambda b, h: (0, h))]
    scratch = [pltpu.VMEM((seq, dh), BF16),
               pltpu.VMEM((dh, seq), BF16),
               pltpu.VMEM((seq, 2 * dh), BF16),
               pltpu.VMEM((seq, LANES), F32),
               pltpu.VMEM((2, 3, n_chunks, 2 * dh), F32),
               pltpu.VMEM((seq, dh), F32), pltpu.VMEM((seq, dh), F32),
               pltpu.VMEM((seq + 2 * CONV_HALO, dh), F32)]
    return pl.pallas_call(
        functools.partial(_mlstm_kernel, n_heads=n_heads),
        grid=(batch, n_heads),
        in_specs=in_specs,
        out_specs=pl.BlockSpec((seq, dh), lambda b, h: (b, h)),
        out_shape=jax.ShapeDtypeStruct((t, n_heads * dh), BF16),
        scratch_shapes=scratch,
        compiler_params=_cparams(2),
        name="mlstm",
    )(mall, mall, mall, mall, gc, gr, conv_w, conv_w, conv_b, conv_b, head_g)


def _outproj_kernel(a_ref, m_ref, w_ref, x_ref, gpost_ref, gpre_ref, x1_ref, hn_ref):
    ka = a_ref.shape[1]
    mixed = (jnp.dot(a_ref[...], w_ref[:ka, :], preferred_element_type=F32)
             + jnp.dot(m_ref[...], w_ref[ka:, :], preferred_element_type=F32))
    x1 = x_ref[...] + _rms(mixed, gpost_ref[...])
    x1_ref[...] = x1
    hn_ref[...] = _rms(x1, gpre_ref[...]).astype(hn_ref.dtype)


def _outproj(attn, ml, w_out, layer, x, g_post, g_pre, tm=1024):
    t, d = x.shape
    ka, km = attn.shape[1], ml.shape[1]
    return pl.pallas_call(
        _outproj_kernel,
        grid=(t // tm,),
        in_specs=[pl.BlockSpec((tm, ka), lambda i: (i, 0)),
                  pl.BlockSpec((tm, km), lambda i: (i, 0)),
                  pl.BlockSpec((None, ka + km, d), lambda i: (layer, 0, 0)),
                  pl.BlockSpec((tm, d), lambda i: (i, 0)),
                  pl.BlockSpec((1, d), lambda i: (0, 0)),
                  pl.BlockSpec((1, d), lambda i: (0, 0))],
        out_specs=[pl.BlockSpec((tm, d), lambda i: (i, 0)),
                   pl.BlockSpec((tm, d), lambda i: (i, 0))],
        out_shape=[jax.ShapeDtypeStruct((t, d), F32), jax.ShapeDtypeStruct((t, d), BF16)],
        compiler_params=_cparams(1),
        name="outproj",
    )(attn, ml, w_out, x, g_post.reshape(1, d), g_pre.reshape(1, d))


FFN_HALO = 16
FFN_CHUNK = 256
FFN_STAGES = 4


def _ffn_kernel(prev_ref, main_ref, next_ref, x1_ref, wup_ref, wdn_ref, cw_ref, cb_ref,
                gpost_ref, x2_ref, *scratch, tiles_per_seq, n_tiles):
    ug_refs = scratch[:FFN_STAGES]
    uv_refs = scratch[FFN_STAGES:2 * FFN_STAGES]
    acc_ref = scratch[-1]
    i = pl.program_id(0)
    tm = main_ref.shape[0]
    d_ff = wdn_ref.shape[0]

    @pl.when(i == 0)
    def _():
        acc_ref[1] = jnp.zeros(acc_ref.shape[1:], acc_ref.dtype)

    cur = i % 2
    x2 = x1_ref[...] + _rms(acc_ref[1 - cur], gpost_ref[...])
    x2_ref[...] = x2
    spread = pltpu.bitcast(jnp.max(x2, axis=0, keepdims=True), jnp.uint32)
    epilogue_done = pltpu.bitcast((spread >> 16) >> 16, F32)

    pos = jnp.minimum(i, n_tiles - 1) % tiles_per_seq
    prev = jnp.where(pos == 0, jnp.zeros_like(prev_ref[...]), prev_ref[...])
    nxt = jnp.where(pos == tiles_per_seq - 1, jnp.zeros_like(next_ref[...]), next_ref[...])
    lhs = jnp.concatenate([prev, main_ref[...], nxt], axis=0)
    taps = cw_ref.shape[0]

    row0 = pl.multiple_of(jnp.minimum(i, 0), FFN_HALO)

    def stage(u_ref, u):
        for k in range(FFN_CHUNK // LANES):
            u_ref[k] = u[:, k * LANES:(k + 1) * LANES]

    def conv(u_ref, k, c0):
        cols = slice(c0 + k * LANES, c0 + (k + 1) * LANES)
        out = cb_ref[:, cols]
        for j in range(taps):
            start = FFN_HALO + j - taps // 2
            out = out + u_ref[k, pl.ds(row0 + start, tm), :] * cw_ref[j:j + 1, cols]
        return out

    gelu_c0 = float(np.sqrt(2.0 / np.pi))
    gelu_c1 = float(np.sqrt(2.0 / np.pi) * 0.044715)
    n_chunks = d_ff // FFN_CHUNK

    def up_project(c):
        c0 = c * FFN_CHUNK
        gate_proj = jnp.dot(lhs, wup_ref[:, c0:c0 + FFN_CHUNK], preferred_element_type=F32)
        if c == 1:
            gate_proj = gate_proj + epilogue_done[:, :FFN_CHUNK]
        stage(ug_refs[c % FFN_STAGES], gate_proj)
        stage(uv_refs[c % FFN_STAGES], jnp.dot(lhs, wup_ref[:, d_ff + c0:d_ff + c0 + FFN_CHUNK],
                                               preferred_element_type=F32))

    for c in range(FFN_STAGES - 1):
        up_project(c)
    for c in range(n_chunks):
        c0 = c * FFN_CHUNK
        slot = c % FFN_STAGES
        if c + FFN_STAGES - 1 < n_chunks:
            up_project(c + FFN_STAGES - 1)
        acts = []
        for k in range(FFN_CHUNK // LANES):
            gate = conv(ug_refs[slot], k, c0)
            val = conv(uv_refs[slot], k, d_ff + c0)
            half = (0.5 * gate) * val
            inner = gate * (gelu_c0 + gelu_c1 * (gate * gate))
            acts.append((half + half * jnp.tanh(inner)).astype(lhs.dtype))
        act = jnp.concatenate(acts, axis=1)
        part = jnp.dot(act, wdn_ref[c0:c0 + FFN_CHUNK, :], preferred_element_type=F32)
        if c == 0:
            acc_ref[cur] = part
        else:
            acc_ref[cur] += part


def _ffn(hn, x1, w_up, w_down, layer, conv_w, conv_b, g_post, *, seq, tm=512):
    t, d = x1.shape
    d_ff = w_down.shape[1]
    n_tiles = t // tm
    tiles_per_seq = seq // tm
    halo_per_tile = tm // FFN_HALO
    n_halo_blocks = t // FFN_HALO

    def tile(i):
        return jnp.minimum(i, n_tiles - 1)

    def done(i):
        return jnp.maximum(i - 1, 0)

    return pl.pallas_call(
        functools.partial(_ffn_kernel, tiles_per_seq=tiles_per_seq, n_tiles=n_tiles),
        grid=(n_tiles + 1,),
        in_specs=[pl.BlockSpec((FFN_HALO, d), lambda i: (jnp.maximum(tile(i) * halo_per_tile - 1, 0), 0)),
                  pl.BlockSpec((tm, d), lambda i: (tile(i), 0)),
                  pl.BlockSpec((FFN_HALO, d),
                               lambda i: (jnp.minimum((tile(i) + 1) * halo_per_tile, n_halo_blocks - 1), 0)),
                  pl.BlockSpec((tm, d), lambda i: (done(i), 0)),
                  pl.BlockSpec((None, d, 2 * d_ff), lambda i: (layer, 0, 0), pipeline_mode=pl.Buffered(1)),
                  pl.BlockSpec((None, d_ff, d), lambda i: (layer, 0, 0), pipeline_mode=pl.Buffered(1)),
                  pl.BlockSpec(conv_w.shape, lambda i: (0, 0)),
                  pl.BlockSpec((1, 2 * d_ff), lambda i: (0, 0)),
                  pl.BlockSpec((1, d), lambda i: (0, 0))],
        out_specs=pl.BlockSpec((tm, d), lambda i: (done(i), 0)),
        out_shape=jax.ShapeDtypeStruct((t, d), F32),
        scratch_shapes=([pltpu.VMEM((FFN_CHUNK // LANES, tm + 2 * FFN_HALO, LANES), F32)
                         for _ in range(2 * FFN_STAGES)]
                        + [pltpu.VMEM((2, tm, d), F32)]),
        compiler_params=_cparams(1),
        name="convffn",
    )(hn, hn, hn, x1, w_up, w_down, conv_w, conv_b.reshape(1, -1), g_post.reshape(1, d))


def _rotary_tables(seq):
    half = ATTN_HEAD_DIM // 2
    inv_freq = ROPE_THETA ** (-jnp.arange(0, ATTN_HEAD_DIM, 2, dtype=F32) / ATTN_HEAD_DIM)
    ang = jnp.arange(seq, dtype=F32)[:, None] * inv_freq[None, :]
    cos, sin = jnp.cos(ang), jnp.sin(ang)
    reps = LANES // ATTN_HEAD_DIM
    cos_t = jnp.tile(jnp.concatenate([cos, cos], axis=-1), (1, reps))
    sin_t = jnp.tile(jnp.concatenate([-sin, sin], axis=-1), (1, reps))
    return cos_t, sin_t


def kernel(x, mix_pre_g, w_in, mlstm_conv_w, mlstm_conv_b, mlstm_gate_b, mlstm_head_g, w_out,
           mix_post_g, ffn_pre_g, w_up, ffn_conv_w, ffn_conv_b, w_down, ffn_post_g):
    batch, seq, d = x.shape
    depth = w_in.shape[0]
    t = batch * seq
    n_heads = MLSTM_HEADS
    n_gates = mlstm_gate_b.shape[1]
    mix_w = mlstm_head_g.shape[1]
    attn_w = w_out.shape[1] - mix_w
    assert n_gates == 4 * n_heads and n_gates <= LANES
    assert w_in.shape[2] == 3 * attn_w + 4 * mix_w + n_gates

    cos_t, sin_t = _rotary_tables(seq)
    xf = x.reshape(t, d)
    w_out_b, w_up_b, w_down_b = w_out.astype(BF16), w_up.astype(BF16), w_down.astype(BF16)
    for l in range(depth):
        gate_b = jnp.pad(mlstm_gate_b[l], (0, LANES - n_gates)).reshape(1, LANES)
        qkv, qkv16, mall, gates = _inproj(xf, mix_pre_g[l], w_in, l, gate_b, cos_t, sin_t,
                                          attn_w=attn_w, mix_w=mix_w, seq=seq)
        attn = _attention(qkv, qkv16, batch=batch, seq=seq)
        gc, gr = _gateprep(gates, n_heads=n_heads)
        ml = _mlstm(mall, gc, gr, mlstm_conv_w[l], mlstm_conv_b[l].reshape(1, -1),
                    mlstm_head_g[l].reshape(1, -1), batch=batch, seq=seq, n_heads=n_heads)
        x1, hn2 = _outproj(attn, ml, w_out_b, l, xf, mix_post_g[l], ffn_pre_g[l])
        xf = _ffn(hn2, x1, w_up_b, w_down_b, l, ffn_conv_w[l],
                  ffn_conv_b[l], ffn_post_g[l], seq=seq)
    return xf.reshape(batch, seq, d)
```
